```python
import numpy as np
import jax
import jax.numpy as jnp
from jax import lax


D_MODEL = 2048
BATCH = 2
SEQ = 4096
DEPTH = 1

MEM_LEN = 256
RMS_EPS = 1e-6
MLA_HEADS = 16
Q_LORA = 512
KV_LORA = 256
NOPE_DIM = 128
ROPE_DIM = 64
V_DIM = 128
QK_DIM = NOPE_DIM + ROPE_DIM
MLA_WIDTH = MLA_HEADS * V_DIM
ROPE_BASE = 10000.0
Q_BLOCK = 128
RWKV_HEAD = 64
RWKV_HEADS = D_MODEL // RWKV_HEAD
RWKV_WIDTH = RWKV_HEADS * RWKV_HEAD
DECAY_LORA = 96
ICLR_LORA = 96
GATE_LORA = 256
SHIFT_WIDTH = 3
LNX_EPS = 64e-5
CROSS_HEADS = 4
CROSS_HEAD_DIM = 128
CROSS_WIDTH = CROSS_HEADS * CROSS_HEAD_DIM
N_GROUPS = 4
EXPERTS_PER_GROUP = 8
TOP_K_IN_GROUP = 2
D_EXPERT = 512

MLA_IN_SPLITS = (Q_LORA, KV_LORA, ROPE_DIM)
RWKV_IN_SPLITS = (RWKV_WIDTH, RWKV_WIDTH, RWKV_WIDTH, DECAY_LORA, DECAY_LORA, ICLR_LORA, ICLR_LORA, GATE_LORA)
MLA_IN = Q_LORA + KV_LORA + ROPE_DIM
RWKV_IN = 3 * RWKV_WIDTH + 2 * DECAY_LORA + 2 * ICLR_LORA + GATE_LORA
GATE_IN = 2 * D_MODEL
IN_WIDTH = MLA_IN + RWKV_IN + GATE_IN

kernel_name = 'hybrid_mla_rwkv7_hmoe_encoder_block'


def _split_cols(z, sizes):
    idx = np.cumsum(np.array(sizes))[:-1].tolist()
    return jnp.split(z, idx, axis=-1)


def rms_norm(x, g):
    xf = x.astype(jnp.float32)
    y = xf * lax.rsqrt(jnp.mean(xf * xf, axis=-1, keepdims=True) + RMS_EPS)
    return (y * g.astype(jnp.float32)).astype(x.dtype)


def rope(x, positions):
    half = ROPE_DIM // 2
    inv_freq = ROPE_BASE ** (-jnp.arange(half, dtype=jnp.float32) * (2.0 / ROPE_DIM))
    ang = positions.astype(jnp.float32)[..., None] * inv_freq
    ang = ang.reshape(ang.shape[:2] + (1,) * (x.ndim - 3) + (half,))
    cos, sin = jnp.cos(ang), jnp.sin(ang)
    xf = x.astype(jnp.float32)
    x1, x2 = xf[..., :half], xf[..., half:]
    return jnp.concatenate([x1 * cos - x2 * sin, x2 * cos + x1 * sin], axis=-1).astype(x.dtype)


def centred_shift(z, w):
    zp = jnp.pad(z, ((0, 0), (1, 1), (0, 0)))
    return zp[:, :-2] * w[0] + zp[:, 1:-1] * w[1] + zp[:, 2:] * w[2]


def mla_branch(z_mla, positions, g_q, w_uq, g_kv, w_ukv):
    bsz, seq, _ = z_mla.shape
    c_q, c_kv, k_r = _split_cols(z_mla, MLA_IN_SPLITS)
    q = (rms_norm(c_q, g_q) @ w_uq).reshape(bsz, seq, MLA_HEADS, QK_DIM)
    q_nope = q[..., :NOPE_DIM]
    q_rope = rope(q[..., NOPE_DIM:], positions)
    kv = (rms_norm(c_kv, g_kv) @ w_ukv).reshape(bsz, seq, MLA_HEADS, NOPE_DIM + V_DIM)
    k_nope, v = kv[..., :NOPE_DIM], kv[..., NOPE_DIM:]
    k_rope = rope(k_r, positions)
    scale = QK_DIM ** -0.5
    nblk = seq // Q_BLOCK
    qn_blk = jnp.swapaxes(q_nope.reshape(bsz, nblk, Q_BLOCK, MLA_HEADS, NOPE_DIM), 0, 1)
    qr_blk = jnp.swapaxes(q_rope.reshape(bsz, nblk, Q_BLOCK, MLA_HEADS, ROPE_DIM), 0, 1)

    def attend(blk):
        qn, qr = blk
        s = jnp.einsum('bqhd,bkhd->bhqk', qn, k_nope) + jnp.einsum('bqhr,bkr->bhqk', qr, k_rope)
        p = jax.nn.softmax(s.astype(jnp.float32) * scale, axis=-1).astype(v.dtype)
        return jnp.einsum('bhqk,bkhd->bqhd', p, v)

    o = lax.map(attend, (qn_blk, qr_blk))
    return jnp.swapaxes(o, 0, 1).reshape(bsz, seq, MLA_WIDTH)


def rwkv7_scan(r, w, k, v, a, b, reverse):
    bsz, _, nh, hd = r.shape

    def step(state, inp):
        r_t, w_t, k_t, v_t, a_t, b_t = inp
        sa = jnp.einsum('bhvk,bhk->bhv', state, a_t)
        state = state * w_t[:, :, None, :] + sa[..., None] * b_t[:, :, None, :] + v_t[..., None] * k_t[:, :, None, :]
        return state, jnp.einsum('bhvk,bhk->bhv', state, r_t)

    xs = tuple(jnp.swapaxes(t, 0, 1) for t in (r, w, k, v, a, b))
    s0 = jnp.zeros((bsz, nh, hd, hd), jnp.float32)
    _, ys = lax.scan(step, s0, xs, reverse=reverse)
    return jnp.swapaxes(ys, 0, 1)


def rwkv7_branch(z_rwkv, shift_conv, w0_f, w2_f, w0_b, w2_b, a0_f, a2_f, a0_b, a2_b, g2, k_k, k_a, r_k, lnx_g, lnx_b):
    bsz, seq, _ = z_rwkv.shape
    z = centred_shift(z_rwkv, shift_conv)
    r, k, v, xw_f, xw_b, xa_f, xa_b, xg = _split_cols(z, RWKV_IN_SPLITS)

    def heads(t):
        return t.reshape(bsz, seq, RWKV_HEADS, RWKV_HEAD).astype(jnp.float32)

    r_h, v_h, k_h = heads(r), heads(v), heads(k)
    kk = heads(k * k_k)
    kk = kk * lax.rsqrt(jnp.sum(kk * kk, axis=-1, keepdims=True) + 1e-12)
    k_a_h = k_a.reshape(RWKV_HEADS, RWKV_HEAD).astype(jnp.float32)
    r_k_f = r_k.astype(jnp.float32)
    g = jax.nn.sigmoid(xg) @ g2
    outs = []
    for xw, w0, w2, xa, a0, a2, rev in ((xw_f, w0_f, w2_f, xa_f, a0_f, a2_f, False),
                                         (xw_b, w0_b, w2_b, xa_b, a0_b, a2_b, True)):
        w_log = -jax.nn.softplus(-(w0 + jnp.tanh(xw) @ w2)) - 0.5
        decay = jnp.exp(-jnp.exp(heads(w_log)))
        a = jax.nn.sigmoid(heads(a0 + xa @ a2))
        k_d = k_h * (1.0 + (a - 1.0) * k_a_h)
        y_d = rwkv7_scan(r_h, decay, k_d, v_h, -kk, kk * a, rev)
        bonus = jnp.sum(r_h * k_d * r_k_f, axis=-1, keepdims=True) * v_h
        outs.append(y_d + bonus)
    y = outs[0] + outs[1]
    mu = jnp.mean(y, axis=-1, keepdims=True)
    var = jnp.mean(jnp.square(y - mu), axis=-1, keepdims=True)
    y = ((y - mu) * lax.rsqrt(var + LNX_EPS)).reshape(bsz, seq, RWKV_WIDTH)
    y = y * lnx_g.astype(jnp.float32) + lnx_b.astype(jnp.float32)
    return (y * g.astype(jnp.float32)).astype(z_rwkv.dtype)


def memory_cross_attention(h_n, mem, g_mem, wq_c, wkv_c, wo_c):
    bsz, seq, _ = h_n.shape
    m_len = mem.shape[1]
    q = (h_n @ wq_c).reshape(bsz, seq, CROSS_HEADS, CROSS_HEAD_DIM)
    kv = (rms_norm(mem, g_mem) @ wkv_c).reshape(bsz, m_len, 2, CROSS_HEADS, CROSS_HEAD_DIM)
    k, v = kv[:, :, 0], kv[:, :, 1]
    s = jnp.einsum('bqhd,bmhd->bhqm', q, k).astype(jnp.float32) * (CROSS_HEAD_DIM ** -0.5)
    p = jax.nn.softmax(s, axis=-1).astype(v.dtype)
    o = jnp.einsum('bhqm,bmhd->bqhd', p, v).reshape(bsz, seq, CROSS_WIDTH)
    return o @ wo_c


def hierarchical_moe(n, w_rg, b_rg, w_re, b_re, w_eg, w_eu, w_ed):
    bsz, seq, dm = n.shape
    t = n.reshape(bsz * seq, dm)
    tf = t.astype(jnp.float32)
    g_prob = jax.nn.softmax(tf @ w_rg.astype(jnp.float32) + b_rg.astype(jnp.float32), axis=-1)
    p_group, g_idx = lax.top_k(g_prob, 1)
    g_onehot = jax.nn.one_hot(g_idx[:, 0], N_GROUPS, dtype=jnp.float32)
    e_logits_all = jnp.einsum('nd,gde->nge', tf, w_re.astype(jnp.float32)) + b_re.astype(jnp.float32)
    e_logits = jnp.einsum('ng,nge->ne', g_onehot, e_logits_all)
    e_prob = jax.nn.softmax(e_logits, axis=-1)
    p_top, e_idx = lax.top_k(e_prob, TOP_K_IN_GROUP)
    p_top = p_top / jnp.sum(p_top, axis=-1, keepdims=True)
    w_tok = p_group * p_top
    within = jnp.einsum('nk,nke->ne', w_tok, jax.nn.one_hot(e_idx, EXPERTS_PER_GROUP, dtype=jnp.float32))
    combine = (g_onehot[:, :, None] * within[:, None, :]).astype(t.dtype)
    y = jnp.zeros_like(t)
    for gi in range(N_GROUPS):
        hid = jax.nn.silu(jnp.einsum('nd,edf->nef', t, w_eg[gi])) * jnp.einsum('nd,edf->nef', t, w_eu[gi])
        y = y + jnp.einsum('nef,efd->nd', hid * combine[:, gi, :, None], w_ed[gi])
    return y.reshape(bsz, seq, dm)


def setup_inputs(seed: int = 0) -> dict:
    key = jax.random.key(seed)
    keys = jax.random.split(key, 64)
    counter = [0]

    def nrm(shape, scale):
        k = keys[counter[0]]
        counter[0] += 1
        return scale * jax.random.normal(k, shape, jnp.float32)

    def gain(shape):
        return 1.0 + nrm(shape, 0.1)

    L = DEPTH
    E = EXPERTS_PER_GROUP
    x = nrm((BATCH, SEQ, D_MODEL), 1.0)
    mem = nrm((BATCH, MEM_LEN, D_MODEL), 1.0)
    positions = jnp.broadcast_to(jnp.arange(SEQ, dtype=jnp.int32)[None, :], (BATCH, SEQ))
    return {
        'x': x,
        'mem': mem,
        'positions': positions,
        'g_mix': gain((L, D_MODEL)),
        'w_in': nrm((L, D_MODEL, IN_WIDTH), D_MODEL ** -0.5),
        'g_q': gain((L, Q_LORA)),
        'w_uq': nrm((L, Q_LORA, MLA_HEADS * QK_DIM), Q_LORA ** -0.5),
        'g_kv': gain((L, KV_LORA)),
        'w_ukv': nrm((L, KV_LORA, MLA_HEADS * (NOPE_DIM + V_DIM)), KV_LORA ** -0.5),
        'shift_conv': jnp.array([0.25, 0.5, 0.25], jnp.float32)[None, :, None] + nrm((L, SHIFT_WIDTH, RWKV_IN), 0.05),
        'w0_f': nrm((L, RWKV_WIDTH), 0.5),
        'w2_f': nrm((L, DECAY_LORA, RWKV_WIDTH), 0.5 * DECAY_LORA ** -0.5),
        'w0_b': nrm((L, RWKV_WIDTH), 0.5),
        'w2_b': nrm((L, DECAY_LORA, RWKV_WIDTH), 0.5 * DECAY_LORA ** -0.5),
        'a0_f': nrm((L, RWKV_WIDTH), 0.5),
        'a2_f': nrm((L, ICLR_LORA, RWKV_WIDTH), 0.5 * ICLR_LORA ** -0.5),
        'a0_b': nrm((L, RWKV_WIDTH), 0.5),
        'a2_b': nrm((L, ICLR_LORA, RWKV_WIDTH), 0.5 * ICLR_LORA ** -0.5),
        'g2': nrm((L, GATE_LORA, RWKV_WIDTH), GATE_LORA ** -0.5),
        'k_k': gain((L, RWKV_WIDTH)),
        'k_a': 0.5 + nrm((L, RWKV_WIDTH), 0.1),
        'r_k': nrm((L, RWKV_HEADS, RWKV_HEAD), 0.5),
        'lnx_g': gain((L, RWKV_WIDTH)),
        'lnx_b': nrm((L, RWKV_WIDTH), 0.01),
        'w_up_mla': nrm((L, MLA_WIDTH, D_MODEL), MLA_WIDTH ** -0.5),
        'w_up_rwkv': nrm((L, RWKV_WIDTH, D_MODEL), RWKV_WIDTH ** -0.5),
        'w_out': nrm((L, D_MODEL, D_MODEL), D_MODEL ** -0.5),
        'g_cross': gain((L, D_MODEL)),
        'g_mem': gain((L, D_MODEL)),
        'wq_c': nrm((L, D_MODEL, CROSS_WIDTH), D_MODEL ** -0.5),
        'wkv_c': nrm((L, D_MODEL, 2 * CROSS_WIDTH), D_MODEL ** -0.5),
        'wo_c': nrm((L, CROSS_WIDTH, D_MODEL), CROSS_WIDTH ** -0.5),
        'g_ffn': gain((L, D_MODEL)),
        'w_rg': nrm((L, D_MODEL, N_GROUPS), D_MODEL ** -0.5),
        'b_rg': nrm((L, N_GROUPS), 0.01),
        'w_re': nrm((L, N_GROUPS, D_MODEL, E), D_MODEL ** -0.5),
        'b_re': nrm((L, N_GROUPS, E), 0.01),
        'w_eg': nrm((L, N_GROUPS, E, D_MODEL, D_EXPERT), D_MODEL ** -0.5),
        'w_eu': nrm((L, N_GROUPS, E, D_MODEL, D_EXPERT), D_MODEL ** -0.5),
        'w_ed': nrm((L, N_GROUPS, E, D_EXPERT, D_MODEL), D_EXPERT ** -0.5),
        'g_final': gain((D_MODEL,)),
    }


def reference(x, mem, positions, g_mix, w_in, g_q, w_uq, g_kv, w_ukv, shift_conv,
              w0_f, w2_f, w0_b, w2_b, a0_f, a2_f, a0_b, a2_b, g2, k_k, k_a, r_k,
              lnx_g, lnx_b, w_up_mla, w_up_rwkv, w_out, g_cross, g_mem, wq_c, wkv_c,
              wo_c, g_ffn, w_rg, b_rg, w_re, b_re, w_eg, w_eu, w_ed, g_final):
    h = x
    for l in range(DEPTH):
        n = rms_norm(h, g_mix[l])
        z = n @ w_in[l]
        z_mla, z_rwkv, z_gate = _split_cols(z, (MLA_IN, RWKV_IN, GATE_IN))
        o_mla = mla_branch(z_mla, positions, g_q[l], w_uq[l], g_kv[l], w_ukv[l])
        o_rwkv = rwkv7_branch(z_rwkv, shift_conv[l], w0_f[l], w2_f[l], w0_b[l], w2_b[l],
                              a0_f[l], a2_f[l], a0_b[l], a2_b[l], g2[l], k_k[l], k_a[l],
                              r_k[l], lnx_g[l], lnx_b[l])
        gate_mla, gate_rwkv = jnp.split(jax.nn.sigmoid(z_gate), 2, axis=-1)
        merged = gate_mla * (o_mla @ w_up_mla[l]) + gate_rwkv * (o_rwkv @ w_up_rwkv[l])
        h = h + merged @ w_out[l]
        h = h + memory_cross_attention(rms_norm(h, g_cross[l]), mem, g_mem[l], wq_c[l], wkv_c[l], wo_c[l])
        h = h + hierarchical_moe(rms_norm(h, g_ffn[l]), w_rg[l], b_rg[l], w_re[l], b_re[l],
                                 w_eg[l], w_eu[l], w_ed[l])
    return rms_norm(h, g_final)
```

```python
import functools

import jax
import jax.numpy as jnp
import numpy as np
from jax import lax
from jax.experimental import pallas as pl
from jax.experimental.pallas import tpu as pltpu

F32 = jnp.float32
BF16 = jnp.bfloat16

D_MODEL = 2048
MEM_LEN = 256
RMS_EPS = 1e-6
MLA_HEADS = 16
Q_LORA = 512
KV_LORA = 256
NOPE_DIM = 128
ROPE_DIM = 64
V_DIM = 128
QK_DIM = NOPE_DIM + ROPE_DIM
ROPE_BASE = 10000.0
RWKV_HEAD = 64
RWKV_HEADS = D_MODEL // RWKV_HEAD
RWKV_WIDTH = D_MODEL
DECAY_LORA = 96
ICLR_LORA = 96
GATE_LORA = 256
LNX_EPS = 64e-5
CROSS_HEADS = 4
CROSS_HEAD_DIM = 128
CROSS_WIDTH = CROSS_HEADS * CROSS_HEAD_DIM
N_GROUPS = 4
EXPERTS_PER_GROUP = 8
N_EXPERTS = N_GROUPS * EXPERTS_PER_GROUP
D_EXPERT = 512
MLA_IN = Q_LORA + KV_LORA + ROPE_DIM
RWKV_IN = 3 * RWKV_WIDTH + 2 * DECAY_LORA + 2 * ICLR_LORA + GATE_LORA

LANES = 128
SUBLANES = 8
QK_PAD = 256
LORA_PAD = LANES
PAIR = LANES // RWKV_HEAD
N_PAIRS = RWKV_HEADS // PAIR
CHUNK = 64
VMEM_LIMIT = 56 * 1024 * 1024


def _cparams(sem, vmem=VMEM_LIMIT):
    return pltpu.CompilerParams(dimension_semantics=sem, vmem_limit_bytes=vmem)


def _iota(shape, dim):
    return lax.broadcasted_iota(jnp.int32, shape, dim)


def _sigmoid(x):
    return 1.0 / (1.0 + jnp.exp(-x))


def _dot(a, b):
    return jnp.dot(a, b, preferred_element_type=F32)


def _dot_nt(a, b):
    return lax.dot_general(a, b, (((1,), (1,)), ((), ())), preferred_element_type=F32)


def _dot_tn(a, b):
    return lax.dot_general(a, b, (((0,), (0,)), ((), ())), preferred_element_type=F32)


def _head_ones():
    r = lax.shift_right_logical(_iota((LANES, LANES), 0), 6)
    c = lax.shift_right_logical(_iota((LANES, LANES), 1), 6)
    return jnp.where(r == c, 1.0, 0.0).astype(BF16)


def _head_sum(x, ones_bd):
    hi = x.astype(BF16)
    lo = (x - hi.astype(F32)).astype(BF16)
    return _dot(hi, ones_bd) + _dot(lo, ones_bd)


def _rmsnorm_kernel(x_ref, g_ref, o_ref):
    x = x_ref[...]
    y = x * lax.rsqrt(jnp.mean(x * x, axis=-1, keepdims=True) + RMS_EPS)
    o_ref[...] = (y * g_ref[...]).astype(o_ref.dtype)


def _rmsnorm(x, g, tm=512):
    m, d = x.shape
    return pl.pallas_call(
        _rmsnorm_kernel,
        grid=(m // tm,),
        in_specs=[pl.BlockSpec((tm, d), lambda i: (i, 0)),
                  pl.BlockSpec((1, d), lambda i: (0, 0))],
        out_specs=pl.BlockSpec((tm, d), lambda i: (i, 0)),
        out_shape=jax.ShapeDtypeStruct((m, d), BF16),
        compiler_params=_cparams(("parallel",)),
        name="rmsnorm",
    )(x, g.reshape(1, d))


def _mm_kernel(a_ref, b_ref, o_ref):
    o_ref[...] = _dot(a_ref[...], b_ref[...]).astype(o_ref.dtype)


def _mm_res_kernel(a_ref, b_ref, r_ref, o_ref):
    o_ref[...] = (r_ref[...] + _dot(a_ref[...], b_ref[...])).astype(o_ref.dtype)


def _matmul(a, b, out_dtype, tm, tn, res=None, name="matmul"):
    m, k = a.shape
    n = b.shape[1]
    in_specs = [pl.BlockSpec((tm, k), lambda i, j: (i, 0)),
                pl.BlockSpec((k, tn), lambda i, j: (0, j))]
    args = [a, b]
    kern = _mm_kernel
    if res is not None:
        in_specs.append(pl.BlockSpec((tm, tn), lambda i, j: (i, j)))
        args.append(res)
        kern = _mm_res_kernel
    return pl.pallas_call(
        kern,
        grid=(m // tm, n // tn),
        in_specs=in_specs,
        out_specs=pl.BlockSpec((tm, tn), lambda i, j: (i, j)),
        out_shape=jax.ShapeDtypeStruct((m, n), out_dtype),
        compiler_params=_cparams(("parallel", "arbitrary")),
        name=name,
    )(*args)


def _mla_proj_kernel(z_ref, pos_ref, invf_ref, gq_ref, gkv_ref, wq_ref, wkv_ref,
                     q_ref, k_ref, v_ref):
    tm = z_ref.shape[0]
    z = z_ref[...]

    def norm(c, g):
        return (c * lax.rsqrt(jnp.mean(c * c, axis=-1, keepdims=True) + RMS_EPS) * g).astype(BF16)

    cq = norm(z[:, :Q_LORA], gq_ref[...])
    ckv = norm(z[:, Q_LORA:Q_LORA + KV_LORA], gkv_ref[...])
    q = _dot(cq, wq_ref[...]) * (QK_DIM ** -0.5)
    kv = _dot(ckv, wkv_ref[...])

    ang = pos_ref[...].astype(F32) * invf_ref[...]
    lane = _iota((tm, LANES), 1)
    half = ROPE_DIM // 2
    cos, sin = jnp.cos(ang), jnp.sin(ang)
    c_tab = jnp.where(lane < ROPE_DIM, cos, 0.0)
    s_up = jnp.where((lane >= half) & (lane < ROPE_DIM), sin, 0.0)
    s_dn = jnp.where(lane < half, -sin, 0.0)

    def rope(x):
        return (x * c_tab + pltpu.roll(x, half, 1) * s_up
                + pltpu.roll(x, LANES - half, 1) * s_dn)

    k_r = rope(z[:, Q_LORA + KV_LORA:]).astype(BF16)
    for h in range(MLA_HEADS):
        lo = h * QK_PAD
        q_ref[:, lo:lo + NOPE_DIM] = q[:, lo:lo + NOPE_DIM].astype(BF16)
        q_ref[:, lo + NOPE_DIM:lo + QK_PAD] = rope(q[:, lo + NOPE_DIM:lo + QK_PAD]).astype(BF16)
        k_ref[:, lo:lo + NOPE_DIM] = kv[:, h * NOPE_DIM:(h + 1) * NOPE_DIM].astype(BF16)
        k_ref[:, lo + NOPE_DIM:lo + QK_PAD] = k_r
    v_ref[...] = kv[:, MLA_HEADS * NOPE_DIM:].astype(BF16)


def _mla_proj(z_mla, pos, invf, g_q, g_kv, wq, wkv, tm=256):
    m, w = z_mla.shape
    full = lambda shape: pl.BlockSpec(shape, lambda i: (0, 0))
    row = lambda n: pl.BlockSpec((tm, n), lambda i: (i, 0))
    return pl.pallas_call(
        _mla_proj_kernel,
        grid=(m // tm,),
        in_specs=[row(w), row(1), full((1, LANES)), full((1, Q_LORA)), full((1, KV_LORA)),
                  full(wq.shape), full(wkv.shape)],
        out_specs=[row(MLA_HEADS * QK_PAD), row(MLA_HEADS * QK_PAD), row(MLA_HEADS * V_DIM)],
        out_shape=[jax.ShapeDtypeStruct((m, MLA_HEADS * QK_PAD), BF16),
                   jax.ShapeDtypeStruct((m, MLA_HEADS * QK_PAD), BF16),
                   jax.ShapeDtypeStruct((m, MLA_HEADS * V_DIM), BF16)],
        compiler_params=_cparams(("parallel",)),
        name="mla_proj",
    )(z_mla, pos, invf, g_q.reshape(1, -1), g_kv.reshape(1, -1), wq, wkv)


def _mla_attn_kernel(q_ref, k_ref, v_ref, o_ref):
    s = _dot_nt(q_ref[...], k_ref[...])
    p = jnp.exp(s - jnp.max(s, axis=-1, keepdims=True))
    l = jnp.sum(p, axis=-1, keepdims=True)
    o = _dot(p.astype(BF16), v_ref[...])
    o_ref[...] = (o / l).astype(o_ref.dtype)


def _mla_attn(q, k, v, tq=512):
    b, s, _ = q.shape
    return pl.pallas_call(
        _mla_attn_kernel,
        grid=(b, MLA_HEADS, s // tq),
        in_specs=[pl.BlockSpec((None, tq, QK_PAD), lambda b_, h, i: (b_, i, h)),
                  pl.BlockSpec((None, s, QK_PAD), lambda b_, h, i: (b_, 0, h)),
                  pl.BlockSpec((None, s, V_DIM), lambda b_, h, i: (b_, 0, h))],
        out_specs=pl.BlockSpec((None, tq, V_DIM), lambda b_, h, i: (b_, i, h)),
        out_shape=jax.ShapeDtypeStruct((b, s, MLA_HEADS * V_DIM), BF16),
        compiler_params=_cparams(("parallel", "parallel", "arbitrary")),
        name="mla_attn",
    )(q, k, v)


def _rwkv_prep_kernel(seq_len,
                      r_ref, rp_ref, rn_ref, k_ref, kp_ref, kn_ref, v_ref, vp_ref, vn_ref,
                      l_ref, lp_ref, ln_ref, scr_ref, sck_ref, scv_ref, scl_ref,
                      w2f_ref, w2b_ref, a2f_ref, a2b_ref, g2_ref,
                      w0f_ref, w0b_ref, a0f_ref, a0b_ref, kk_w_ref, ka_ref,
                      ro_ref, vo_ref, kko_ref, lwf_ref, lwb_ref, alf_ref, alb_ref,
                      kdf_ref, kdb_ref, g_ref):
    tm = r_ref.shape[0]
    i = pl.program_id(0)
    first = lax.rem(i * tm, seq_len) == 0
    last = lax.rem((i + 1) * tm, seq_len) == 0

    def shift(z_ref, zp_ref, zn_ref, w_ref):
        z = z_ref[...]
        rows = _iota(z.shape, 0)
        prev_row = jnp.where(first, 0.0, zp_ref[SUBLANES - 1:SUBLANES, :])
        next_row = jnp.where(last, 0.0, zn_ref[0:1, :])
        z_prev = jnp.where(rows == 0, prev_row, pltpu.roll(z, 1, 0))
        z_next = jnp.where(rows == tm - 1, next_row, pltpu.roll(z, tm - 1, 0))
        return w_ref[0:1, :] * z_prev + w_ref[1:2, :] * z + w_ref[2:3, :] * z_next

    r = shift(r_ref, rp_ref, rn_ref, scr_ref)
    k = shift(k_ref, kp_ref, kn_ref, sck_ref)
    v = shift(v_ref, vp_ref, vn_ref, scv_ref)
    lo = shift(l_ref, lp_ref, ln_ref, scl_ref)
    xw_f = lo[:, 0 * LORA_PAD:1 * LORA_PAD]
    xw_b = lo[:, 1 * LORA_PAD:2 * LORA_PAD]
    xa_f = lo[:, 2 * LORA_PAD:3 * LORA_PAD]
    xa_b = lo[:, 3 * LORA_PAD:4 * LORA_PAD]
    xg = lo[:, 4 * LORA_PAD:]

    def log_decay(xw, w0_ref, w2_ref):
        y = -(w0_ref[...] + _dot(jnp.tanh(xw).astype(BF16), w2_ref[...]))
        softplus = jnp.maximum(y, 0.0) + jnp.log(1.0 + jnp.exp(-jnp.abs(y)))
        return -jnp.exp(-softplus - 0.5)

    def rate(xa, a0_ref, a2_ref):
        return _sigmoid(a0_ref[...] + _dot(xa.astype(BF16), a2_ref[...]))

    al_f = rate(xa_f, a0f_ref, a2f_ref)
    al_b = rate(xa_b, a0b_ref, a2b_ref)
    kk = k * kk_w_ref[...]
    kk = kk * lax.rsqrt(_head_sum(kk * kk, _head_ones()) + 1e-12)
    ka = ka_ref[...]

    ro_ref[...] = r
    vo_ref[...] = v
    kko_ref[...] = kk
    lwf_ref[...] = log_decay(xw_f, w0f_ref, w2f_ref)
    lwb_ref[...] = log_decay(xw_b, w0b_ref, w2b_ref)
    alf_ref[...] = al_f
    alb_ref[...] = al_b
    kdf_ref[...] = k * (1.0 + (al_f - 1.0) * ka)
    kdb_ref[...] = k * (1.0 + (al_b - 1.0) * ka)
    g_ref[...] = _dot(_sigmoid(xg).astype(BF16), g2_ref[...])


def _rwkv_prep(z_rkv, z_lora, seq_len, sc_rkv, sc_lora, w2f, w2b, a2f, a2b, g2,
               w0f, w0b, a0f, a0b, k_k, k_a, tm=512):
    m = z_rkv.shape[0]
    wl = z_lora.shape[1]
    nb = RWKV_WIDTH // LANES
    rb = tm // SUBLANES
    nrb = m // SUBLANES

    def main(seg):
        return pl.BlockSpec((tm, LANES), lambda i, p: (i, seg * nb + p))

    def prev(seg):
        return pl.BlockSpec((SUBLANES, LANES),
                            lambda i, p: (jnp.maximum(i * rb - 1, 0), seg * nb + p))

    def nxt(seg):
        return pl.BlockSpec((SUBLANES, LANES),
                            lambda i, p: (jnp.minimum((i + 1) * rb, nrb - 1), seg * nb + p))

    def sc(seg):
        return pl.BlockSpec((3, LANES), lambda i, p: (0, seg * nb + p))

    colblk = lambda rows: pl.BlockSpec((rows, LANES), lambda i, p: (0, p))
    in_specs = []
    for seg in range(3):
        in_specs += [main(seg), prev(seg), nxt(seg)]
    in_specs += [pl.BlockSpec((tm, wl), lambda i, p: (i, 0)),
                 pl.BlockSpec((SUBLANES, wl), lambda i, p: (jnp.maximum(i * rb - 1, 0), 0)),
                 pl.BlockSpec((SUBLANES, wl), lambda i, p: (jnp.minimum((i + 1) * rb, nrb - 1), 0))]
    in_specs += [sc(0), sc(1), sc(2), pl.BlockSpec((3, wl), lambda i, p: (0, 0))]
    in_specs += [colblk(LORA_PAD)] * 4 + [colblk(GATE_LORA)] + [colblk(1)] * 6
    out_spec = pl.BlockSpec((tm, LANES), lambda i, p: (i, p))
    n_out = 10
    row = lambda a: a.reshape(1, -1)
    return pl.pallas_call(
        functools.partial(_rwkv_prep_kernel, seq_len),
        grid=(m // tm, nb),
        in_specs=in_specs,
        out_specs=[out_spec] * n_out,
        out_shape=[jax.ShapeDtypeStruct((m, RWKV_WIDTH), F32)] * n_out,
        compiler_params=_cparams(("parallel", "arbitrary")),
        name="rwkv_prep",
    )(z_rkv, z_rkv, z_rkv, z_rkv, z_rkv, z_rkv, z_rkv, z_rkv, z_rkv,
      z_lora, z_lora, z_lora, sc_rkv, sc_rkv, sc_rkv, sc_lora,
      w2f, w2b, a2f, a2b, g2, row(w0f), row(w0b), row(a0f), row(a0b), row(k_k), row(k_a))


def _scan_chunk(r, v, kk, lw, al, kd, rk, ht, reverse):
    c = CHUNK
    n = PAIR * c
    a = -kk
    b = kk * al
    t_i = _iota((c, c), 0)
    s_i = _iota((c, c), 1)
    tri = jnp.where((s_i >= t_i) if reverse else (s_i <= t_i), 1.0, 0.0).astype(BF16)
    lw_hi = lw.astype(BF16)
    lw_lo = (lw - lw_hi.astype(F32)).astype(BF16)
    cum = _dot(tri, lw_hi) + _dot(tri, lw_lo)
    cum_prev = cum - lw
    ref_row = c // 2 if reverse else c // 2 - 1
    tot_row = 0 if reverse else c - 1
    c_ref = cum[ref_row:ref_row + 1, :]
    c_tot = cum[tot_row:tot_row + 1, :]
    e_prev = jnp.exp(cum_prev - c_ref)
    e_cur = jnp.exp(cum - c_ref)
    e_inv = jnp.exp(c_ref - cum)
    e_out = jnp.exp(c_tot - cum)

    lane = _iota((c, LANES), 1)
    head0 = lane < RWKV_HEAD

    def stack(x):
        xb = x.astype(BF16)
        zero = jnp.zeros_like(xb)
        return jnp.concatenate([jnp.where(head0, xb, zero), jnp.where(head0, zero, xb)], axis=0)

    row = _iota((n, n), 0)
    col = _iota((n, n), 1)
    strict = (col > row) if reverse else (col < row)
    incl = (col >= row) if reverse else (col <= row)

    lhs = jnp.concatenate([stack(a * e_prev), stack(r * e_cur)], axis=0)
    rhs = jnp.concatenate([stack(b * e_inv), stack(kd * e_inv)], axis=0)
    nmat = _dot_nt(lhs, rhs)
    n_ab = jnp.where(strict, nmat[:n, :n], 0.0)
    n_ak = jnp.where(strict, nmat[:n, n:], 0.0).astype(BF16)
    n_rb = jnp.where(incl, nmat[n:, :n], 0.0).astype(BF16)
    n_rk = jnp.where(incl, nmat[n:, n:], 0.0).astype(BF16)

    pw = n_ab
    tinv = jnp.where(row == col, 1.0, 0.0) + pw
    for _ in range(int(np.log2(c)) - 1):
        pb = pw.astype(BF16)
        pw = _dot(pb, pb)
        tinv = tinv + _dot(tinv.astype(BF16), pw.astype(BF16))

    v_s = stack(v)
    ht_b = ht.astype(BF16)
    from_state = _dot_nt(jnp.concatenate([stack(a * jnp.exp(cum_prev)), stack(r * jnp.exp(cum))], axis=0),
                         ht_b)
    u = _dot(tinv.astype(BF16), from_state[:n] + _dot(n_ak, v_s)).astype(BF16)
    uv = jnp.concatenate([u, v_s], axis=0)
    o_s = from_state[n:] + _dot(jnp.concatenate([n_rb, n_rk], axis=1), uv)
    o = o_s[:c] + o_s[c:]
    ht_new = ht * jnp.exp(c_tot) + _dot_tn(
        uv, jnp.concatenate([stack(b * e_out), stack(kd * e_out)], axis=0))
    bonus = _head_sum(r * kd * rk, _head_ones()) * v
    return o + bonus, ht_new


def _rwkv_scan_kernel(rf, vf, kkf, lwf, alf, kdf, rb, vb, kkb, lwb, alb, kdb, rk_ref,
                      yf_ref, yb_ref, h_ref):
    @pl.when(pl.program_id(2) == 0)
    def _():
        h_ref[...] = jnp.zeros_like(h_ref)

    n_chunks = rf.shape[0] // CHUNK
    rk = rk_ref[...]

    def body(ci, carry):
        for d, (refs, y_ref) in enumerate((((rf, vf, kkf, lwf, alf, kdf), yf_ref),
                                           ((rb, vb, kkb, lwb, alb, kdb), yb_ref))):
            cidx = ci if d == 0 else n_chunks - 1 - ci
            rows = pl.ds(pl.multiple_of(cidx * CHUNK, CHUNK), CHUNK)
            vals = [x[rows, :] for x in refs]
            y, ht = _scan_chunk(*vals, rk, h_ref[d], reverse=(d == 1))
            y_ref[rows, :] = y
            h_ref[d] = ht
        return carry

    lax.fori_loop(0, n_chunks, body, 0)


def _rwkv_scan(r, v, kk, lw_f, lw_b, al_f, al_b, kd_f, kd_b, r_k, tb=512):
    b, s, w = r.shape
    nblk = s // tb
    fwd = pl.BlockSpec((None, tb, LANES), lambda b_, p, j: (b_, j, p))
    bwd = pl.BlockSpec((None, tb, LANES), lambda b_, p, j: (b_, nblk - 1 - j, p))
    return pl.pallas_call(
        _rwkv_scan_kernel,
        grid=(b, w // LANES, nblk),
        in_specs=[fwd] * 6 + [bwd] * 6 + [pl.BlockSpec((1, LANES), lambda b_, p, j: (0, p))],
        out_specs=[fwd, bwd],
        out_shape=[jax.ShapeDtypeStruct((b, s, w), F32)] * 2,
        scratch_shapes=[pltpu.VMEM((2, LANES, LANES), F32)],
        compiler_params=_cparams(("parallel", "parallel", "arbitrary")),
        name="rwkv_scan",
    )(r, v, kk, lw_f, al_f, kd_f, r, v, kk, lw_b, al_b, kd_b, r_k.reshape(1, -1))


def _rwkv_post_kernel(yf_ref, yb_ref, g_ref, lg_ref, lb_ref, o_ref):
    ones_bd = _head_ones()
    y = yf_ref[...] + yb_ref[...]
    mu = _head_sum(y, ones_bd) * (1.0 / RWKV_HEAD)
    d = y - mu
    var = _head_sum(d * d, ones_bd) * (1.0 / RWKV_HEAD)
    yn = d * lax.rsqrt(var + LNX_EPS) * lg_ref[...] + lb_ref[...]
    o_ref[...] = (yn * g_ref[...]).astype(o_ref.dtype)


def _rwkv_post(y_f, y_b, g, lnx_g, lnx_b, tm=512):
    m, w = y_f.shape
    blk = pl.BlockSpec((tm, LANES), lambda i, p: (i, p))
    vec = pl.BlockSpec((1, LANES), lambda i, p: (0, p))
    return pl.pallas_call(
        _rwkv_post_kernel,
        grid=(m // tm, w // LANES),
        in_specs=[blk, blk, blk, vec, vec],
        out_specs=blk,
        out_shape=jax.ShapeDtypeStruct((m, w), BF16),
        compiler_params=_cparams(("parallel", "parallel")),
        name="rwkv_post",
    )(y_f, y_b, g, lnx_g.reshape(1, -1), lnx_b.reshape(1, -1))


def _merge_kernel(a1_ref, w1_ref, a2_ref, w2_ref, g1_ref, g2_ref, o_ref):
    m1 = _dot(a1_ref[...], w1_ref[...])
    m2 = _dot(a2_ref[...], w2_ref[...])
    o_ref[...] = (_sigmoid(g1_ref[...]) * m1 + _sigmoid(g2_ref[...]) * m2).astype(o_ref.dtype)


def _merge(o_mla, w_up_mla, o_rwkv, w_up_rwkv, z_gate, tm=1024, tn=512):
    m, k = o_mla.shape
    n = w_up_mla.shape[1]
    nj = n // tn
    a_spec = pl.BlockSpec((tm, k), lambda i, j: (i, 0))
    w_spec = pl.BlockSpec((k, tn), lambda i, j: (0, j))
    return pl.pallas_call(
        _merge_kernel,
        grid=(m // tm, nj),
        in_specs=[a_spec, w_spec, a_spec, w_spec,
                  pl.BlockSpec((tm, tn), lambda i, j: (i, j)),
                  pl.BlockSpec((tm, tn), lambda i, j: (i, nj + j))],
        out_specs=pl.BlockSpec((tm, tn), lambda i, j: (i, j)),
        out_shape=jax.ShapeDtypeStruct((m, n), BF16),
        compiler_params=_cparams(("parallel", "arbitrary")),
        name="merge",
    )(o_mla, w_up_mla, o_rwkv, w_up_rwkv, z_gate, z_gate)


def _cross_router_kernel(h_ref, gc_ref, wq_ref, kv_ref, wo_ref, gf_ref, wr_ref, br_ref,
                         h2_ref, n3_ref, comb_ref):
    h = h_ref[...]
    hn = (h * lax.rsqrt(jnp.mean(h * h, axis=-1, keepdims=True) + RMS_EPS) * gc_ref[...]).astype(BF16)
    q = (_dot(hn, wq_ref[...]) * (CROSS_HEAD_DIM ** -0.5)).astype(BF16)
    kv = kv_ref[...]
    outs = []
    for hd in range(CROSS_HEADS):
        lo = hd * CROSS_HEAD_DIM
        s = _dot_nt(q[:, lo:lo + CROSS_HEAD_DIM], kv[:, lo:lo + CROSS_HEAD_DIM])
        p = jnp.exp(s - jnp.max(s, axis=-1, keepdims=True))
        p = p / jnp.sum(p, axis=-1, keepdims=True)
        outs.append(_dot(p.astype(BF16), kv[:, CROSS_WIDTH + lo:CROSS_WIDTH + lo + CROSS_HEAD_DIM]))
    o = jnp.concatenate(outs, axis=-1).astype(BF16)
    h2 = h + _dot(o, wo_ref[...])
    h2_ref[...] = h2

    n3 = h2 * lax.rsqrt(jnp.mean(h2 * h2, axis=-1, keepdims=True) + RMS_EPS) * gf_ref[...]
    n3_ref[...] = n3.astype(BF16)

    logits = jnp.dot(n3, wr_ref[...], preferred_element_type=F32,
                     precision=lax.Precision.HIGHEST) + br_ref[...]
    lane = _iota(logits.shape, 1)
    lane_f = lane.astype(F32)
    neg = jnp.float32(-jnp.inf)
    big = jnp.float32(1e9)

    def masked_softmax(mask):
        x = jnp.where(mask, logits, neg)
        e = jnp.exp(x - jnp.max(x, axis=-1, keepdims=True))
        return e / jnp.sum(e, axis=-1, keepdims=True)

    def top1(prob, mask):
        pmax = jnp.max(jnp.where(mask, prob, -1.0), axis=-1, keepdims=True)
        idx = jnp.min(jnp.where(mask & (prob == pmax), lane_f, big), axis=-1, keepdims=True)
        return pmax, idx

    g_mask = (lane >= N_EXPERTS) & (lane < N_EXPERTS + N_GROUPS)
    p_group, g_idx = top1(masked_softmax(g_mask), g_mask)
    g_sel = g_idx - float(N_EXPERTS)
    e_mask = (lane < N_EXPERTS) & (lax.shift_right_logical(lane, 3).astype(F32) == g_sel)
    e_prob = masked_softmax(e_mask)
    p1, i1 = top1(e_prob, e_mask)
    rest = e_mask & (lane_f != i1)
    p2, i2 = top1(e_prob, rest)
    denom = p1 + p2
    comb_ref[...] = jnp.where(lane_f == i1, p_group * (p1 / denom),
                              jnp.where(lane_f == i2, p_group * (p2 / denom), 0.0))


def _cross_router(h1, seq_len, g_cross, wq, kvm, wo, g_ffn, w_r, b_r, tm=256):
    m, d = h1.shape
    full = lambda a: pl.BlockSpec(a.shape, lambda i: (0,) * a.ndim)
    row = lambda n: pl.BlockSpec((tm, n), lambda i: (i, 0))
    gc, gf = g_cross.reshape(1, d), g_ffn.reshape(1, d)
    per_seq = seq_len // tm
    return pl.pallas_call(
        _cross_router_kernel,
        grid=(m // tm,),
        in_specs=[row(d), full(gc), full(wq),
                  pl.BlockSpec((None,) + kvm.shape[1:], lambda i: (i // per_seq, 0, 0)),
                  full(wo), full(gf), full(w_r), full(b_r)],
        out_specs=[row(d), row(d), row(LANES)],
        out_shape=[jax.ShapeDtypeStruct((m, d), F32), jax.ShapeDtypeStruct((m, d), BF16),
                   jax.ShapeDtypeStruct((m, LANES), F32)],
        compiler_params=_cparams(("parallel",)),
        name="cross_router",
    )(h1, gc, wq, kvm, wo, gf, w_r, b_r)


def _moe_kernel(n_ref, comb_ref, h_ref, wg_ref, wu_ref, wd_ref, gfin_ref, o_ref, acc_ref):
    e = pl.program_id(1)

    @pl.when(e == 0)
    def _():
        acc_ref[...] = h_ref[...]

    n = n_ref[...]
    comb = comb_ref[...]
    lane = _iota(comb.shape, 1)
    gate = jnp.sum(jnp.where(lane == e, comb, 0.0), axis=-1, keepdims=True)
    hg = _dot(n, wg_ref[...])
    hu = _dot(n, wu_ref[...])
    hid = (hg * _sigmoid(hg) * hu * gate).astype(BF16)
    acc_ref[...] += _dot(hid, wd_ref[...])

    @pl.when(e == pl.num_programs(1) - 1)
    def _():
        y = acc_ref[...]
        o_ref[...] = y * lax.rsqrt(jnp.mean(y * y, axis=-1, keepdims=True) + RMS_EPS) * gfin_ref[...]


def _moe(n3, comb, h2, w_eg, w_eu, w_ed, g_final, tm=512):
    m, d = n3.shape
    ne, _, f = w_eg.shape
    row = lambda n: pl.BlockSpec((tm, n), lambda i, e: (i, 0))
    return pl.pallas_call(
        _moe_kernel,
        grid=(m // tm, ne),
        in_specs=[row(d), row(LANES), row(d),
                  pl.BlockSpec((None, d, f), lambda i, e: (e, 0, 0)),
                  pl.BlockSpec((None, d, f), lambda i, e: (e, 0, 0)),
                  pl.BlockSpec((None, f, d), lambda i, e: (e, 0, 0)),
                  pl.BlockSpec((1, d), lambda i, e: (0, 0))],
        out_specs=row(d),
        out_shape=jax.ShapeDtypeStruct((m, d), F32),
        scratch_shapes=[pltpu.VMEM((tm, d), F32)],
        compiler_params=_cparams(("parallel", "arbitrary")),
        name="moe",
    )(n3, comb, h2, w_eg, w_eu, w_ed, g_final.reshape(1, d))


def _pad_cols(w, n):
    return jnp.pad(w, ((0, 0), (0, n - w.shape[1])))


def _pad_rows(w, n):
    return jnp.pad(w, ((0, n - w.shape[0]), (0, 0)))


def _split_lora(w):
    o = 0
    parts = []
    for width in (DECAY_LORA, DECAY_LORA, ICLR_LORA, ICLR_LORA):
        parts.append(_pad_cols(w[:, o:o + width], LORA_PAD))
        o += width
    parts.append(w[:, o:o + GATE_LORA])
    return jnp.concatenate(parts, axis=1)


def kernel(x, mem, positions, g_mix, w_in, g_q, w_uq, g_kv, w_ukv, shift_conv, w0_f, w2_f, w0_b, w2_b, a0_f, a2_f, a0_b, a2_b, g2, k_k, k_a, r_k, lnx_g, lnx_b, w_up_mla, w_up_rwkv, w_out, g_cross, g_mem, wq_c, wkv_c, wo_c, g_ffn, w_rg, b_rg, w_re, b_re, w_eg, w_eu, w_ed, g_final):
    bsz, seq, d = x.shape
    m = bsz * seq
    depth = w_in.shape[0]
    h = x.reshape(m, d)
    pos = positions.reshape(m, 1)
    lane = np.arange(LANES)
    invf = jnp.asarray(np.where(lane < ROPE_DIM, 1.0, 0.0), F32) * (
        ROPE_BASE ** (-jnp.asarray(lane % (ROPE_DIM // 2), F32) * (2.0 / ROPE_DIM)))
    invf = invf.reshape(1, LANES)
    assert depth == 1, "the MoE kernel applies the final norm, so it must be the last layer"
    for l in range(depth):
        wi = w_in[l]
        w_mla = _pad_cols(wi[:, :MLA_IN], MLA_IN + (LANES - ROPE_DIM)).astype(BF16)
        rw = wi[:, MLA_IN:MLA_IN + RWKV_IN]
        w_rkv = rw[:, :3 * RWKV_WIDTH].astype(BF16)
        w_lora = _split_lora(rw[:, 3 * RWKV_WIDTH:]).astype(BF16)
        w_gate = wi[:, MLA_IN + RWKV_IN:].astype(BF16)
        sc = shift_conv[l]
        sc_rkv = sc[:, :3 * RWKV_WIDTH]
        sc_lora = _split_lora(sc[:, 3 * RWKV_WIDTH:])
        wq = w_uq[l].reshape(Q_LORA, MLA_HEADS, QK_DIM)
        wq = jnp.pad(wq, ((0, 0), (0, 0), (0, QK_PAD - QK_DIM))).reshape(Q_LORA, MLA_HEADS * QK_PAD)
        wkv = w_ukv[l].reshape(KV_LORA, MLA_HEADS, NOPE_DIM + V_DIM)
        wkv = jnp.concatenate([wkv[:, :, :NOPE_DIM].reshape(KV_LORA, -1),
                               wkv[:, :, NOPE_DIM:].reshape(KV_LORA, -1)], axis=1)
        lora_rows = lambda w: _pad_rows(w, LORA_PAD).astype(BF16)
        w_router = _pad_cols(jnp.concatenate(
            [jnp.moveaxis(w_re[l], 0, 1).reshape(d, N_EXPERTS), w_rg[l]], axis=1), LANES)
        b_router = _pad_cols(jnp.concatenate([b_re[l].reshape(1, N_EXPERTS), b_rg[l].reshape(1, N_GROUPS)],
                                             axis=1), LANES)

        n1 = _rmsnorm(h, g_mix[l])
        z_mla = _matmul(n1, w_mla, F32, 1024, w_mla.shape[1], name="in_proj_mla")
        z_rkv = _matmul(n1, w_rkv, F32, 1024, 512, name="in_proj_rkv")
        z_lora = _matmul(n1, w_lora, F32, 1024, w_lora.shape[1], name="in_proj_lora")
        z_gate = _matmul(n1, w_gate, F32, 1024, 512, name="in_proj_gate")

        q_cat, k_cat, v_mla = _mla_proj(z_mla, pos, invf, g_q[l], g_kv[l], wq.astype(BF16), wkv.astype(BF16))
        o_mla = _mla_attn(q_cat.reshape(bsz, seq, -1), k_cat.reshape(bsz, seq, -1),
                          v_mla.reshape(bsz, seq, -1)).reshape(m, -1)

        (r, v, kk, lw_f, lw_b, al_f, al_b, kd_f, kd_b, gate) = _rwkv_prep(
            z_rkv, z_lora, seq, sc_rkv, sc_lora, lora_rows(w2_f[l]), lora_rows(w2_b[l]),
            lora_rows(a2_f[l]), lora_rows(a2_b[l]), g2[l].astype(BF16),
            w0_f[l], w0_b[l], a0_f[l], a0_b[l], k_k[l], k_a[l])
        sh = lambda t: t.reshape(bsz, seq, RWKV_WIDTH)
        y_f, y_b = _rwkv_scan(sh(r), sh(v), sh(kk), sh(lw_f), sh(lw_b), sh(al_f), sh(al_b),
                              sh(kd_f), sh(kd_b), r_k[l])
        o_rwkv = _rwkv_post(y_f.reshape(m, -1), y_b.reshape(m, -1), gate, lnx_g[l], lnx_b[l])

        merged = _merge(o_mla, w_up_mla[l].astype(BF16), o_rwkv, w_up_rwkv[l].astype(BF16), z_gate)
        h1 = _matmul(merged, w_out[l].astype(BF16), F32, 1024, 512, res=h, name="out_proj")

        mem_n = _rmsnorm(mem.reshape(bsz * MEM_LEN, d), g_mem[l], tm=MEM_LEN)
        kvm = _matmul(mem_n, wkv_c[l].astype(BF16), BF16, bsz * MEM_LEN, 512, name="mem_kv")
        h2, n3, comb = _cross_router(h1, seq, g_cross[l], wq_c[l].astype(BF16),
                                     kvm.reshape(bsz, MEM_LEN, 2 * CROSS_WIDTH), wo_c[l].astype(BF16),
                                     g_ffn[l], w_router, b_router)

        flat = lambda w: w.reshape((N_EXPERTS,) + w.shape[2:]).astype(BF16)
        out = _moe(n3, comb, h2, flat(w_eg[l]), flat(w_eu[l]), flat(w_ed[l]), g_final)
    return out.reshape(bsz, seq, d)
```

```python
import functools

import jax
import jax.numpy as jnp
import numpy as np
from jax import lax
from jax.experimental import pallas as pl
from jax.experimental.pallas import tpu as pltpu

F32 = jnp.float32
BF16 = jnp.bfloat16

D_MODEL = 2048
MEM_LEN = 256
RMS_EPS = 1e-6
MLA_HEADS = 16
Q_LORA = 512
KV_LORA = 256
NOPE_DIM = 128
ROPE_DIM = 64
V_DIM = 128
QK_DIM = NOPE_DIM + ROPE_DIM
ROPE_BASE = 10000.0
RWKV_HEAD = 64
RWKV_HEADS = D_MODEL // RWKV_HEAD
RWKV_WIDTH = D_MODEL
DECAY_LORA = 96
ICLR_LORA = 96
GATE_LORA = 256
LNX_EPS = 64e-5
CROSS_HEADS = 4
CROSS_HEAD_DIM = 128
CROSS_WIDTH = CROSS_HEADS * CROSS_HEAD_DIM
N_GROUPS = 4
EXPERTS_PER_GROUP = 8
N_EXPERTS = N_GROUPS * EXPERTS_PER_GROUP
D_EXPERT = 512
MLA_IN = Q_LORA + KV_LORA + ROPE_DIM
RWKV_IN = 3 * RWKV_WIDTH + 2 * DECAY_LORA + 2 * ICLR_LORA + GATE_LORA

LANES = 128
SUBLANES = 8
QK_PAD = 256
LORA_PAD = LANES
PAIR = LANES // RWKV_HEAD
N_PAIRS = RWKV_HEADS // PAIR
CHUNK = 64
ATTN_TK = 512
LOG2E = 1.4426950408889634
VMEM_LIMIT = 56 * 1024 * 1024


def _cparams(sem, vmem=VMEM_LIMIT, flags=None):
    return pltpu.CompilerParams(dimension_semantics=sem, vmem_limit_bytes=vmem, flags=flags)


def _iota(shape, dim):
    return lax.broadcasted_iota(jnp.int32, shape, dim)


def _sigmoid(x):
    return 1.0 / (1.0 + jnp.exp(-x))


def _dot(a, b):
    return jnp.dot(a, b, preferred_element_type=F32)


def _dot_nt(a, b):
    return lax.dot_general(a, b, (((1,), (1,)), ((), ())), preferred_element_type=F32)


def _dot_tn(a, b):
    return lax.dot_general(a, b, (((0,), (0,)), ((), ())), preferred_element_type=F32)


def _head_ones():
    r = lax.shift_right_logical(_iota((LANES, LANES), 0), 6)
    c = lax.shift_right_logical(_iota((LANES, LANES), 1), 6)
    return jnp.where(r == c, 1.0, 0.0).astype(BF16)


def _head_sum(x, ones_bd):
    hi = x.astype(BF16)
    lo = (x - hi.astype(F32)).astype(BF16)
    return _dot(hi, ones_bd) + _dot(lo, ones_bd)


def _rmsnorm_kernel(x_ref, g_ref, o_ref):
    x = x_ref[...]
    y = x * lax.rsqrt(jnp.mean(x * x, axis=-1, keepdims=True) + RMS_EPS)
    o_ref[...] = (y * g_ref[...]).astype(o_ref.dtype)


def _rmsnorm(x, g, tm=512):
    m, d = x.shape
    return pl.pallas_call(
        _rmsnorm_kernel,
        grid=(m // tm,),
        in_specs=[pl.BlockSpec((tm, d), lambda i: (i, 0)),
                  pl.BlockSpec((1, d), lambda i: (0, 0))],
        out_specs=pl.BlockSpec((tm, d), lambda i: (i, 0)),
        out_shape=jax.ShapeDtypeStruct((m, d), BF16),
        compiler_params=_cparams(("parallel",)),
        name="rmsnorm",
    )(x, g.reshape(1, d))


def _mm_kernel(a_ref, b_ref, o_ref):
    o_ref[...] = _dot(a_ref[...], b_ref[...]).astype(o_ref.dtype)


def _mm_res_kernel(a_ref, b_ref, r_ref, o_ref):
    o_ref[...] = (r_ref[...] + _dot(a_ref[...], b_ref[...])).astype(o_ref.dtype)


def _matmul(a, b, out_dtype, tm, tn, res=None, name="matmul"):
    m, k = a.shape
    n = b.shape[1]
    in_specs = [pl.BlockSpec((tm, k), lambda i, j: (i, 0)),
                pl.BlockSpec((k, tn), lambda i, j: (0, j))]
    args = [a, b]
    kern = _mm_kernel
    if res is not None:
        in_specs.append(pl.BlockSpec((tm, tn), lambda i, j: (i, j)))
        args.append(res)
        kern = _mm_res_kernel
    return pl.pallas_call(
        kern,
        grid=(m // tm, n // tn),
        in_specs=in_specs,
        out_specs=pl.BlockSpec((tm, tn), lambda i, j: (i, j)),
        out_shape=jax.ShapeDtypeStruct((m, n), out_dtype),
        compiler_params=_cparams(("parallel", "arbitrary")),
        name=name,
    )(*args)


def _mla_proj_kernel(z_ref, pos_ref, invf_ref, gq_ref, gkv_ref, wq_ref, wkv_ref,
                     q_ref, k_ref, v_ref):
    tm = z_ref.shape[0]
    z = z_ref[...]

    def norm(c, g):
        return (c * lax.rsqrt(jnp.mean(c * c, axis=-1, keepdims=True) + RMS_EPS) * g).astype(BF16)

    cq = norm(z[:, :Q_LORA], gq_ref[...])
    ckv = norm(z[:, Q_LORA:Q_LORA + KV_LORA], gkv_ref[...])
    q = _dot(cq, wq_ref[...]) * (QK_DIM ** -0.5 * LOG2E)
    kv = _dot(ckv, wkv_ref[...])

    ang = pos_ref[...].astype(F32) * invf_ref[...]
    lane = _iota((tm, LANES), 1)
    half = ROPE_DIM // 2
    cos, sin = jnp.cos(ang), jnp.sin(ang)
    c_tab = jnp.where(lane < ROPE_DIM, cos, 0.0)
    s_up = jnp.where((lane >= half) & (lane < ROPE_DIM), sin, 0.0)
    s_dn = jnp.where(lane < half, -sin, 0.0)

    def rope(x):
        return (x * c_tab + pltpu.roll(x, half, 1) * s_up
                + pltpu.roll(x, LANES - half, 1) * s_dn)

    k_r = rope(z[:, Q_LORA + KV_LORA:]).astype(BF16)
    for h in range(MLA_HEADS):
        lo = h * QK_PAD
        q_ref[:, lo:lo + NOPE_DIM] = q[:, lo:lo + NOPE_DIM].astype(BF16)
        q_ref[:, lo + NOPE_DIM:lo + QK_PAD] = rope(q[:, lo + NOPE_DIM:lo + QK_PAD]).astype(BF16)
        k_ref[:, lo:lo + NOPE_DIM] = kv[:, h * NOPE_DIM:(h + 1) * NOPE_DIM].astype(BF16)
        k_ref[:, lo + NOPE_DIM:lo + QK_PAD] = k_r
    v_ref[...] = kv[:, MLA_HEADS * NOPE_DIM:].astype(BF16)


def _mla_proj(z_mla, pos, invf, g_q, g_kv, wq, wkv, tm=256):
    m, w = z_mla.shape
    full = lambda shape: pl.BlockSpec(shape, lambda i: (0, 0))
    row = lambda n: pl.BlockSpec((tm, n), lambda i: (i, 0))
    return pl.pallas_call(
        _mla_proj_kernel,
        grid=(m // tm,),
        in_specs=[row(w), row(1), full((1, LANES)), full((1, Q_LORA)), full((1, KV_LORA)),
                  full(wq.shape), full(wkv.shape)],
        out_specs=[row(MLA_HEADS * QK_PAD), row(MLA_HEADS * QK_PAD), row(MLA_HEADS * V_DIM)],
        out_shape=[jax.ShapeDtypeStruct((m, MLA_HEADS * QK_PAD), BF16),
                   jax.ShapeDtypeStruct((m, MLA_HEADS * QK_PAD), BF16),
                   jax.ShapeDtypeStruct((m, MLA_HEADS * V_DIM), BF16)],
        compiler_params=_cparams(("parallel",)),
        name="mla_proj",
    )(z_mla, pos, invf, g_q.reshape(1, -1), g_kv.reshape(1, -1), wq, wkv)


def _mla_attn_kernel(q_ref, k_ref, v_ref, o_ref):
    tq = q_ref.shape[0]
    nk = k_ref.shape[0] // ATTN_TK
    q = q_ref[...]

    def scores(j):
        return _dot_nt(q, k_ref[j * ATTN_TK:(j + 1) * ATTN_TK, :])

    m = jnp.full((tq, 1), -jnp.inf, F32)
    l = jnp.zeros((tq, 1), F32)
    acc = jnp.zeros((tq, V_DIM), F32)
    s_next = scores(0)
    for j in range(nk):
        s = s_next
        if j + 1 < nk:
            s_next = scores(j + 1)
        m_new = jnp.maximum(m, jnp.max(s, axis=-1, keepdims=True))
        alpha = jnp.exp2(m - m_new)
        p = jnp.exp2(s - m_new)
        l = alpha * l + jnp.sum(p, axis=-1, keepdims=True)
        acc = alpha * acc + _dot(p.astype(BF16), v_ref[j * ATTN_TK:(j + 1) * ATTN_TK, :])
        m = m_new
    o_ref[...] = (acc / l).astype(o_ref.dtype)


def _mla_attn(q, k, v, tq=512):
    b, s, _ = q.shape
    return pl.pallas_call(
        _mla_attn_kernel,
        grid=(b, MLA_HEADS, s // tq),
        in_specs=[pl.BlockSpec((None, tq, QK_PAD), lambda b_, h, i: (b_, i, h)),
                  pl.BlockSpec((None, s, QK_PAD), lambda b_, h, i: (b_, 0, h)),
                  pl.BlockSpec((None, s, V_DIM), lambda b_, h, i: (b_, 0, h))],
        out_specs=pl.BlockSpec((None, tq, V_DIM), lambda b_, h, i: (b_, i, h)),
        out_shape=jax.ShapeDtypeStruct((b, s, MLA_HEADS * V_DIM), BF16),
        compiler_params=_cparams(("parallel", "parallel", "arbitrary")),
        name="mla_attn",
    )(q, k, v)


def _rwkv_prep_kernel(seq_len,
                      r_ref, rp_ref, rn_ref, k_ref, kp_ref, kn_ref, v_ref, vp_ref, vn_ref,
                      l_ref, lp_ref, ln_ref, scr_ref, sck_ref, scv_ref, scl_ref,
                      w2f_ref, w2b_ref, a2f_ref, a2b_ref, g2_ref,
                      w0f_ref, w0b_ref, a0f_ref, a0b_ref, kk_w_ref, ka_ref,
                      ro_ref, vo_ref, kko_ref, lwf_ref, lwb_ref, alf_ref, alb_ref,
                      kdf_ref, kdb_ref, g_ref):
    tm = r_ref.shape[0]
    i = pl.program_id(0)
    first = lax.rem(i * tm, seq_len) == 0
    last = lax.rem((i + 1) * tm, seq_len) == 0

    def shift(z_ref, zp_ref, zn_ref, w_ref):
        z = z_ref[...]
        rows = _iota(z.shape, 0)
        prev_row = jnp.where(first, 0.0, zp_ref[SUBLANES - 1:SUBLANES, :])
        next_row = jnp.where(last, 0.0, zn_ref[0:1, :])
        z_prev = jnp.where(rows == 0, prev_row, pltpu.roll(z, 1, 0))
        z_next = jnp.where(rows == tm - 1, next_row, pltpu.roll(z, tm - 1, 0))
        return w_ref[0:1, :] * z_prev + w_ref[1:2, :] * z + w_ref[2:3, :] * z_next

    r = shift(r_ref, rp_ref, rn_ref, scr_ref)
    k = shift(k_ref, kp_ref, kn_ref, sck_ref)
    v = shift(v_ref, vp_ref, vn_ref, scv_ref)
    lo = shift(l_ref, lp_ref, ln_ref, scl_ref)
    xw_f = lo[:, 0 * LORA_PAD:1 * LORA_PAD]
    xw_b = lo[:, 1 * LORA_PAD:2 * LORA_PAD]
    xa_f = lo[:, 2 * LORA_PAD:3 * LORA_PAD]
    xa_b = lo[:, 3 * LORA_PAD:4 * LORA_PAD]
    xg = lo[:, 4 * LORA_PAD:]

    def log_decay(xw, w0_ref, w2_ref):
        y = -(w0_ref[...] + _dot(jnp.tanh(xw).astype(BF16), w2_ref[...]))
        softplus = jnp.maximum(y, 0.0) + jnp.log(1.0 + jnp.exp(-jnp.abs(y)))
        return -jnp.exp(-softplus - 0.5)

    def rate(xa, a0_ref, a2_ref):
        return _sigmoid(a0_ref[...] + _dot(xa.astype(BF16), a2_ref[...]))

    al_f = rate(xa_f, a0f_ref, a2f_ref)
    al_b = rate(xa_b, a0b_ref, a2b_ref)
    kk = k * kk_w_ref[...]
    kk = kk * lax.rsqrt(_head_sum(kk * kk, _head_ones()) + 1e-12)
    ka = ka_ref[...]

    ro_ref[...] = r
    vo_ref[...] = v
    kko_ref[...] = kk
    lwf_ref[...] = log_decay(xw_f, w0f_ref, w2f_ref)
    lwb_ref[...] = log_decay(xw_b, w0b_ref, w2b_ref)
    alf_ref[...] = al_f
    alb_ref[...] = al_b
    kdf_ref[...] = k * (1.0 + (al_f - 1.0) * ka)
    kdb_ref[...] = k * (1.0 + (al_b - 1.0) * ka)
    g_ref[...] = _dot(_sigmoid(xg).astype(BF16), g2_ref[...])


def _rwkv_prep(z_rkv, z_lora, seq_len, sc_rkv, sc_lora, w2f, w2b, a2f, a2b, g2,
               w0f, w0b, a0f, a0b, k_k, k_a, tm=512):
    m = z_rkv.shape[0]
    wl = z_lora.shape[1]
    nb = RWKV_WIDTH // LANES
    rb = tm // SUBLANES
    nrb = m // SUBLANES

    def main(seg):
        return pl.BlockSpec((tm, LANES), lambda i, p: (i, seg * nb + p))

    def prev(seg):
        return pl.BlockSpec((SUBLANES, LANES),
                            lambda i, p: (jnp.maximum(i * rb - 1, 0), seg * nb + p))

    def nxt(seg):
        return pl.BlockSpec((SUBLANES, LANES),
                            lambda i, p: (jnp.minimum((i + 1) * rb, nrb - 1), seg * nb + p))

    def sc(seg):
        return pl.BlockSpec((3, LANES), lambda i, p: (0, seg * nb + p))

    colblk = lambda rows: pl.BlockSpec((rows, LANES), lambda i, p: (0, p))
    in_specs = []
    for seg in range(3):
        in_specs += [main(seg), prev(seg), nxt(seg)]
    in_specs += [pl.BlockSpec((tm, wl), lambda i, p: (i, 0)),
                 pl.BlockSpec((SUBLANES, wl), lambda i, p: (jnp.maximum(i * rb - 1, 0), 0)),
                 pl.BlockSpec((SUBLANES, wl), lambda i, p: (jnp.minimum((i + 1) * rb, nrb - 1), 0))]
    in_specs += [sc(0), sc(1), sc(2), pl.BlockSpec((3, wl), lambda i, p: (0, 0))]
    in_specs += [colblk(LORA_PAD)] * 4 + [colblk(GATE_LORA)] + [colblk(1)] * 6
    out_spec = pl.BlockSpec((tm, LANES), lambda i, p: (i, p))
    n_out = 10
    row = lambda a: a.reshape(1, -1)
    return pl.pallas_call(
        functools.partial(_rwkv_prep_kernel, seq_len),
        grid=(m // tm, nb),
        in_specs=in_specs,
        out_specs=[out_spec] * n_out,
        out_shape=[jax.ShapeDtypeStruct((m, RWKV_WIDTH), F32)] * n_out,
        compiler_params=_cparams(("parallel", "arbitrary")),
        name="rwkv_prep",
    )(z_rkv, z_rkv, z_rkv, z_rkv, z_rkv, z_rkv, z_rkv, z_rkv, z_rkv,
      z_lora, z_lora, z_lora, sc_rkv, sc_rkv, sc_rkv, sc_lora,
      w2f, w2b, a2f, a2b, g2, row(w0f), row(w0b), row(a0f), row(a0b), row(k_k), row(k_a))


def _scan_chunk(r, v, kk, lw, al, kd, rk, ht, reverse):
    return _scan_chunks([(r, v, kk, lw, al, kd, rk, ht, reverse)])[0]


def _scan_chunks(chains):
    c = CHUNK
    n = PAIR * c
    nc = range(len(chains))
    r, v, kk, lw, al, kd, rk, ht, rev = [list(t) for t in zip(*chains)]
    t_i = _iota((c, c), 0)
    s_i = _iota((c, c), 1)
    tri = {False: jnp.where(s_i <= t_i, 1.0, 0.0).astype(BF16),
           True: jnp.where(s_i >= t_i, 1.0, 0.0).astype(BF16)}
    row = _iota((n, n), 0)
    col = _iota((n, n), 1)
    strict = {False: col < row, True: col > row}
    incl = {False: col <= row, True: col >= row}
    eye = jnp.where(row == col, 1.0, 0.0)
    head0 = _iota((c, LANES), 1) < RWKV_HEAD
    ones_bd = _head_ones()

    def stack(x):
        xb = x.astype(BF16)
        zero = jnp.zeros_like(xb)
        return jnp.concatenate([jnp.where(head0, xb, zero), jnp.where(head0, zero, xb)], axis=0)

    def cat(x, y, axis=0):
        return jnp.concatenate([x, y], axis=axis)

    a = [-kk[i] for i in nc]
    b = [kk[i] * al[i] for i in nc]
    lw_hi = [lw[i].astype(BF16) for i in nc]
    lw_lo = [(lw[i] - lw_hi[i].astype(F32)).astype(BF16) for i in nc]
    cum = [_dot(tri[rev[i]], lw_hi[i]) + _dot(tri[rev[i]], lw_lo[i]) for i in nc]
    cum_prev = [cum[i] - lw[i] for i in nc]
    ref_row = [c // 2 if rev[i] else c // 2 - 1 for i in nc]
    tot_row = [0 if rev[i] else c - 1 for i in nc]
    c_ref = [cum[i][ref_row[i]:ref_row[i] + 1, :] for i in nc]
    c_tot = [cum[i][tot_row[i]:tot_row[i] + 1, :] for i in nc]
    e_inv = [jnp.exp(c_ref[i] - cum[i]) for i in nc]
    e_out = [jnp.exp(c_tot[i] - cum[i]) for i in nc]
    lhs = [cat(stack(a[i] * jnp.exp(cum_prev[i] - c_ref[i])), stack(r[i] * jnp.exp(cum[i] - c_ref[i]))) for i in nc]
    rhs = [cat(stack(b[i] * e_inv[i]), stack(kd[i] * e_inv[i])) for i in nc]
    nmat = [_dot_nt(lhs[i], rhs[i]) for i in nc]
    n_ab = [jnp.where(strict[rev[i]], nmat[i][:n, :n], 0.0) for i in nc]
    n_ak = [jnp.where(strict[rev[i]], nmat[i][:n, n:], 0.0).astype(BF16) for i in nc]
    n_r = [jnp.where(cat(incl[rev[i]], incl[rev[i]], axis=1), nmat[i][n:, :], 0.0).astype(BF16) for i in nc]

    pw = n_ab
    tinv = [eye + pw[i] for i in nc]
    for _ in range(int(np.log2(c)) - 1):
        pb = [pw[i].astype(BF16) for i in nc]
        pw = [_dot(pb[i], pb[i]) for i in nc]
        tinv = [tinv[i] + _dot(tinv[i].astype(BF16), pw[i].astype(BF16)) for i in nc]

    v_s = [stack(v[i]) for i in nc]
    state_lhs = [cat(stack(a[i] * jnp.exp(cum_prev[i])), stack(r[i] * jnp.exp(cum[i]))) for i in nc]
    from_state = [_dot_nt(state_lhs[i], ht[i].astype(BF16)) for i in nc]
    from_v = [_dot(n_ak[i], v_s[i]) for i in nc]
    u = [_dot(tinv[i].astype(BF16), from_state[i][:n] + from_v[i]).astype(BF16) for i in nc]
    uv = [cat(u[i], v_s[i]) for i in nc]
    o_s = [from_state[i][n:] + _dot(n_r[i], uv[i]) for i in nc]
    out_rhs = [cat(stack(b[i] * e_out[i]), stack(kd[i] * e_out[i])) for i in nc]
    ht_new = [ht[i] * jnp.exp(c_tot[i]) + _dot_tn(uv[i], out_rhs[i]) for i in nc]
    bonus = [_head_sum(r[i] * kd[i] * rk[i], ones_bd) * v[i] for i in nc]
    return [(o_s[i][:c] + o_s[i][c:] + bonus[i], ht_new[i]) for i in nc]


def _rwkv_scan_kernel(rf, vf, kkf, lwf, alf, kdf, rb, vb, kkb, lwb, alb, kdb, rk_ref,
                      yf_ref, yb_ref, h_ref):
    @pl.when(pl.program_id(2) == 0)
    def _():
        h_ref[...] = jnp.zeros_like(h_ref)

    n_chunks = rf.shape[0] // CHUNK
    n_pairs = rf.shape[1] // LANES

    def body(ci, carry):
        chains = []
        for d, (refs, y_ref) in enumerate((((rf, vf, kkf, lwf, alf, kdf), yf_ref),
                                           ((rb, vb, kkb, lwb, alb, kdb), yb_ref))):
            cidx = ci if d == 0 else n_chunks - 1 - ci
            rows = pl.ds(pl.multiple_of(cidx * CHUNK, CHUNK), CHUNK)
            for p in range(n_pairs):
                lanes = slice(p * LANES, (p + 1) * LANES)
                vals = [x[rows, lanes] for x in refs] + [rk_ref[:, lanes], h_ref[d, p]]
                chains.append((d, p, y_ref, rows, lanes, vals))
        results = _scan_chunks([tuple(vals) + (d == 1,) for d, _, _, _, _, vals in chains])
        for (d, p, y_ref, rows, lanes, _), (y, ht) in zip(chains, results):
            y_ref[rows, lanes] = y
            h_ref[d, p] = ht
        return carry

    lax.fori_loop(0, n_chunks, body, 0)


def _rwkv_scan(r, v, kk, lw_f, lw_b, al_f, al_b, kd_f, kd_b, r_k, tb=512, pairs=4):
    b, s, w = r.shape
    nblk = s // tb
    wb = pairs * LANES
    fwd = pl.BlockSpec((None, tb, wb), lambda b_, p, j: (b_, j, p))
    bwd = pl.BlockSpec((None, tb, wb), lambda b_, p, j: (b_, nblk - 1 - j, p))
    return pl.pallas_call(
        _rwkv_scan_kernel,
        grid=(b, w // wb, nblk),
        in_specs=[fwd] * 6 + [bwd] * 6 + [pl.BlockSpec((1, wb), lambda b_, p, j: (0, p))],
        out_specs=[fwd, bwd],
        out_shape=[jax.ShapeDtypeStruct((b, s, w), F32)] * 2,
        scratch_shapes=[pltpu.VMEM((2, pairs, LANES, LANES), F32)],
        compiler_params=_cparams(("parallel", "parallel", "arbitrary")),
        name="rwkv_scan",
    )(r, v, kk, lw_f, al_f, kd_f, r, v, kk, lw_b, al_b, kd_b, r_k.reshape(1, -1))


def _rwkv_post_kernel(yf_ref, yb_ref, g_ref, lg_ref, lb_ref, o_ref):
    ones_bd = _head_ones()
    y = yf_ref[...] + yb_ref[...]
    mu = _head_sum(y, ones_bd) * (1.0 / RWKV_HEAD)
    d = y - mu
    var = _head_sum(d * d, ones_bd) * (1.0 / RWKV_HEAD)
    yn = d * lax.rsqrt(var + LNX_EPS) * lg_ref[...] + lb_ref[...]
    o_ref[...] = (yn * g_ref[...]).astype(o_ref.dtype)


def _rwkv_post(y_f, y_b, g, lnx_g, lnx_b, tm=512):
    m, w = y_f.shape
    blk = pl.BlockSpec((tm, LANES), lambda i, p: (i, p))
    vec = pl.BlockSpec((1, LANES), lambda i, p: (0, p))
    return pl.pallas_call(
        _rwkv_post_kernel,
        grid=(m // tm, w // LANES),
        in_specs=[blk, blk, blk, vec, vec],
        out_specs=blk,
        out_shape=jax.ShapeDtypeStruct((m, w), BF16),
        compiler_params=_cparams(("parallel", "parallel")),
        name="rwkv_post",
    )(y_f, y_b, g, lnx_g.reshape(1, -1), lnx_b.reshape(1, -1))


def _merge_kernel(a1_ref, w1_ref, a2_ref, w2_ref, g1_ref, g2_ref, o_ref):
    m1 = _dot(a1_ref[...], w1_ref[...])
    m2 = _dot(a2_ref[...], w2_ref[...])
    o_ref[...] = (_sigmoid(g1_ref[...]) * m1 + _sigmoid(g2_ref[...]) * m2).astype(o_ref.dtype)


def _merge(o_mla, w_up_mla, o_rwkv, w_up_rwkv, z_gate, tm=1024, tn=512):
    m, k = o_mla.shape
    n = w_up_mla.shape[1]
    nj = n // tn
    a_spec = pl.BlockSpec((tm, k), lambda i, j: (i, 0))
    w_spec = pl.BlockSpec((k, tn), lambda i, j: (0, j))
    return pl.pallas_call(
        _merge_kernel,
        grid=(m // tm, nj),
        in_specs=[a_spec, w_spec, a_spec, w_spec,
                  pl.BlockSpec((tm, tn), lambda i, j: (i, j)),
                  pl.BlockSpec((tm, tn), lambda i, j: (i, nj + j))],
        out_specs=pl.BlockSpec((tm, tn), lambda i, j: (i, j)),
        out_shape=jax.ShapeDtypeStruct((m, n), BF16),
        compiler_params=_cparams(("parallel", "arbitrary")),
        name="merge",
    )(o_mla, w_up_mla, o_rwkv, w_up_rwkv, z_gate, z_gate)


def _cross_router_kernel(h_ref, gc_ref, wq_ref, kv_ref, wo_ref, gf_ref, wr_ref, br_ref,
                         h2_ref, n3_ref, comb_ref):
    h = h_ref[...]
    hn = (h * lax.rsqrt(jnp.mean(h * h, axis=-1, keepdims=True) + RMS_EPS) * gc_ref[...]).astype(BF16)
    q = (_dot(hn, wq_ref[...]) * (CROSS_HEAD_DIM ** -0.5)).astype(BF16)
    kv = kv_ref[...]
    outs = []
    for hd in range(CROSS_HEADS):
        lo = hd * CROSS_HEAD_DIM
        s = _dot_nt(q[:, lo:lo + CROSS_HEAD_DIM], kv[:, lo:lo + CROSS_HEAD_DIM])
        p = jnp.exp(s - jnp.max(s, axis=-1, keepdims=True))
        p = p / jnp.sum(p, axis=-1, keepdims=True)
        outs.append(_dot(p.astype(BF16), kv[:, CROSS_WIDTH + lo:CROSS_WIDTH + lo + CROSS_HEAD_DIM]))
    o = jnp.concatenate(outs, axis=-1).astype(BF16)
    h2 = h + _dot(o, wo_ref[...])
    h2_ref[...] = h2

    n3 = h2 * lax.rsqrt(jnp.mean(h2 * h2, axis=-1, keepdims=True) + RMS_EPS) * gf_ref[...]
    n3_ref[...] = n3.astype(BF16)

    logits = jnp.dot(n3, wr_ref[...], preferred_element_type=F32,
                     precision=lax.Precision.HIGHEST) + br_ref[...]
    lane = _iota(logits.shape, 1)
    lane_f = lane.astype(F32)
    neg = jnp.float32(-jnp.inf)
    big = jnp.float32(1e9)

    def masked_softmax(mask):
        x = jnp.where(mask, logits, neg)
        e = jnp.exp(x - jnp.max(x, axis=-1, keepdims=True))
        return e / jnp.sum(e, axis=-1, keepdims=True)

    def top1(prob, mask):
        pmax = jnp.max(jnp.where(mask, prob, -1.0), axis=-1, keepdims=True)
        idx = jnp.min(jnp.where(mask & (prob == pmax), lane_f, big), axis=-1, keepdims=True)
        return pmax, idx

    g_mask = (lane >= N_EXPERTS) & (lane < N_EXPERTS + N_GROUPS)
    p_group, g_idx = top1(masked_softmax(g_mask), g_mask)
    g_sel = g_idx - float(N_EXPERTS)
    e_mask = (lane < N_EXPERTS) & (lax.shift_right_logical(lane, 3).astype(F32) == g_sel)
    e_prob = masked_softmax(e_mask)
    p1, i1 = top1(e_prob, e_mask)
    rest = e_mask & (lane_f != i1)
    p2, i2 = top1(e_prob, rest)
    denom = p1 + p2
    comb_ref[...] = jnp.where(lane_f == i1, p_group * (p1 / denom),
                              jnp.where(lane_f == i2, p_group * (p2 / denom), 0.0))


def _cross_router(h1, seq_len, g_cross, wq, kvm, wo, g_ffn, w_r, b_r, tm=256):
    m, d = h1.shape
    full = lambda a: pl.BlockSpec(a.shape, lambda i: (0,) * a.ndim)
    row = lambda n: pl.BlockSpec((tm, n), lambda i: (i, 0))
    gc, gf = g_cross.reshape(1, d), g_ffn.reshape(1, d)
    per_seq = seq_len // tm
    return pl.pallas_call(
        _cross_router_kernel,
        grid=(m // tm,),
        in_specs=[row(d), full(gc), full(wq),
                  pl.BlockSpec((None,) + kvm.shape[1:], lambda i: (i // per_seq, 0, 0)),
                  full(wo), full(gf), full(w_r), full(b_r)],
        out_specs=[row(d), row(d), row(LANES)],
        out_shape=[jax.ShapeDtypeStruct((m, d), F32), jax.ShapeDtypeStruct((m, d), BF16),
                   jax.ShapeDtypeStruct((m, LANES), F32)],
        compiler_params=_cparams(("parallel",)),
        name="cross_router",
    )(h1, gc, wq, kvm, wo, gf, w_r, b_r)


def _moe_kernel(n_ref, comb_ref, h_ref, wg_ref, wu_ref, wd_ref, gfin_ref, o_ref, acc_ref):
    e = pl.program_id(1)

    @pl.when(e == 0)
    def _():
        acc_ref[...] = h_ref[...]

    n = n_ref[...]
    comb = comb_ref[...]
    lane = _iota(comb.shape, 1)
    gate = jnp.sum(jnp.where(lane == e, comb, 0.0), axis=-1, keepdims=True)
    hg = _dot(n, wg_ref[...])
    hu = _dot(n, wu_ref[...])
    hid = (hg * _sigmoid(hg) * hu * gate).astype(BF16)
    acc_ref[...] += _dot(hid, wd_ref[...])

    @pl.when(e == pl.num_programs(1) - 1)
    def _():
        y = acc_ref[...]
        o_ref[...] = y * lax.rsqrt(jnp.mean(y * y, axis=-1, keepdims=True) + RMS_EPS) * gfin_ref[...]


def _moe(n3, comb, h2, w_eg, w_eu, w_ed, g_final, tm=512):
    m, d = n3.shape
    ne, _, f = w_eg.shape
    row = lambda n: pl.BlockSpec((tm, n), lambda i, e: (i, 0))
    return pl.pallas_call(
        _moe_kernel,
        grid=(m // tm, ne),
        in_specs=[row(d), row(LANES), row(d),
                  pl.BlockSpec((None, d, f), lambda i, e: (e, 0, 0)),
                  pl.BlockSpec((None, d, f), lambda i, e: (e, 0, 0)),
                  pl.BlockSpec((None, f, d), lambda i, e: (e, 0, 0)),
                  pl.BlockSpec((1, d), lambda i, e: (0, 0))],
        out_specs=row(d),
        out_shape=jax.ShapeDtypeStruct((m, d), F32),
        scratch_shapes=[pltpu.VMEM((tm, d), F32)],
        compiler_params=_cparams(("parallel", "arbitrary")),
        name="moe",
    )(n3, comb, h2, w_eg, w_eu, w_ed, g_final.reshape(1, d))


def _pad_cols(w, n):
    return jnp.pad(w, ((0, 0), (0, n - w.shape[1])))


def _pad_rows(w, n):
    return jnp.pad(w, ((0, n - w.shape[0]), (0, 0)))


def _split_lora(w):
    o = 0
    parts = []
    for width in (DECAY_LORA, DECAY_LORA, ICLR_LORA, ICLR_LORA):
        parts.append(_pad_cols(w[:, o:o + width], LORA_PAD))
        o += width
    parts.append(w[:, o:o + GATE_LORA])
    return jnp.concatenate(parts, axis=1)


def kernel(x, mem, positions, g_mix, w_in, g_q, w_uq, g_kv, w_ukv, shift_conv, w0_f, w2_f, w0_b, w2_b, a0_f, a2_f, a0_b, a2_b, g2, k_k, k_a, r_k, lnx_g, lnx_b, w_up_mla, w_up_rwkv, w_out, g_cross, g_mem, wq_c, wkv_c, wo_c, g_ffn, w_rg, b_rg, w_re, b_re, w_eg, w_eu, w_ed, g_final):
    bsz, seq, d = x.shape
    m = bsz * seq
    depth = w_in.shape[0]
    h = x.reshape(m, d)
    pos = positions.reshape(m, 1)
    lane = np.arange(LANES)
    invf = jnp.asarray(np.where(lane < ROPE_DIM, 1.0, 0.0), F32) * (
        ROPE_BASE ** (-jnp.asarray(lane % (ROPE_DIM // 2), F32) * (2.0 / ROPE_DIM)))
    invf = invf.reshape(1, LANES)
    assert depth == 1, "the MoE kernel applies the final norm, so it must be the last layer"
    for l in range(depth):
        wi = w_in[l]
        w_mla = _pad_cols(wi[:, :MLA_IN], MLA_IN + (LANES - ROPE_DIM)).astype(BF16)
        rw = wi[:, MLA_IN:MLA_IN + RWKV_IN]
        w_rkv = rw[:, :3 * RWKV_WIDTH].astype(BF16)
        w_lora = _split_lora(rw[:, 3 * RWKV_WIDTH:]).astype(BF16)
        w_gate = wi[:, MLA_IN + RWKV_IN:].astype(BF16)
        sc = shift_conv[l]
        sc_rkv = sc[:, :3 * RWKV_WIDTH]
        sc_lora = _split_lora(sc[:, 3 * RWKV_WIDTH:])
        wq = w_uq[l].reshape(Q_LORA, MLA_HEADS, QK_DIM)
        wq = jnp.pad(wq, ((0, 0), (0, 0), (0, QK_PAD - QK_DIM))).reshape(Q_LORA, MLA_HEADS * QK_PAD)
        wkv = w_ukv[l].reshape(KV_LORA, MLA_HEADS, NOPE_DIM + V_DIM)
        wkv = jnp.concatenate([wkv[:, :, :NOPE_DIM].reshape(KV_LORA, -1),
                               wkv[:, :, NOPE_DIM:].reshape(KV_LORA, -1)], axis=1)
        lora_rows = lambda w: _pad_rows(w, LORA_PAD).astype(BF16)
        w_router = _pad_cols(jnp.concatenate(
            [jnp.moveaxis(w_re[l], 0, 1).reshape(d, N_EXPERTS), w_rg[l]], axis=1), LANES)
        b_router = _pad_cols(jnp.concatenate([b_re[l].reshape(1, N_EXPERTS), b_rg[l].reshape(1, N_GROUPS)],
                                             axis=1), LANES)

        n1 = _rmsnorm(h, g_mix[l])
        z_mla = _matmul(n1, w_mla, F32, 1024, w_mla.shape[1], name="in_proj_mla")
        z_rkv = _matmul(n1, w_rkv, F32, 1024, 512, name="in_proj_rkv")
        z_lora = _matmul(n1, w_lora, F32, 1024, w_lora.shape[1], name="in_proj_lora")
        z_gate = _matmul(n1, w_gate, F32, 1024, 512, name="in_proj_gate")

        q_cat, k_cat, v_mla = _mla_proj(z_mla, pos, invf, g_q[l], g_kv[l], wq.astype(BF16), wkv.astype(BF16))
        o_mla = _mla_attn(q_cat.reshape(bsz, seq, -1), k_cat.reshape(bsz, seq, -1),
                          v_mla.reshape(bsz, seq, -1)).reshape(m, -1)

        (r, v, kk, lw_f, lw_b, al_f, al_b, kd_f, kd_b, gate) = _rwkv_prep(
            z_rkv, z_lora, seq, sc_rkv, sc_lora, lora_rows(w2_f[l]), lora_rows(w2_b[l]),
            lora_rows(a2_f[l]), lora_rows(a2_b[l]), g2[l].astype(BF16),
            w0_f[l], w0_b[l], a0_f[l], a0_b[l], k_k[l], k_a[l])
        sh = lambda t: t.reshape(bsz, seq, RWKV_WIDTH)
        y_f, y_b = _rwkv_scan(sh(r), sh(v), sh(kk), sh(lw_f), sh(lw_b), sh(al_f), sh(al_b),
                              sh(kd_f), sh(kd_b), r_k[l])
        o_rwkv = _rwkv_post(y_f.reshape(m, -1), y_b.reshape(m, -1), gate, lnx_g[l], lnx_b[l])

        merged = _merge(o_mla, w_up_mla[l].astype(BF16), o_rwkv, w_up_rwkv[l].astype(BF16), z_gate)
        h1 = _matmul(merged, w_out[l].astype(BF16), F32, 1024, 512, res=h, name="out_proj")

        mem_n = _rmsnorm(mem.reshape(bsz * MEM_LEN, d), g_mem[l], tm=MEM_LEN)
        kvm = _matmul(mem_n, wkv_c[l].astype(BF16), BF16, bsz * MEM_LEN, 512, name="mem_kv")
        h2, n3, comb = _cross_router(h1, seq, g_cross[l], wq_c[l].astype(BF16),
                                     kvm.reshape(bsz, MEM_LEN, 2 * CROSS_WIDTH), wo_c[l].astype(BF16),
                                     g_ffn[l], w_router, b_router)

        flat = lambda w: w.reshape((N_EXPERTS,) + w.shape[2:]).astype(BF16)
        out = _moe(n3, comb, h2, flat(w_eg[l]), flat(w_eu[l]), flat(w_ed[l]), g_final)
    return out.reshape(bsz, seq, d)
```

```python
import functools

import jax
import jax.numpy as jnp
import numpy as np
from jax import lax
from jax.experimental import pallas as pl
from jax.experimental.pallas import tpu as pltpu

F32 = jnp.float32
BF16 = jnp.bfloat16

D_MODEL = 2048
MEM_LEN = 256
RMS_EPS = 1e-6
MLA_HEADS = 16
Q_LORA = 512
KV_LORA = 256
NOPE_DIM = 128
ROPE_DIM = 64
V_DIM = 128
QK_DIM = NOPE_DIM + ROPE_DIM
ROPE_BASE = 10000.0
RWKV_HEAD = 64
RWKV_HEADS = D_MODEL // RWKV_HEAD
RWKV_WIDTH = D_MODEL
DECAY_LORA = 96
ICLR_LORA = 96
GATE_LORA = 256
LNX_EPS = 64e-5
CROSS_HEADS = 4
CROSS_HEAD_DIM = 128
CROSS_WIDTH = CROSS_HEADS * CROSS_HEAD_DIM
N_GROUPS = 4
EXPERTS_PER_GROUP = 8
N_EXPERTS = N_GROUPS * EXPERTS_PER_GROUP
D_EXPERT = 512
MLA_IN = Q_LORA + KV_LORA + ROPE_DIM
RWKV_IN = 3 * RWKV_WIDTH + 2 * DECAY_LORA + 2 * ICLR_LORA + GATE_LORA

LANES = 128
SUBLANES = 8
QK_PAD = 256
LORA_PAD = LANES
QUAD = 4
QW = QUAD * RWKV_HEAD
CHUNK = 64
ATTN_TK = 512
MOE_TILE = 256
GATHER_UNROLL = 8
LOG2E = 1.4426950408889634
VMEM_LIMIT = 56 * 1024 * 1024


def _cparams(sem, vmem=VMEM_LIMIT, flags=None):
    return pltpu.CompilerParams(dimension_semantics=sem, vmem_limit_bytes=vmem, flags=flags)


def _iota(shape, dim):
    return lax.broadcasted_iota(jnp.int32, shape, dim)


def _sigmoid(x):
    return 1.0 / (1.0 + jnp.exp(-x))


def _dot(a, b):
    return jnp.dot(a, b, preferred_element_type=F32)


def _dot_nt(a, b):
    return lax.dot_general(a, b, (((1,), (1,)), ((), ())), preferred_element_type=F32)


def _dot_tn(a, b):
    return lax.dot_general(a, b, (((0,), (0,)), ((), ())), preferred_element_type=F32)


def _head_ones():
    r = lax.shift_right_logical(_iota((LANES, LANES), 0), 6)
    c = lax.shift_right_logical(_iota((LANES, LANES), 1), 6)
    return jnp.where(r == c, 1.0, 0.0).astype(BF16)


def _head_sum(x, ones_bd):
    w = ones_bd.shape[0]
    hi = x.astype(BF16)
    lo = (x - hi.astype(F32)).astype(BF16)
    parts = [_dot(hi[:, j:j + w], ones_bd) + _dot(lo[:, j:j + w], ones_bd) for j in range(0, x.shape[1], w)]
    return parts[0] if len(parts) == 1 else jnp.concatenate(parts, axis=1)


def _rmsnorm_kernel(x_ref, g_ref, o_ref):
    x = x_ref[...]
    y = x * lax.rsqrt(jnp.mean(x * x, axis=-1, keepdims=True) + RMS_EPS)
    o_ref[...] = (y * g_ref[...]).astype(o_ref.dtype)


def _rmsnorm(x, g, tm=512):
    m, d = x.shape
    return pl.pallas_call(
        _rmsnorm_kernel,
        grid=(m // tm,),
        in_specs=[pl.BlockSpec((tm, d), lambda i: (i, 0)),
                  pl.BlockSpec((1, d), lambda i: (0, 0))],
        out_specs=pl.BlockSpec((tm, d), lambda i: (i, 0)),
        out_shape=jax.ShapeDtypeStruct((m, d), BF16),
        compiler_params=_cparams(("parallel",)),
        name="rmsnorm",
    )(x, g.reshape(1, d))


def _mm_kernel(a_ref, b_ref, o_ref):
    o_ref[...] = _dot(a_ref[...], b_ref[...]).astype(o_ref.dtype)


def _mm_res_kernel(a_ref, b_ref, r_ref, o_ref):
    o_ref[...] = (r_ref[...] + _dot(a_ref[...], b_ref[...])).astype(o_ref.dtype)


def _matmul(a, b, out_dtype, tm, tn, res=None, name="matmul"):
    m, k = a.shape
    n = b.shape[1]
    in_specs = [pl.BlockSpec((tm, k), lambda i, j: (i, 0)),
                pl.BlockSpec((k, tn), lambda i, j: (0, j))]
    args = [a, b]
    kern = _mm_kernel
    if res is not None:
        in_specs.append(pl.BlockSpec((tm, tn), lambda i, j: (i, j)))
        args.append(res)
        kern = _mm_res_kernel
    return pl.pallas_call(
        kern,
        grid=(m // tm, n // tn),
        in_specs=in_specs,
        out_specs=pl.BlockSpec((tm, tn), lambda i, j: (i, j)),
        out_shape=jax.ShapeDtypeStruct((m, n), out_dtype),
        compiler_params=_cparams(("parallel", "arbitrary")),
        name=name,
    )(*args)


def _mla_proj_kernel(z_ref, pos_ref, invf_ref, gq_ref, gkv_ref, wq_ref, wkv_ref,
                     q_ref, k_ref, v_ref):
    tm = z_ref.shape[0]
    z = z_ref[...]

    def norm(c, g):
        return (c * lax.rsqrt(jnp.mean(c * c, axis=-1, keepdims=True) + RMS_EPS) * g).astype(BF16)

    cq = norm(z[:, :Q_LORA], gq_ref[...])
    ckv = norm(z[:, Q_LORA:Q_LORA + KV_LORA], gkv_ref[...])
    q = _dot(cq, wq_ref[...]) * (QK_DIM ** -0.5 * LOG2E)
    kv = _dot(ckv, wkv_ref[...])

    ang = pos_ref[...].astype(F32) * invf_ref[...]
    lane = _iota((tm, LANES), 1)
    half = ROPE_DIM // 2
    cos, sin = jnp.cos(ang), jnp.sin(ang)
    c_tab = jnp.where(lane < ROPE_DIM, cos, 0.0)
    s_up = jnp.where((lane >= half) & (lane < ROPE_DIM), sin, 0.0)
    s_dn = jnp.where(lane < half, -sin, 0.0)

    def rope(x):
        return (x * c_tab + pltpu.roll(x, half, 1) * s_up
                + pltpu.roll(x, LANES - half, 1) * s_dn)

    k_r = rope(z[:, Q_LORA + KV_LORA:]).astype(BF16)
    for h in range(MLA_HEADS):
        lo = h * QK_PAD
        q_ref[:, lo:lo + NOPE_DIM] = q[:, lo:lo + NOPE_DIM].astype(BF16)
        q_ref[:, lo + NOPE_DIM:lo + QK_PAD] = rope(q[:, lo + NOPE_DIM:lo + QK_PAD]).astype(BF16)
        k_ref[:, lo:lo + NOPE_DIM] = kv[:, h * NOPE_DIM:(h + 1) * NOPE_DIM].astype(BF16)
        k_ref[:, lo + NOPE_DIM:lo + QK_PAD] = k_r
    v_ref[...] = kv[:, MLA_HEADS * NOPE_DIM:].astype(BF16)


def _mla_proj(z_mla, pos, invf, g_q, g_kv, wq, wkv, tm=256):
    m, w = z_mla.shape
    full = lambda shape: pl.BlockSpec(shape, lambda i: (0, 0))
    row = lambda n: pl.BlockSpec((tm, n), lambda i: (i, 0))
    return pl.pallas_call(
        _mla_proj_kernel,
        grid=(m // tm,),
        in_specs=[row(w), row(1), full((1, LANES)), full((1, Q_LORA)), full((1, KV_LORA)),
                  full(wq.shape), full(wkv.shape)],
        out_specs=[row(MLA_HEADS * QK_PAD), row(MLA_HEADS * QK_PAD), row(MLA_HEADS * V_DIM)],
        out_shape=[jax.ShapeDtypeStruct((m, MLA_HEADS * QK_PAD), BF16),
                   jax.ShapeDtypeStruct((m, MLA_HEADS * QK_PAD), BF16),
                   jax.ShapeDtypeStruct((m, MLA_HEADS * V_DIM), BF16)],
        compiler_params=_cparams(("parallel",)),
        name="mla_proj",
    )(z_mla, pos, invf, g_q.reshape(1, -1), g_kv.reshape(1, -1), wq, wkv)


def _mla_attn_kernel(q_ref, k_ref, v_ref, o_ref):
    tq = q_ref.shape[0]
    nk = k_ref.shape[0] // ATTN_TK
    q = q_ref[...]

    def scores(j):
        return _dot_nt(q, k_ref[j * ATTN_TK:(j + 1) * ATTN_TK, :])

    m = jnp.full((tq, 1), -jnp.inf, F32)
    l = jnp.zeros((tq, 1), F32)
    acc = jnp.zeros((tq, V_DIM), F32)
    s_next = scores(0)
    for j in range(nk):
        s = s_next
        if j + 1 < nk:
            s_next = scores(j + 1)
        m_new = jnp.maximum(m, jnp.max(s, axis=-1, keepdims=True))
        alpha = jnp.exp2(m - m_new)
        p = jnp.exp2(s - m_new)
        l = alpha * l + jnp.sum(p, axis=-1, keepdims=True)
        acc = alpha * acc + _dot(p.astype(BF16), v_ref[j * ATTN_TK:(j + 1) * ATTN_TK, :])
        m = m_new
    o_ref[...] = (acc / l).astype(o_ref.dtype)


def _mla_attn(q, k, v, tq=512):
    b, s, _ = q.shape
    return pl.pallas_call(
        _mla_attn_kernel,
        grid=(b, MLA_HEADS, s // tq),
        in_specs=[pl.BlockSpec((None, tq, QK_PAD), lambda b_, h, i: (b_, i, h)),
                  pl.BlockSpec((None, s, QK_PAD), lambda b_, h, i: (b_, 0, h)),
                  pl.BlockSpec((None, s, V_DIM), lambda b_, h, i: (b_, 0, h))],
        out_specs=pl.BlockSpec((None, tq, V_DIM), lambda b_, h, i: (b_, i, h)),
        out_shape=jax.ShapeDtypeStruct((b, s, MLA_HEADS * V_DIM), BF16),
        compiler_params=_cparams(("parallel", "parallel", "arbitrary")),
        name="mla_attn",
    )(q, k, v)


def _rwkv_prep_kernel(seq_len,
                      r_ref, rp_ref, rn_ref, k_ref, kp_ref, kn_ref, v_ref, vp_ref, vn_ref,
                      l_ref, lp_ref, ln_ref, scr_ref, sck_ref, scv_ref, scl_ref,
                      w2f_ref, w2b_ref, a2f_ref, a2b_ref, g2_ref,
                      w0f_ref, w0b_ref, a0f_ref, a0b_ref, kk_w_ref, ka_ref,
                      ro_ref, vo_ref, kko_ref, lwf_ref, lwb_ref, alf_ref, alb_ref,
                      kdf_ref, kdb_ref, g_ref):
    tm = r_ref.shape[0]
    i = pl.program_id(0)
    first = lax.rem(i * tm, seq_len) == 0
    last = lax.rem((i + 1) * tm, seq_len) == 0

    def shift(z_ref, zp_ref, zn_ref, w_ref):
        z = z_ref[...]
        rows = _iota(z.shape, 0)
        prev_row = jnp.where(first, 0.0, zp_ref[SUBLANES - 1:SUBLANES, :])
        next_row = jnp.where(last, 0.0, zn_ref[0:1, :])
        z_prev = jnp.where(rows == 0, prev_row, pltpu.roll(z, 1, 0))
        z_next = jnp.where(rows == tm - 1, next_row, pltpu.roll(z, tm - 1, 0))
        return w_ref[0:1, :] * z_prev + w_ref[1:2, :] * z + w_ref[2:3, :] * z_next

    r = shift(r_ref, rp_ref, rn_ref, scr_ref)
    k = shift(k_ref, kp_ref, kn_ref, sck_ref)
    v = shift(v_ref, vp_ref, vn_ref, scv_ref)
    lo = shift(l_ref, lp_ref, ln_ref, scl_ref)
    xw_f = lo[:, 0 * LORA_PAD:1 * LORA_PAD]
    xw_b = lo[:, 1 * LORA_PAD:2 * LORA_PAD]
    xa_f = lo[:, 2 * LORA_PAD:3 * LORA_PAD]
    xa_b = lo[:, 3 * LORA_PAD:4 * LORA_PAD]
    xg = lo[:, 4 * LORA_PAD:]

    def log_decay(xw, w0_ref, w2_ref):
        y = -(w0_ref[...] + _dot(jnp.tanh(xw).astype(BF16), w2_ref[...]))
        softplus = jnp.maximum(y, 0.0) + jnp.log(1.0 + jnp.exp(-jnp.abs(y)))
        return -jnp.exp(-softplus - 0.5)

    def rate(xa, a0_ref, a2_ref):
        return _sigmoid(a0_ref[...] + _dot(xa.astype(BF16), a2_ref[...]))

    al_f = rate(xa_f, a0f_ref, a2f_ref)
    al_b = rate(xa_b, a0b_ref, a2b_ref)
    kk = k * kk_w_ref[...]
    kk = kk * lax.rsqrt(_head_sum(kk * kk, _head_ones()) + 1e-12)
    ka = ka_ref[...]

    ro_ref[...] = r
    vo_ref[...] = v
    kko_ref[...] = kk
    lwf_ref[...] = log_decay(xw_f, w0f_ref, w2f_ref)
    lwb_ref[...] = log_decay(xw_b, w0b_ref, w2b_ref)
    alf_ref[...] = al_f
    alb_ref[...] = al_b
    kdf_ref[...] = k * (1.0 + (al_f - 1.0) * ka)
    kdb_ref[...] = k * (1.0 + (al_b - 1.0) * ka)
    g_ref[...] = _dot(_sigmoid(xg).astype(BF16), g2_ref[...])


def _rwkv_prep(z_rkv, z_lora, seq_len, sc_rkv, sc_lora, w2f, w2b, a2f, a2b, g2,
               w0f, w0b, a0f, a0b, k_k, k_a, tm=512, tw=512):
    m = z_rkv.shape[0]
    wl = z_lora.shape[1]
    nb = RWKV_WIDTH // tw
    rb = tm // SUBLANES
    nrb = m // SUBLANES

    def main(seg):
        return pl.BlockSpec((tm, tw), lambda i, p: (i, seg * nb + p))

    def prev(seg):
        return pl.BlockSpec((SUBLANES, tw),
                            lambda i, p: (jnp.maximum(i * rb - 1, 0), seg * nb + p))

    def nxt(seg):
        return pl.BlockSpec((SUBLANES, tw),
                            lambda i, p: (jnp.minimum((i + 1) * rb, nrb - 1), seg * nb + p))

    def sc(seg):
        return pl.BlockSpec((3, tw), lambda i, p: (0, seg * nb + p))

    colblk = lambda rows: pl.BlockSpec((rows, tw), lambda i, p: (0, p))
    in_specs = []
    for seg in range(3):
        in_specs += [main(seg), prev(seg), nxt(seg)]
    in_specs += [pl.BlockSpec((tm, wl), lambda i, p: (i, 0)),
                 pl.BlockSpec((SUBLANES, wl), lambda i, p: (jnp.maximum(i * rb - 1, 0), 0)),
                 pl.BlockSpec((SUBLANES, wl), lambda i, p: (jnp.minimum((i + 1) * rb, nrb - 1), 0))]
    in_specs += [sc(0), sc(1), sc(2), pl.BlockSpec((3, wl), lambda i, p: (0, 0))]
    in_specs += [colblk(LORA_PAD)] * 4 + [colblk(GATE_LORA)] + [colblk(1)] * 6
    out_spec = pl.BlockSpec((tm, tw), lambda i, p: (i, p))
    n_out = 10
    row = lambda a: a.reshape(1, -1)
    return pl.pallas_call(
        functools.partial(_rwkv_prep_kernel, seq_len),
        grid=(m // tm, nb),
        in_specs=in_specs,
        out_specs=[out_spec] * n_out,
        out_shape=[jax.ShapeDtypeStruct((m, RWKV_WIDTH), F32)] * n_out,
        compiler_params=_cparams(("parallel", "arbitrary")),
        name="rwkv_prep",
    )(z_rkv, z_rkv, z_rkv, z_rkv, z_rkv, z_rkv, z_rkv, z_rkv, z_rkv,
      z_lora, z_lora, z_lora, sc_rkv, sc_rkv, sc_rkv, sc_lora,
      w2f, w2b, a2f, a2b, g2, row(w0f), row(w0b), row(a0f), row(a0b), row(k_k), row(k_a))


def _scan_chunks(chains):
    c = CHUNK
    nc = range(len(chains))
    r, v, kk, lw, al, kd, rk, ht, rev = [list(t) for t in zip(*chains)]
    t_i = _iota((c, c), 0)
    s_i = _iota((c, c), 1)
    tri = {False: jnp.where(s_i <= t_i, 1.0, 0.0).astype(BF16),
           True: jnp.where(s_i >= t_i, 1.0, 0.0).astype(BF16)}
    row = _iota((c, QW), 0)
    col = jnp.bitwise_and(_iota((c, QW), 1), RWKV_HEAD - 1)
    strict = {False: col < row, True: col > row}
    incl = {False: col <= row, True: col >= row}
    eye = jnp.where(row == col, 1.0, 0.0)
    lane_head = lax.shift_right_logical(_iota((c, QW), 1), 6)
    head_mask = [lane_head == h for h in range(QUAD)]
    on_diag = (lax.shift_right_logical(_iota((QW, QW), 0), 6)
               == lax.shift_right_logical(_iota((QW, QW), 1), 6))
    ones_bd = jnp.where(on_diag, 1.0, 0.0).astype(BF16)

    def blockdiag(x):
        xb = x.astype(BF16)
        zero = jnp.zeros_like(xb)
        return jnp.concatenate([jnp.where(head_mask[h], xb, zero) for h in range(QUAD)], axis=0)

    def cat(x, y, axis=0):
        return jnp.concatenate([x, y], axis=axis)

    a = [-kk[i] for i in nc]
    b = [kk[i] * al[i] for i in nc]
    lw_hi = [lw[i].astype(BF16) for i in nc]
    lw_lo = [(lw[i] - lw_hi[i].astype(F32)).astype(BF16) for i in nc]
    cum = [_dot(tri[rev[i]], lw_hi[i]) + _dot(tri[rev[i]], lw_lo[i]) for i in nc]
    cum_prev = [cum[i] - lw[i] for i in nc]
    ref_row = [c // 2 if rev[i] else c // 2 - 1 for i in nc]
    tot_row = [0 if rev[i] else c - 1 for i in nc]
    c_ref = [cum[i][ref_row[i]:ref_row[i] + 1, :] for i in nc]
    c_tot = [cum[i][tot_row[i]:tot_row[i] + 1, :] for i in nc]
    e_inv = [jnp.exp(c_ref[i] - cum[i]) for i in nc]
    e_out = [jnp.exp(c_tot[i] - cum[i]) for i in nc]
    lhs = [cat(a[i] * jnp.exp(cum_prev[i] - c_ref[i]), r[i] * jnp.exp(cum[i] - c_ref[i])).astype(BF16) for i in nc]
    rhs = [cat(blockdiag(b[i] * e_inv[i]), blockdiag(kd[i] * e_inv[i])) for i in nc]
    nmat = [_dot_nt(lhs[i], rhs[i]) for i in nc]
    n_ab = [jnp.where(strict[rev[i]], nmat[i][:c, :QW], 0.0) for i in nc]
    n_ak = [jnp.where(strict[rev[i]], nmat[i][:c, QW:], 0.0).astype(BF16) for i in nc]
    n_rb = [jnp.where(incl[rev[i]], nmat[i][c:, :QW], 0.0).astype(BF16) for i in nc]
    n_rk = [jnp.where(incl[rev[i]], nmat[i][c:, QW:], 0.0).astype(BF16) for i in nc]

    pw = n_ab
    tinv = [eye + pw[i] for i in nc]
    for _ in range(int(np.log2(c)) - 1):
        pw = [_dot(pw[i].astype(BF16), blockdiag(pw[i])) for i in nc]
        tinv = [tinv[i] + _dot(tinv[i].astype(BF16), blockdiag(pw[i])) for i in nc]

    v_bd = [blockdiag(v[i]) for i in nc]
    state_lhs = [cat(a[i] * jnp.exp(cum_prev[i]), r[i] * jnp.exp(cum[i])).astype(BF16) for i in nc]
    from_state = [_dot_nt(state_lhs[i], ht[i].astype(BF16)) for i in nc]
    from_v = [_dot(n_ak[i], v_bd[i]) for i in nc]
    u = [_dot(tinv[i].astype(BF16), blockdiag(from_state[i][:c] + from_v[i])) for i in nc]
    u_bd = [blockdiag(u[i]) for i in nc]
    o = [from_state[i][c:] + _dot(cat(n_rb[i], n_rk[i], axis=1), cat(u_bd[i], v_bd[i])) for i in nc]
    u_b = [u[i].astype(BF16) for i in nc]
    v_b = [v[i].astype(BF16) for i in nc]
    upd = [_dot_tn(u_b[i], (b[i] * e_out[i]).astype(BF16)) + _dot_tn(v_b[i], (kd[i] * e_out[i]).astype(BF16))
           for i in nc]
    ht_new = [ht[i] * jnp.exp(c_tot[i]) + jnp.where(on_diag, upd[i], 0.0) for i in nc]
    bonus = [_head_sum(r[i] * kd[i] * rk[i], ones_bd) * v[i] for i in nc]
    return [(o[i] + bonus[i], ht_new[i]) for i in nc]


def _rwkv_scan_kernel(rf, vf, kkf, lwf, alf, kdf, rb, vb, kkb, lwb, alb, kdb, rk_ref,
                      yf_ref, yb_ref, h_ref):
    @pl.when(pl.program_id(2) == 0)
    def _():
        h_ref[...] = jnp.zeros_like(h_ref)

    n_chunks = rf.shape[0] // CHUNK
    n_groups = rf.shape[1] // QW

    def body(ci, carry):
        chains = []
        for d, (refs, y_ref) in enumerate((((rf, vf, kkf, lwf, alf, kdf), yf_ref),
                                           ((rb, vb, kkb, lwb, alb, kdb), yb_ref))):
            cidx = ci if d == 0 else n_chunks - 1 - ci
            rows = pl.ds(pl.multiple_of(cidx * CHUNK, CHUNK), CHUNK)
            for g in range(n_groups):
                lanes = slice(g * QW, (g + 1) * QW)
                vals = [x[rows, lanes] for x in refs] + [rk_ref[:, lanes], h_ref[d, g]]
                chains.append((d, g, y_ref, rows, lanes, vals))
        results = _scan_chunks([tuple(vals) + (d == 1,) for d, _, _, _, _, vals in chains])
        for (d, g, y_ref, rows, lanes, _), (y, ht) in zip(chains, results):
            y_ref[rows, lanes] = y
            h_ref[d, g] = ht
        return carry

    lax.fori_loop(0, n_chunks, body, 0)


def _rwkv_scan(r, v, kk, lw_f, lw_b, al_f, al_b, kd_f, kd_b, r_k, tb=256, groups=4):
    b, s, w = r.shape
    nblk = s // tb
    wb = groups * QW
    fwd = pl.BlockSpec((None, tb, wb), lambda b_, p, j: (b_, j, p))
    bwd = pl.BlockSpec((None, tb, wb), lambda b_, p, j: (b_, nblk - 1 - j, p))
    return pl.pallas_call(
        _rwkv_scan_kernel,
        grid=(b, w // wb, nblk),
        in_specs=[fwd] * 6 + [bwd] * 6 + [pl.BlockSpec((1, wb), lambda b_, p, j: (0, p))],
        out_specs=[fwd, bwd],
        out_shape=[jax.ShapeDtypeStruct((b, s, w), F32)] * 2,
        scratch_shapes=[pltpu.VMEM((2, groups, QW, QW), F32)],
        compiler_params=_cparams(("parallel", "parallel", "arbitrary")),
        name="rwkv_scan",
    )(r, v, kk, lw_f, al_f, kd_f, r, v, kk, lw_b, al_b, kd_b, r_k.reshape(1, -1))


def _rwkv_post_kernel(yf_ref, yb_ref, g_ref, lg_ref, lb_ref, o_ref):
    ones_bd = _head_ones()
    y = yf_ref[...] + yb_ref[...]
    mu = _head_sum(y, ones_bd) * (1.0 / RWKV_HEAD)
    d = y - mu
    var = _head_sum(d * d, ones_bd) * (1.0 / RWKV_HEAD)
    yn = d * lax.rsqrt(var + LNX_EPS) * lg_ref[...] + lb_ref[...]
    o_ref[...] = (yn * g_ref[...]).astype(o_ref.dtype)


def _rwkv_post(y_f, y_b, g, lnx_g, lnx_b, tm=512, tw=512):
    m, w = y_f.shape
    blk = pl.BlockSpec((tm, tw), lambda i, p: (i, p))
    vec = pl.BlockSpec((1, tw), lambda i, p: (0, p))
    return pl.pallas_call(
        _rwkv_post_kernel,
        grid=(m // tm, w // tw),
        in_specs=[blk, blk, blk, vec, vec],
        out_specs=blk,
        out_shape=jax.ShapeDtypeStruct((m, w), BF16),
        compiler_params=_cparams(("parallel", "parallel")),
        name="rwkv_post",
    )(y_f, y_b, g, lnx_g.reshape(1, -1), lnx_b.reshape(1, -1))


def _merge_kernel(a1_ref, w1_ref, a2_ref, w2_ref, g1_ref, g2_ref, o_ref):
    m1 = _dot(a1_ref[...], w1_ref[...])
    m2 = _dot(a2_ref[...], w2_ref[...])
    o_ref[...] = (_sigmoid(g1_ref[...]) * m1 + _sigmoid(g2_ref[...]) * m2).astype(o_ref.dtype)


def _merge(o_mla, w_up_mla, o_rwkv, w_up_rwkv, z_gate, tm=1024, tn=512):
    m, k = o_mla.shape
    n = w_up_mla.shape[1]
    nj = n // tn
    a_spec = pl.BlockSpec((tm, k), lambda i, j: (i, 0))
    w_spec = pl.BlockSpec((k, tn), lambda i, j: (0, j))
    return pl.pallas_call(
        _merge_kernel,
        grid=(m // tm, nj),
        in_specs=[a_spec, w_spec, a_spec, w_spec,
                  pl.BlockSpec((tm, tn), lambda i, j: (i, j)),
                  pl.BlockSpec((tm, tn), lambda i, j: (i, nj + j))],
        out_specs=pl.BlockSpec((tm, tn), lambda i, j: (i, j)),
        out_shape=jax.ShapeDtypeStruct((m, n), BF16),
        compiler_params=_cparams(("parallel", "arbitrary")),
        name="merge",
    )(o_mla, w_up_mla, o_rwkv, w_up_rwkv, z_gate, z_gate)


def _cross_router_kernel(h_ref, gc_ref, wq_ref, kv_ref, wo_ref, gf_ref, wr_ref, br_ref,
                         h2_ref, n3_ref, info_ref, cnt_ref, carry_ref):
    @pl.when(pl.program_id(0) == 0)
    def _():
        carry_ref[...] = jnp.zeros_like(carry_ref)

    h = h_ref[...]
    hn = (h * lax.rsqrt(jnp.mean(h * h, axis=-1, keepdims=True) + RMS_EPS) * gc_ref[...]).astype(BF16)
    q = (_dot(hn, wq_ref[...]) * (CROSS_HEAD_DIM ** -0.5)).astype(BF16)
    kv = kv_ref[...]
    outs = []
    for hd in range(CROSS_HEADS):
        lo = hd * CROSS_HEAD_DIM
        s = _dot_nt(q[:, lo:lo + CROSS_HEAD_DIM], kv[:, lo:lo + CROSS_HEAD_DIM])
        p = jnp.exp(s - jnp.max(s, axis=-1, keepdims=True))
        p = p / jnp.sum(p, axis=-1, keepdims=True)
        outs.append(_dot(p.astype(BF16), kv[:, CROSS_WIDTH + lo:CROSS_WIDTH + lo + CROSS_HEAD_DIM]))
    o = jnp.concatenate(outs, axis=-1).astype(BF16)
    h2 = h + _dot(o, wo_ref[...])
    h2_ref[...] = h2

    n3 = h2 * lax.rsqrt(jnp.mean(h2 * h2, axis=-1, keepdims=True) + RMS_EPS) * gf_ref[...]
    n3_ref[...] = n3

    logits = jnp.dot(n3, wr_ref[...], preferred_element_type=F32,
                     precision=lax.Precision.HIGHEST) + br_ref[...]
    lane = _iota(logits.shape, 1)
    lane_f = lane.astype(F32)
    neg = jnp.float32(-jnp.inf)
    big = jnp.float32(1e9)

    def masked_softmax(mask):
        x = jnp.where(mask, logits, neg)
        e = jnp.exp(x - jnp.max(x, axis=-1, keepdims=True))
        return e / jnp.sum(e, axis=-1, keepdims=True)

    def top1(prob, mask):
        pmax = jnp.max(jnp.where(mask, prob, -1.0), axis=-1, keepdims=True)
        idx = jnp.min(jnp.where(mask & (prob == pmax), lane_f, big), axis=-1, keepdims=True)
        return pmax, idx

    g_mask = (lane >= N_EXPERTS) & (lane < N_EXPERTS + N_GROUPS)
    p_group, g_idx = top1(masked_softmax(g_mask), g_mask)
    g_sel = g_idx - float(N_EXPERTS)
    e_mask = (lane < N_EXPERTS) & (lax.shift_right_logical(lane, 3).astype(F32) == g_sel)
    e_prob = masked_softmax(e_mask)
    p1, i1 = top1(e_prob, e_mask)
    rest = e_mask & (lane_f != i1)
    p2, i2 = top1(e_prob, rest)
    denom = p1 + p2
    w1 = p_group * (p1 / denom)
    w2 = p_group * (p2 / denom)

    tm = logits.shape[0]
    oh1 = jnp.where(lane_f == i1, 1.0, 0.0)
    oh2 = jnp.where(lane_f == i2, 1.0, 0.0)
    before = jnp.where(_iota((tm, tm), 1) < _iota((tm, tm), 0), 1.0, 0.0).astype(BF16)
    carry = carry_ref[...]
    cnt1 = jnp.sum(oh1, axis=0, keepdims=True)
    cnt2 = jnp.sum(oh2, axis=0, keepdims=True)
    rank1 = jnp.sum(oh1 * (carry + _dot(before, oh1.astype(BF16))), axis=-1, keepdims=True)
    rank2 = jnp.sum(oh2 * (carry + cnt1 + _dot(before, oh2.astype(BF16))), axis=-1, keepdims=True)
    carry = carry + cnt1 + cnt2
    carry_ref[...] = carry
    cnt_ref[...] = carry
    info = jnp.zeros_like(logits)
    for k, val in enumerate((i1, i2, w1, w2, rank1, rank2)):
        info = jnp.where(lane == k, val, info)
    info_ref[...] = info


def _cross_router(h1, seq_len, g_cross, wq, kvm, wo, g_ffn, w_r, b_r, tm=256):
    m, d = h1.shape
    full = lambda a: pl.BlockSpec(a.shape, lambda i: (0,) * a.ndim)
    row = lambda n: pl.BlockSpec((tm, n), lambda i: (i, 0))
    gc, gf = g_cross.reshape(1, d), g_ffn.reshape(1, d)
    per_seq = seq_len // tm
    return pl.pallas_call(
        _cross_router_kernel,
        grid=(m // tm,),
        in_specs=[row(d), full(gc), full(wq),
                  pl.BlockSpec((None,) + kvm.shape[1:], lambda i: (i // per_seq, 0, 0)),
                  full(wo), full(gf), full(w_r), full(b_r)],
        out_specs=[row(d), row(d), row(LANES), pl.BlockSpec((1, LANES), lambda i: (0, 0))],
        out_shape=[jax.ShapeDtypeStruct((m, d), F32), jax.ShapeDtypeStruct((m, d), F32),
                   jax.ShapeDtypeStruct((m, LANES), F32), jax.ShapeDtypeStruct((1, LANES), F32)],
        scratch_shapes=[pltpu.VMEM((1, LANES), F32)],
        compiler_params=_cparams(("arbitrary",)),
        name="cross_router",
    )(h1, gc, wq, kvm, wo, gf, w_r, b_r)


def _gather_rows(src_hbm, idx_ref, base, dst, sem, n_rows):
    def issue(g, carry):
        for u in range(GATHER_UNROLL):
            r = g * GATHER_UNROLL + u
            src_row = idx_ref[base + r]
            pltpu.make_async_copy(src_hbm.at[pl.ds(src_row, 1), :], dst.at[pl.ds(r, 1), :], sem).start()
        return carry
    lax.fori_loop(0, n_rows // GATHER_UNROLL, issue, 0)


def _wait_rows(src_hbm, dst, sem, n_rows):
    pltpu.make_async_copy(src_hbm.at[pl.ds(0, n_rows), :], dst, sem).wait()


def _moe_expert_kernel(tile_expert, row_token, n_valid, x_hbm, rw_ref, wg_ref, wu_ref, wd_ref,
                       y_ref, xbuf, wg_b, wu_b, wd_b, sem):
    t = pl.program_id(0)
    nt = pl.num_programs(0)
    slot = lax.rem(t, 2)

    @pl.when(t == 0)
    def _():
        _gather_rows(x_hbm, row_token, 0, xbuf.at[0], sem.at[0], MOE_TILE)

    @pl.when(t + 1 < nt)
    def _():
        _gather_rows(x_hbm, row_token, (t + 1) * MOE_TILE, xbuf.at[1 - slot], sem.at[1 - slot], MOE_TILE)

    changed = jnp.logical_or(t == 0, tile_expert[t] != tile_expert[jnp.maximum(t - 1, 0)])

    @pl.when(changed)
    def _():
        wg_b[...] = wg_ref[...].astype(BF16)
        wu_b[...] = wu_ref[...].astype(BF16)
        wd_b[...] = wd_ref[...].astype(BF16)

    _wait_rows(x_hbm, xbuf.at[slot], sem.at[slot], MOE_TILE)

    @pl.when(t < n_valid[0])
    def _():
        x = xbuf[slot].astype(BF16)
        hg = _dot(x, wg_b[...])
        hu = _dot(x, wu_b[...])
        hid = (hg * _sigmoid(hg) * hu * rw_ref[...]).astype(BF16)
        y_ref[...] = _dot(hid, wd_b[...])

    @pl.when(t >= n_valid[0])
    def _():
        y_ref[...] = jnp.zeros_like(y_ref)


def _moe_experts(n3, row_w, tile_expert, row_token, n_valid, w_eg, w_eu, w_ed):
    m, d = n3.shape
    p_max = row_token.shape[0]
    ne, _, f = w_eg.shape
    grid_spec = pltpu.PrefetchScalarGridSpec(
        num_scalar_prefetch=3,
        grid=(p_max // MOE_TILE,),
        in_specs=[pl.BlockSpec(memory_space=pl.ANY),
                  pl.BlockSpec((MOE_TILE, 1), lambda t, te, rt, nv: (t, 0)),
                  pl.BlockSpec((None, d, f), lambda t, te, rt, nv: (te[t], 0, 0)),
                  pl.BlockSpec((None, d, f), lambda t, te, rt, nv: (te[t], 0, 0)),
                  pl.BlockSpec((None, f, d), lambda t, te, rt, nv: (te[t], 0, 0))],
        out_specs=pl.BlockSpec((MOE_TILE, d), lambda t, te, rt, nv: (t, 0)),
        scratch_shapes=[pltpu.VMEM((2, MOE_TILE, d), F32),
                        pltpu.VMEM((d, f), BF16), pltpu.VMEM((d, f), BF16), pltpu.VMEM((f, d), BF16),
                        pltpu.SemaphoreType.DMA((2,))],
    )
    return pl.pallas_call(
        _moe_expert_kernel,
        grid_spec=grid_spec,
        out_shape=jax.ShapeDtypeStruct((p_max, d), F32),
        compiler_params=_cparams(("arbitrary",)),
        name="moe_experts",
    )(tile_expert, row_token, n_valid, n3, row_w, w_eg, w_eu, w_ed)


def _moe_combine_kernel(pos1, pos2, y_hbm, h_ref, g_ref, o_ref, buf1, buf2, sem):
    i = pl.program_id(0)
    ni = pl.num_programs(0)
    tm = h_ref.shape[0]
    slot = lax.rem(i, 2)

    def start(step, s):
        _gather_rows(y_hbm, pos1, step * tm, buf1.at[s], sem.at[0, s], tm)
        _gather_rows(y_hbm, pos2, step * tm, buf2.at[s], sem.at[1, s], tm)

    @pl.when(i == 0)
    def _():
        start(0, 0)

    @pl.when(i + 1 < ni)
    def _():
        start(i + 1, 1 - slot)

    _wait_rows(y_hbm, buf1.at[slot], sem.at[0, slot], tm)
    _wait_rows(y_hbm, buf2.at[slot], sem.at[1, slot], tm)
    y = h_ref[...] + buf1[slot] + buf2[slot]
    o_ref[...] = y * lax.rsqrt(jnp.mean(y * y, axis=-1, keepdims=True) + RMS_EPS) * g_ref[...]


def _moe_combine(ys, pos1, pos2, h2, g_final, tm=256):
    m, d = h2.shape
    grid_spec = pltpu.PrefetchScalarGridSpec(
        num_scalar_prefetch=2,
        grid=(m // tm,),
        in_specs=[pl.BlockSpec(memory_space=pl.ANY),
                  pl.BlockSpec((tm, d), lambda i, p1, p2: (i, 0)),
                  pl.BlockSpec((1, d), lambda i, p1, p2: (0, 0))],
        out_specs=pl.BlockSpec((tm, d), lambda i, p1, p2: (i, 0)),
        scratch_shapes=[pltpu.VMEM((2, tm, d), F32), pltpu.VMEM((2, tm, d), F32),
                        pltpu.SemaphoreType.DMA((2, 2))],
    )
    return pl.pallas_call(
        _moe_combine_kernel,
        grid_spec=grid_spec,
        out_shape=jax.ShapeDtypeStruct((m, d), F32),
        compiler_params=_cparams(("arbitrary",)),
        name="moe_combine",
    )(pos1, pos2, ys, h2, g_final.reshape(1, d))


def _moe_plan(info, counts):
    m = info.shape[0]
    p_max = 2 * m + N_EXPERTS * MOE_TILE
    n_tiles = p_max // MOE_TILE
    e1 = info[:, 0].astype(jnp.int32)
    e2 = info[:, 1].astype(jnp.int32)
    w1, w2 = info[:, 2], info[:, 3]
    rank1 = info[:, 4].astype(jnp.int32)
    rank2 = info[:, 5].astype(jnp.int32)
    cnt = counts[0, :N_EXPERTS].astype(jnp.int32)
    tiles_e = (cnt + MOE_TILE - 1) // MOE_TILE
    tile_end = jnp.cumsum(tiles_e)
    off_pad = (tile_end - tiles_e) * MOE_TILE
    pos1 = off_pad[e1] + rank1
    pos2 = off_pad[e2] + rank2
    n_valid = tile_end[-1:]
    t_idx = jnp.arange(n_tiles, dtype=jnp.int32)
    t_clip = jnp.minimum(t_idx, n_valid[0] - 1)
    tile_expert = jnp.sum((tile_end[None, :] <= t_clip[:, None]).astype(jnp.int32), axis=1)
    tok = jnp.arange(m, dtype=jnp.int32)
    row_token = jnp.zeros((p_max,), jnp.int32).at[pos1].set(tok).at[pos2].set(tok)
    row_w = jnp.zeros((p_max,), F32).at[pos1].set(w1).at[pos2].set(w2)
    return pos1, pos2, n_valid, tile_expert, row_token, row_w.reshape(p_max, 1)


def _pad_cols(w, n):
    return jnp.pad(w, ((0, 0), (0, n - w.shape[1])))


def _pad_rows(w, n):
    return jnp.pad(w, ((0, n - w.shape[0]), (0, 0)))


def _split_lora(w):
    o = 0
    parts = []
    for width in (DECAY_LORA, DECAY_LORA, ICLR_LORA, ICLR_LORA):
        parts.append(_pad_cols(w[:, o:o + width], LORA_PAD))
        o += width
    parts.append(w[:, o:o + GATE_LORA])
    return jnp.concatenate(parts, axis=1)


def kernel(x, mem, positions, g_mix, w_in, g_q, w_uq, g_kv, w_ukv, shift_conv, w0_f, w2_f, w0_b, w2_b, a0_f, a2_f, a0_b, a2_b, g2, k_k, k_a, r_k, lnx_g, lnx_b, w_up_mla, w_up_rwkv, w_out, g_cross, g_mem, wq_c, wkv_c, wo_c, g_ffn, w_rg, b_rg, w_re, b_re, w_eg, w_eu, w_ed, g_final):
    bsz, seq, d = x.shape
    m = bsz * seq
    depth = w_in.shape[0]
    h = x.reshape(m, d)
    pos = positions.reshape(m, 1)
    lane = np.arange(LANES)
    invf = jnp.asarray(np.where(lane < ROPE_DIM, 1.0, 0.0), F32) * (
        ROPE_BASE ** (-jnp.asarray(lane % (ROPE_DIM // 2), F32) * (2.0 / ROPE_DIM)))
    invf = invf.reshape(1, LANES)
    assert depth == 1, "the MoE kernel applies the final norm, so it must be the last layer"
    for l in range(depth):
        wi = w_in[l]
        w_mla = _pad_cols(wi[:, :MLA_IN], MLA_IN + (LANES - ROPE_DIM)).astype(BF16)
        rw = wi[:, MLA_IN:MLA_IN + RWKV_IN]
        w_rkv = rw[:, :3 * RWKV_WIDTH].astype(BF16)
        w_lora = _split_lora(rw[:, 3 * RWKV_WIDTH:]).astype(BF16)
        w_gate = wi[:, MLA_IN + RWKV_IN:].astype(BF16)
        sc = shift_conv[l]
        sc_rkv = sc[:, :3 * RWKV_WIDTH]
        sc_lora = _split_lora(sc[:, 3 * RWKV_WIDTH:])
        wq = w_uq[l].reshape(Q_LORA, MLA_HEADS, QK_DIM)
        wq = jnp.pad(wq, ((0, 0), (0, 0), (0, QK_PAD - QK_DIM))).reshape(Q_LORA, MLA_HEADS * QK_PAD)
        wkv = w_ukv[l].reshape(KV_LORA, MLA_HEADS, NOPE_DIM + V_DIM)
        wkv = jnp.concatenate([wkv[:, :, :NOPE_DIM].reshape(KV_LORA, -1),
                               wkv[:, :, NOPE_DIM:].reshape(KV_LORA, -1)], axis=1)
        lora_rows = lambda w: _pad_rows(w, LORA_PAD).astype(BF16)
        w_router = _pad_cols(jnp.concatenate(
            [jnp.moveaxis(w_re[l], 0, 1).reshape(d, N_EXPERTS), w_rg[l]], axis=1), LANES)
        b_router = _pad_cols(jnp.concatenate([b_re[l].reshape(1, N_EXPERTS), b_rg[l].reshape(1, N_GROUPS)],
                                             axis=1), LANES)

        n1 = _rmsnorm(h, g_mix[l])
        z_mla = _matmul(n1, w_mla, F32, 1024, w_mla.shape[1], name="in_proj_mla")
        z_rkv = _matmul(n1, w_rkv, F32, 1024, 512, name="in_proj_rkv")
        z_lora = _matmul(n1, w_lora, F32, 1024, w_lora.shape[1], name="in_proj_lora")
        z_gate = _matmul(n1, w_gate, F32, 1024, 512, name="in_proj_gate")

        q_cat, k_cat, v_mla = _mla_proj(z_mla, pos, invf, g_q[l], g_kv[l], wq.astype(BF16), wkv.astype(BF16))
        o_mla = _mla_attn(q_cat.reshape(bsz, seq, -1), k_cat.reshape(bsz, seq, -1),
                          v_mla.reshape(bsz, seq, -1)).reshape(m, -1)

        (r, v, kk, lw_f, lw_b, al_f, al_b, kd_f, kd_b, gate) = _rwkv_prep(
            z_rkv, z_lora, seq, sc_rkv, sc_lora, lora_rows(w2_f[l]), lora_rows(w2_b[l]),
            lora_rows(a2_f[l]), lora_rows(a2_b[l]), g2[l].astype(BF16),
            w0_f[l], w0_b[l], a0_f[l], a0_b[l], k_k[l], k_a[l])
        sh = lambda t: t.reshape(bsz, seq, RWKV_WIDTH)
        y_f, y_b = _rwkv_scan(sh(r), sh(v), sh(kk), sh(lw_f), sh(lw_b), sh(al_f), sh(al_b),
                              sh(kd_f), sh(kd_b), r_k[l])
        o_rwkv = _rwkv_post(y_f.reshape(m, -1), y_b.reshape(m, -1), gate, lnx_g[l], lnx_b[l])

        merged = _merge(o_mla, w_up_mla[l].astype(BF16), o_rwkv, w_up_rwkv[l].astype(BF16), z_gate)
        h1 = _matmul(merged, w_out[l].astype(BF16), F32, 1024, 512, res=h, name="out_proj")

        mem_n = _rmsnorm(mem.reshape(bsz * MEM_LEN, d), g_mem[l], tm=MEM_LEN)
        kvm = _matmul(mem_n, wkv_c[l].astype(BF16), BF16, bsz * MEM_LEN, 512, name="mem_kv")
        h2, n3, info, counts = _cross_router(h1, seq, g_cross[l], wq_c[l].astype(BF16),
                                             kvm.reshape(bsz, MEM_LEN, 2 * CROSS_WIDTH),
                                             wo_c[l].astype(BF16), g_ffn[l], w_router, b_router)

        pos1, pos2, n_valid, tile_expert, row_token, row_w = _moe_plan(info, counts)
        flat = lambda w: w.reshape((N_EXPERTS,) + w.shape[2:])
        ys = _moe_experts(n3, row_w, tile_expert, row_token, n_valid,
                          flat(w_eg[l]), flat(w_eu[l]), flat(w_ed[l]))
        out = _moe_combine(ys, pos1, pos2, h2, g_final)
    return out.reshape(bsz, seq, d)
```

```python
import functools

import jax
import jax.numpy as jnp
import numpy as np
from jax import lax
from jax.experimental import pallas as pl
from jax.experimental.pallas import tpu as pltpu

F32 = jnp.float32
BF16 = jnp.bfloat16

D_MODEL = 2048
MEM_LEN = 256
RMS_EPS = 1e-6
MLA_HEADS = 16
Q_LORA = 512
KV_LORA = 256
NOPE_DIM = 128
ROPE_DIM = 64
V_DIM = 128
QK_DIM = NOPE_DIM + ROPE_DIM
ROPE_BASE = 10000.0
RWKV_HEAD = 64
RWKV_HEADS = D_MODEL // RWKV_HEAD
RWKV_WIDTH = D_MODEL
DECAY_LORA = 96
ICLR_LORA = 96
GATE_LORA = 256
LNX_EPS = 64e-5
CROSS_HEADS = 4
CROSS_HEAD_DIM = 128
CROSS_WIDTH = CROSS_HEADS * CROSS_HEAD_DIM
N_GROUPS = 4
EXPERTS_PER_GROUP = 8
N_EXPERTS = N_GROUPS * EXPERTS_PER_GROUP
D_EXPERT = 512
MLA_IN = Q_LORA + KV_LORA + ROPE_DIM
RWKV_IN = 3 * RWKV_WIDTH + 2 * DECAY_LORA + 2 * ICLR_LORA + GATE_LORA

LANES = 128
SUBLANES = 8
QK_PAD = 256
LORA_PAD = LANES
QUAD = 4
QW = QUAD * RWKV_HEAD
CHUNK = 64
ATTN_TK = 1024
MOE_TILE = 256
GATHER_UNROLL = 8
LOG2E = 1.4426950408889634
VMEM_LIMIT = 56 * 1024 * 1024


def _cparams(sem, vmem=VMEM_LIMIT, flags=None):
    return pltpu.CompilerParams(dimension_semantics=sem, vmem_limit_bytes=vmem, flags=flags)


def _iota(shape, dim):
    return lax.broadcasted_iota(jnp.int32, shape, dim)


def _sigmoid(x):
    return 1.0 / (1.0 + jnp.exp(-x))


def _dot(a, b):
    return jnp.dot(a, b, preferred_element_type=F32)


def _dot_nt(a, b):
    return lax.dot_general(a, b, (((1,), (1,)), ((), ())), preferred_element_type=F32)


def _dot_tn(a, b):
    return lax.dot_general(a, b, (((0,), (0,)), ((), ())), preferred_element_type=F32)


def _pack_bf16_pair(lo, hi):
    lo_bits = pltpu.bitcast(lo.astype(BF16).astype(F32), jnp.uint32)
    hi_bits = pltpu.bitcast(hi.astype(BF16).astype(F32), jnp.uint32)
    return jnp.bitwise_or(jnp.bitwise_and(hi_bits, jnp.uint32(0xFFFF0000)),
                          lax.shift_right_logical(lo_bits, jnp.uint32(16)))


def _unpack_bf16_pair(p):
    lo = pltpu.bitcast(lax.shift_left(p, jnp.uint32(16)), F32).astype(BF16)
    hi = pltpu.bitcast(jnp.bitwise_and(p, jnp.uint32(0xFFFF0000)), F32).astype(BF16)
    return lo, hi


def _head_ones():
    r = lax.shift_right_logical(_iota((LANES, LANES), 0), 6)
    c = lax.shift_right_logical(_iota((LANES, LANES), 1), 6)
    return jnp.where(r == c, 1.0, 0.0).astype(BF16)


def _head_sum(x, ones_bd):
    w = ones_bd.shape[0]
    hi = x.astype(BF16)
    lo = (x - hi.astype(F32)).astype(BF16)
    parts = [_dot(hi[:, j:j + w], ones_bd) + _dot(lo[:, j:j + w], ones_bd) for j in range(0, x.shape[1], w)]
    return parts[0] if len(parts) == 1 else jnp.concatenate(parts, axis=1)


def _rmsnorm_kernel(x_ref, g_ref, o_ref):
    x = x_ref[...]
    y = x * lax.rsqrt(jnp.mean(x * x, axis=-1, keepdims=True) + RMS_EPS)
    o_ref[...] = (y * g_ref[...]).astype(o_ref.dtype)


def _rmsnorm(x, g, tm=512):
    m, d = x.shape
    return pl.pallas_call(
        _rmsnorm_kernel,
        grid=(m // tm,),
        in_specs=[pl.BlockSpec((tm, d), lambda i: (i, 0)),
                  pl.BlockSpec((1, d), lambda i: (0, 0))],
        out_specs=pl.BlockSpec((tm, d), lambda i: (i, 0)),
        out_shape=jax.ShapeDtypeStruct((m, d), BF16),
        compiler_params=_cparams(("parallel",)),
        name="rmsnorm",
    )(x, g.reshape(1, d))


def _mm_kernel(a_ref, b_ref, o_ref):
    o_ref[...] = _dot(a_ref[...], b_ref[...]).astype(o_ref.dtype)


def _mm_res_kernel(a_ref, b_ref, r_ref, o_ref):
    o_ref[...] = (r_ref[...] + _dot(a_ref[...], b_ref[...])).astype(o_ref.dtype)


def _matmul(a, b, out_dtype, tm, tn, res=None, name="matmul"):
    m, k = a.shape
    n = b.shape[1]
    in_specs = [pl.BlockSpec((tm, k), lambda i, j: (i, 0)),
                pl.BlockSpec((k, tn), lambda i, j: (0, j))]
    args = [a, b]
    kern = _mm_kernel
    if res is not None:
        in_specs.append(pl.BlockSpec((tm, tn), lambda i, j: (i, j)))
        args.append(res)
        kern = _mm_res_kernel
    return pl.pallas_call(
        kern,
        grid=(m // tm, n // tn),
        in_specs=in_specs,
        out_specs=pl.BlockSpec((tm, tn), lambda i, j: (i, j)),
        out_shape=jax.ShapeDtypeStruct((m, n), out_dtype),
        compiler_params=_cparams(("parallel", "arbitrary")),
        name=name,
    )(*args)


def _mla_proj_kernel(z_ref, pos_ref, invf_ref, gq_ref, gkv_ref, wq_ref, wkv_ref,
                     q_ref, k_ref, v_ref):
    tm = z_ref.shape[0]
    z = z_ref[...]

    def norm(c, g):
        return (c * lax.rsqrt(jnp.mean(c * c, axis=-1, keepdims=True) + RMS_EPS) * g).astype(BF16)

    cq = norm(z[:, :Q_LORA], gq_ref[...])
    ckv = norm(z[:, Q_LORA:Q_LORA + KV_LORA], gkv_ref[...])
    q = _dot(cq, wq_ref[...]) * (QK_DIM ** -0.5 * LOG2E)
    kv = _dot(ckv, wkv_ref[...])

    ang = pos_ref[...].astype(F32) * invf_ref[...]
    lane = _iota((tm, LANES), 1)
    half = ROPE_DIM // 2
    cos, sin = jnp.cos(ang), jnp.sin(ang)
    c_tab = jnp.where(lane < ROPE_DIM, cos, 0.0)
    s_up = jnp.where((lane >= half) & (lane < ROPE_DIM), sin, 0.0)
    s_dn = jnp.where(lane < half, -sin, 0.0)

    def rope(x):
        return (x * c_tab + pltpu.roll(x, half, 1) * s_up
                + pltpu.roll(x, LANES - half, 1) * s_dn)

    k_r = rope(z[:, Q_LORA + KV_LORA:]).astype(BF16)
    for h in range(MLA_HEADS):
        lo = h * QK_PAD
        q_ref[:, lo:lo + NOPE_DIM] = q[:, lo:lo + NOPE_DIM].astype(BF16)
        q_ref[:, lo + NOPE_DIM:lo + QK_PAD] = rope(q[:, lo + NOPE_DIM:lo + QK_PAD]).astype(BF16)
        k_ref[:, lo:lo + NOPE_DIM] = kv[:, h * NOPE_DIM:(h + 1) * NOPE_DIM].astype(BF16)
        k_ref[:, lo + NOPE_DIM:lo + QK_PAD] = k_r
    v_ref[...] = kv[:, MLA_HEADS * NOPE_DIM:].astype(BF16)


def _mla_proj(z_mla, pos, invf, g_q, g_kv, wq, wkv, tm=256):
    m, w = z_mla.shape
    full = lambda shape: pl.BlockSpec(shape, lambda i: (0, 0))
    row = lambda n: pl.BlockSpec((tm, n), lambda i: (i, 0))
    return pl.pallas_call(
        _mla_proj_kernel,
        grid=(m // tm,),
        in_specs=[row(w), row(1), full((1, LANES)), full((1, Q_LORA)), full((1, KV_LORA)),
                  full(wq.shape), full(wkv.shape)],
        out_specs=[row(MLA_HEADS * QK_PAD), row(MLA_HEADS * QK_PAD), row(MLA_HEADS * V_DIM)],
        out_shape=[jax.ShapeDtypeStruct((m, MLA_HEADS * QK_PAD), BF16),
                   jax.ShapeDtypeStruct((m, MLA_HEADS * QK_PAD), BF16),
                   jax.ShapeDtypeStruct((m, MLA_HEADS * V_DIM), BF16)],
        compiler_params=_cparams(("parallel",)),
        name="mla_proj",
    )(z_mla, pos, invf, g_q.reshape(1, -1), g_kv.reshape(1, -1), wq, wkv)


def _mla_attn_kernel(q_ref, k_ref, v_ref, o_ref):
    tq = q_ref.shape[0]
    nk = k_ref.shape[0] // ATTN_TK
    q = q_ref[...]

    def scores(j):
        return _dot_nt(q, k_ref[j * ATTN_TK:(j + 1) * ATTN_TK, :])

    m = jnp.full((tq, 1), -jnp.inf, F32)
    l = jnp.zeros((tq, 1), F32)
    acc = jnp.zeros((tq, V_DIM), F32)
    s_next = scores(0)
    for j in range(nk):
        s = s_next
        if j + 1 < nk:
            s_next = scores(j + 1)
        m_new = jnp.maximum(m, jnp.max(s, axis=-1, keepdims=True))
        alpha = jnp.exp2(m - m_new)
        p = jnp.exp2(s - m_new)
        l = alpha * l + jnp.sum(p, axis=-1, keepdims=True)
        acc = alpha * acc + _dot(p.astype(BF16), v_ref[j * ATTN_TK:(j + 1) * ATTN_TK, :])
        m = m_new
    o_ref[...] = (acc / l).astype(o_ref.dtype)


def _mla_attn(q, k, v, tq=512):
    b, s, _ = q.shape
    return pl.pallas_call(
        _mla_attn_kernel,
        grid=(b, MLA_HEADS, s // tq),
        in_specs=[pl.BlockSpec((None, tq, QK_PAD), lambda b_, h, i: (b_, i, h)),
                  pl.BlockSpec((None, s, QK_PAD), lambda b_, h, i: (b_, 0, h)),
                  pl.BlockSpec((None, s, V_DIM), lambda b_, h, i: (b_, 0, h))],
        out_specs=pl.BlockSpec((None, tq, V_DIM), lambda b_, h, i: (b_, i, h)),
        out_shape=jax.ShapeDtypeStruct((b, s, MLA_HEADS * V_DIM), BF16),
        compiler_params=_cparams(("parallel", "parallel", "arbitrary")),
        name="mla_attn",
    )(q, k, v)


def _rwkv_prep_kernel(seq_len,
                      r_ref, rp_ref, rn_ref, k_ref, kp_ref, kn_ref, v_ref, vp_ref, vn_ref,
                      l_ref, lp_ref, ln_ref, scr_ref, sck_ref, scv_ref, scl_ref,
                      w2f_ref, w2b_ref, a2f_ref, a2b_ref, g2_ref,
                      w0f_ref, w0b_ref, a0f_ref, a0b_ref, kk_w_ref, ka_ref,
                      ro_ref, vo_ref, kko_ref, lwf_ref, lwb_ref, alf_ref, alb_ref,
                      kdf_ref, kdb_ref, g_ref):
    tm = r_ref.shape[0]
    i = pl.program_id(0)
    first = lax.rem(i * tm, seq_len) == 0
    last = lax.rem((i + 1) * tm, seq_len) == 0

    def shift(z_ref, zp_ref, zn_ref, w_ref):
        z = z_ref[...]
        rows = _iota(z.shape, 0)
        prev_row = jnp.where(first, 0.0, zp_ref[SUBLANES - 1:SUBLANES, :])
        next_row = jnp.where(last, 0.0, zn_ref[0:1, :])
        z_prev = jnp.where(rows == 0, prev_row, pltpu.roll(z, 1, 0))
        z_next = jnp.where(rows == tm - 1, next_row, pltpu.roll(z, tm - 1, 0))
        return w_ref[0:1, :] * z_prev + w_ref[1:2, :] * z + w_ref[2:3, :] * z_next

    r = shift(r_ref, rp_ref, rn_ref, scr_ref)
    k = shift(k_ref, kp_ref, kn_ref, sck_ref)
    v = shift(v_ref, vp_ref, vn_ref, scv_ref)
    lo = shift(l_ref, lp_ref, ln_ref, scl_ref)
    xw_f = lo[:, 0 * LORA_PAD:1 * LORA_PAD]
    xw_b = lo[:, 1 * LORA_PAD:2 * LORA_PAD]
    xa_f = lo[:, 2 * LORA_PAD:3 * LORA_PAD]
    xa_b = lo[:, 3 * LORA_PAD:4 * LORA_PAD]
    xg = lo[:, 4 * LORA_PAD:]

    def log_decay(xw, w0_ref, w2_ref):
        y = -(w0_ref[...] + _dot(jnp.tanh(xw).astype(BF16), w2_ref[...]))
        softplus = jnp.maximum(y, 0.0) + jnp.log(1.0 + jnp.exp(-jnp.abs(y)))
        return -jnp.exp(-softplus - 0.5)

    def rate(xa, a0_ref, a2_ref):
        return _sigmoid(a0_ref[...] + _dot(xa.astype(BF16), a2_ref[...]))

    al_f = rate(xa_f, a0f_ref, a2f_ref)
    al_b = rate(xa_b, a0b_ref, a2b_ref)
    kk = k * kk_w_ref[...]
    kk = kk * lax.rsqrt(_head_sum(kk * kk, _head_ones()) + 1e-12)
    ka = ka_ref[...]

    ro_ref[...] = r
    vo_ref[...] = v
    kko_ref[...] = kk
    lwf_ref[...] = log_decay(xw_f, w0f_ref, w2f_ref)
    lwb_ref[...] = log_decay(xw_b, w0b_ref, w2b_ref)
    alf_ref[...] = al_f
    alb_ref[...] = al_b
    kdf_ref[...] = k * (1.0 + (al_f - 1.0) * ka)
    kdb_ref[...] = k * (1.0 + (al_b - 1.0) * ka)
    g_ref[...] = _dot(_sigmoid(xg).astype(BF16), g2_ref[...])


def _rwkv_prep(z_rkv, z_lora, seq_len, sc_rkv, sc_lora, w2f, w2b, a2f, a2b, g2,
               w0f, w0b, a0f, a0b, k_k, k_a, tm=512, tw=512):
    m = z_rkv.shape[0]
    wl = z_lora.shape[1]
    nb = RWKV_WIDTH // tw
    rb = tm // SUBLANES
    nrb = m // SUBLANES

    def main(seg):
        return pl.BlockSpec((tm, tw), lambda i, p: (i, seg * nb + p))

    def prev(seg):
        return pl.BlockSpec((SUBLANES, tw),
                            lambda i, p: (jnp.maximum(i * rb - 1, 0), seg * nb + p))

    def nxt(seg):
        return pl.BlockSpec((SUBLANES, tw),
                            lambda i, p: (jnp.minimum((i + 1) * rb, nrb - 1), seg * nb + p))

    def sc(seg):
        return pl.BlockSpec((3, tw), lambda i, p: (0, seg * nb + p))

    colblk = lambda rows: pl.BlockSpec((rows, tw), lambda i, p: (0, p))
    in_specs = []
    for seg in range(3):
        in_specs += [main(seg), prev(seg), nxt(seg)]
    in_specs += [pl.BlockSpec((tm, wl), lambda i, p: (i, 0)),
                 pl.BlockSpec((SUBLANES, wl), lambda i, p: (jnp.maximum(i * rb - 1, 0), 0)),
                 pl.BlockSpec((SUBLANES, wl), lambda i, p: (jnp.minimum((i + 1) * rb, nrb - 1), 0))]
    in_specs += [sc(0), sc(1), sc(2), pl.BlockSpec((3, wl), lambda i, p: (0, 0))]
    in_specs += [colblk(LORA_PAD)] * 4 + [colblk(GATE_LORA)] + [colblk(1)] * 6
    out_spec = pl.BlockSpec((tm, tw), lambda i, p: (i, p))
    n_out = 10
    row = lambda a: a.reshape(1, -1)
    return pl.pallas_call(
        functools.partial(_rwkv_prep_kernel, seq_len),
        grid=(m // tm, nb),
        in_specs=in_specs,
        out_specs=[out_spec] * n_out,
        out_shape=[jax.ShapeDtypeStruct((m, RWKV_WIDTH), F32)] * n_out,
        compiler_params=_cparams(("parallel", "arbitrary")),
        name="rwkv_prep",
    )(z_rkv, z_rkv, z_rkv, z_rkv, z_rkv, z_rkv, z_rkv, z_rkv, z_rkv,
      z_lora, z_lora, z_lora, sc_rkv, sc_rkv, sc_rkv, sc_lora,
      w2f, w2b, a2f, a2b, g2, row(w0f), row(w0b), row(a0f), row(a0b), row(k_k), row(k_a))


def _scan_chunks(chains):
    c = CHUNK
    nc = range(len(chains))
    r, v, kk, lw, al, kd, rk, ht, rev = [list(t) for t in zip(*chains)]
    t_i = _iota((c, c), 0)
    s_i = _iota((c, c), 1)
    tri = {False: jnp.where(s_i <= t_i, 1.0, 0.0).astype(BF16),
           True: jnp.where(s_i >= t_i, 1.0, 0.0).astype(BF16)}
    row = _iota((c, QW), 0)
    col = jnp.bitwise_and(_iota((c, QW), 1), RWKV_HEAD - 1)
    strict = {False: col < row, True: col > row}
    incl = {False: col <= row, True: col >= row}
    eye = jnp.where(row == col, 1.0, 0.0)
    lane_head = lax.shift_right_logical(_iota((c, QW), 1), 6)
    head_mask = [lane_head == h for h in range(QUAD)]
    on_diag = (lax.shift_right_logical(_iota((QW, QW), 0), 6)
               == lax.shift_right_logical(_iota((QW, QW), 1), 6))
    ones_bd = jnp.where(on_diag, 1.0, 0.0).astype(BF16)

    def blockdiag(x):
        xb = x.astype(BF16)
        zero = jnp.zeros_like(xb)
        return jnp.concatenate([jnp.where(head_mask[h], xb, zero) for h in range(QUAD)], axis=0)

    def cat(x, y, axis=0):
        return jnp.concatenate([x, y], axis=axis)

    a = [-kk[i] for i in nc]
    b = [kk[i] * al[i] for i in nc]
    lw_hi = [lw[i].astype(BF16) for i in nc]
    lw_lo = [(lw[i] - lw_hi[i].astype(F32)).astype(BF16) for i in nc]
    cum = [_dot(tri[rev[i]], lw_hi[i]) + _dot(tri[rev[i]], lw_lo[i]) for i in nc]
    cum_prev = [cum[i] - lw[i] for i in nc]
    ref_row = [c // 2 if rev[i] else c // 2 - 1 for i in nc]
    tot_row = [0 if rev[i] else c - 1 for i in nc]
    c_ref = [cum[i][ref_row[i]:ref_row[i] + 1, :] for i in nc]
    c_tot = [cum[i][tot_row[i]:tot_row[i] + 1, :] for i in nc]
    e_inv = [jnp.exp(c_ref[i] - cum[i]) for i in nc]
    e_out = [jnp.exp(c_tot[i] - cum[i]) for i in nc]
    lhs = [cat(a[i] * jnp.exp(cum_prev[i] - c_ref[i]), r[i] * jnp.exp(cum[i] - c_ref[i])).astype(BF16) for i in nc]
    rhs = [cat(blockdiag(b[i] * e_inv[i]), blockdiag(kd[i] * e_inv[i])) for i in nc]
    nmat = [_dot_nt(lhs[i], rhs[i]) for i in nc]
    n_ab = [jnp.where(strict[rev[i]], nmat[i][:c, :QW], 0.0) for i in nc]
    n_ak = [jnp.where(strict[rev[i]], nmat[i][:c, QW:], 0.0).astype(BF16) for i in nc]
    n_rb = [jnp.where(incl[rev[i]], nmat[i][c:, :QW], 0.0).astype(BF16) for i in nc]
    n_rk = [jnp.where(incl[rev[i]], nmat[i][c:, QW:], 0.0).astype(BF16) for i in nc]

    pw = n_ab
    tinv = [eye + pw[i] for i in nc]
    for _ in range(int(np.log2(c)) - 1):
        pw = [_dot(pw[i].astype(BF16), blockdiag(pw[i])) for i in nc]
        tinv = [tinv[i] + _dot(tinv[i].astype(BF16), blockdiag(pw[i])) for i in nc]

    v_bd = [blockdiag(v[i]) for i in nc]
    state_lhs = [cat(a[i] * jnp.exp(cum_prev[i]), r[i] * jnp.exp(cum[i])).astype(BF16) for i in nc]
    from_state = [_dot_nt(state_lhs[i], ht[i].astype(BF16)) for i in nc]
    from_v = [_dot(n_ak[i], v_bd[i]) for i in nc]
    u = [_dot(tinv[i].astype(BF16), blockdiag(from_state[i][:c] + from_v[i])) for i in nc]
    u_bd = [blockdiag(u[i]) for i in nc]
    o = [from_state[i][c:] + _dot(cat(n_rb[i], n_rk[i], axis=1), cat(u_bd[i], v_bd[i])) for i in nc]
    u_b = [u[i].astype(BF16) for i in nc]
    v_b = [v[i].astype(BF16) for i in nc]
    upd = [_dot_tn(u_b[i], (b[i] * e_out[i]).astype(BF16)) + _dot_tn(v_b[i], (kd[i] * e_out[i]).astype(BF16))
           for i in nc]
    ht_new = [ht[i] * jnp.exp(c_tot[i]) + jnp.where(on_diag, upd[i], 0.0) for i in nc]
    bonus = [_head_sum(r[i] * kd[i] * rk[i], ones_bd) * v[i] for i in nc]
    return [(o[i] + bonus[i], ht_new[i]) for i in nc]


def _rwkv_scan_kernel(rf, vf, kkf, lwf, alf, kdf, rb, vb, kkb, lwb, alb, kdb, rk_ref,
                      yf_ref, yb_ref, h_ref):
    @pl.when(pl.program_id(2) == 0)
    def _():
        h_ref[...] = jnp.zeros_like(h_ref)

    n_chunks = rf.shape[0] // CHUNK
    n_groups = rf.shape[1] // QW

    def body(ci, carry):
        chains = []
        for d, (refs, y_ref) in enumerate((((rf, vf, kkf, lwf, alf, kdf), yf_ref),
                                           ((rb, vb, kkb, lwb, alb, kdb), yb_ref))):
            cidx = ci if d == 0 else n_chunks - 1 - ci
            rows = pl.ds(pl.multiple_of(cidx * CHUNK, CHUNK), CHUNK)
            for g in range(n_groups):
                lanes = slice(g * QW, (g + 1) * QW)
                vals = [x[rows, lanes] for x in refs] + [rk_ref[:, lanes], h_ref[d, g]]
                chains.append((d, g, y_ref, rows, lanes, vals))
        results = _scan_chunks([tuple(vals) + (d == 1,) for d, _, _, _, _, vals in chains])
        for (d, g, y_ref, rows, lanes, _), (y, ht) in zip(chains, results):
            y_ref[rows, lanes] = y
            h_ref[d, g] = ht
        return carry

    lax.fori_loop(0, n_chunks, body, 0)


def _rwkv_scan(r, v, kk, lw_f, lw_b, al_f, al_b, kd_f, kd_b, r_k, tb=256, groups=4):
    b, s, w = r.shape
    nblk = s // tb
    wb = groups * QW
    fwd = pl.BlockSpec((None, tb, wb), lambda b_, p, j: (b_, j, p))
    bwd = pl.BlockSpec((None, tb, wb), lambda b_, p, j: (b_, nblk - 1 - j, p))
    return pl.pallas_call(
        _rwkv_scan_kernel,
        grid=(b, w // wb, nblk),
        in_specs=[fwd] * 6 + [bwd] * 6 + [pl.BlockSpec((1, wb), lambda b_, p, j: (0, p))],
        out_specs=[fwd, bwd],
        out_shape=[jax.ShapeDtypeStruct((b, s, w), F32)] * 2,
        scratch_shapes=[pltpu.VMEM((2, groups, QW, QW), F32)],
        compiler_params=_cparams(("parallel", "parallel", "arbitrary")),
        name="rwkv_scan",
    )(r, v, kk, lw_f, al_f, kd_f, r, v, kk, lw_b, al_b, kd_b, r_k.reshape(1, -1))


def _rwkv_post_kernel(yf_ref, yb_ref, g_ref, lg_ref, lb_ref, o_ref):
    ones_bd = _head_ones()
    y = yf_ref[...] + yb_ref[...]
    mu = _head_sum(y, ones_bd) * (1.0 / RWKV_HEAD)
    d = y - mu
    var = _head_sum(d * d, ones_bd) * (1.0 / RWKV_HEAD)
    yn = d * lax.rsqrt(var + LNX_EPS) * lg_ref[...] + lb_ref[...]
    o_ref[...] = (yn * g_ref[...]).astype(o_ref.dtype)


def _rwkv_post(y_f, y_b, g, lnx_g, lnx_b, tm=512, tw=512):
    m, w = y_f.shape
    blk = pl.BlockSpec((tm, tw), lambda i, p: (i, p))
    vec = pl.BlockSpec((1, tw), lambda i, p: (0, p))
    return pl.pallas_call(
        _rwkv_post_kernel,
        grid=(m // tm, w // tw),
        in_specs=[blk, blk, blk, vec, vec],
        out_specs=blk,
        out_shape=jax.ShapeDtypeStruct((m, w), BF16),
        compiler_params=_cparams(("parallel", "parallel")),
        name="rwkv_post",
    )(y_f, y_b, g, lnx_g.reshape(1, -1), lnx_b.reshape(1, -1))


def _merge_kernel(a1_ref, w1_ref, a2_ref, w2_ref, g1_ref, g2_ref, o_ref):
    m1 = _dot(a1_ref[...], w1_ref[...])
    m2 = _dot(a2_ref[...], w2_ref[...])
    o_ref[...] = (_sigmoid(g1_ref[...]) * m1 + _sigmoid(g2_ref[...]) * m2).astype(o_ref.dtype)


def _merge(o_mla, w_up_mla, o_rwkv, w_up_rwkv, z_gate, tm=1024, tn=512):
    m, k = o_mla.shape
    n = w_up_mla.shape[1]
    nj = n // tn
    a_spec = pl.BlockSpec((tm, k), lambda i, j: (i, 0))
    w_spec = pl.BlockSpec((k, tn), lambda i, j: (0, j))
    return pl.pallas_call(
        _merge_kernel,
        grid=(m // tm, nj),
        in_specs=[a_spec, w_spec, a_spec, w_spec,
                  pl.BlockSpec((tm, tn), lambda i, j: (i, j)),
                  pl.BlockSpec((tm, tn), lambda i, j: (i, nj + j))],
        out_specs=pl.BlockSpec((tm, tn), lambda i, j: (i, j)),
        out_shape=jax.ShapeDtypeStruct((m, n), BF16),
        compiler_params=_cparams(("parallel", "arbitrary")),
        name="merge",
    )(o_mla, w_up_mla, o_rwkv, w_up_rwkv, z_gate, z_gate)


def _cross_router_kernel(h_ref, gc_ref, wq_ref, kv_ref, wo_ref, gf_ref, wr_ref, br_ref,
                         h2_ref, n3_ref, info_ref, cnt_ref, carry_ref):
    @pl.when(pl.program_id(0) == 0)
    def _():
        carry_ref[...] = jnp.zeros_like(carry_ref)

    h = h_ref[...]
    hn = (h * lax.rsqrt(jnp.mean(h * h, axis=-1, keepdims=True) + RMS_EPS) * gc_ref[...]).astype(BF16)
    q = (_dot(hn, wq_ref[...]) * (CROSS_HEAD_DIM ** -0.5)).astype(BF16)
    kv = kv_ref[...]
    outs = []
    for hd in range(CROSS_HEADS):
        lo = hd * CROSS_HEAD_DIM
        s = _dot_nt(q[:, lo:lo + CROSS_HEAD_DIM], kv[:, lo:lo + CROSS_HEAD_DIM])
        p = jnp.exp(s - jnp.max(s, axis=-1, keepdims=True))
        p = p / jnp.sum(p, axis=-1, keepdims=True)
        outs.append(_dot(p.astype(BF16), kv[:, CROSS_WIDTH + lo:CROSS_WIDTH + lo + CROSS_HEAD_DIM]))
    o = jnp.concatenate(outs, axis=-1).astype(BF16)
    h2 = h + _dot(o, wo_ref[...])
    h2_ref[...] = h2

    n3 = h2 * lax.rsqrt(jnp.mean(h2 * h2, axis=-1, keepdims=True) + RMS_EPS) * gf_ref[...]
    half = n3.shape[1] // 2
    n3_ref[...] = _pack_bf16_pair(n3[:, :half], n3[:, half:])

    wr = wr_ref[...]
    n_hi, w_hi = n3.astype(BF16), wr.astype(BF16)
    n_lo = (n3 - n_hi.astype(F32)).astype(BF16)
    w_lo = (wr - w_hi.astype(F32)).astype(BF16)
    logits = _dot(n_hi, w_hi) + (_dot(n_hi, w_lo) + _dot(n_lo, w_hi)) + br_ref[...]
    lane = _iota(logits.shape, 1)
    lane_f = lane.astype(F32)
    neg = jnp.float32(-jnp.inf)
    big = jnp.float32(1e9)

    def masked_softmax(mask):
        x = jnp.where(mask, logits, neg)
        e = jnp.exp(x - jnp.max(x, axis=-1, keepdims=True))
        return e / jnp.sum(e, axis=-1, keepdims=True)

    def top1(prob, mask):
        pmax = jnp.max(jnp.where(mask, prob, -1.0), axis=-1, keepdims=True)
        idx = jnp.min(jnp.where(mask & (prob == pmax), lane_f, big), axis=-1, keepdims=True)
        return pmax, idx

    g_mask = (lane >= N_EXPERTS) & (lane < N_EXPERTS + N_GROUPS)
    p_group, g_idx = top1(masked_softmax(g_mask), g_mask)
    g_sel = g_idx - float(N_EXPERTS)
    e_mask = (lane < N_EXPERTS) & (lax.shift_right_logical(lane, 3).astype(F32) == g_sel)
    e_prob = masked_softmax(e_mask)
    p1, i1 = top1(e_prob, e_mask)
    rest = e_mask & (lane_f != i1)
    p2, i2 = top1(e_prob, rest)
    denom = p1 + p2
    w1 = p_group * (p1 / denom)
    w2 = p_group * (p2 / denom)

    tm = logits.shape[0]
    oh1 = jnp.where(lane_f == i1, 1.0, 0.0)
    oh2 = jnp.where(lane_f == i2, 1.0, 0.0)
    before = jnp.where(_iota((tm, tm), 1) < _iota((tm, tm), 0), 1.0, 0.0).astype(BF16)
    carry = carry_ref[...]
    cnt1 = jnp.sum(oh1, axis=0, keepdims=True)
    cnt2 = jnp.sum(oh2, axis=0, keepdims=True)
    rank1 = jnp.sum(oh1 * (carry + _dot(before, oh1.astype(BF16))), axis=-1, keepdims=True)
    rank2 = jnp.sum(oh2 * (carry + cnt1 + _dot(before, oh2.astype(BF16))), axis=-1, keepdims=True)
    carry = carry + cnt1 + cnt2
    carry_ref[...] = carry
    cnt_ref[...] = carry
    info = jnp.zeros_like(logits)
    for k, val in enumerate((i1, i2, w1, w2, rank1, rank2)):
        info = jnp.where(lane == k, val, info)
    info_ref[...] = info


def _cross_router(h1, seq_len, g_cross, wq, kvm, wo, g_ffn, w_r, b_r, tm=512):
    m, d = h1.shape
    full = lambda a: pl.BlockSpec(a.shape, lambda i: (0,) * a.ndim)
    row = lambda n: pl.BlockSpec((tm, n), lambda i: (i, 0))
    gc, gf = g_cross.reshape(1, d), g_ffn.reshape(1, d)
    per_seq = seq_len // tm
    return pl.pallas_call(
        _cross_router_kernel,
        grid=(m // tm,),
        in_specs=[row(d), full(gc), full(wq),
                  pl.BlockSpec((None,) + kvm.shape[1:], lambda i: (i // per_seq, 0, 0)),
                  full(wo), full(gf), full(w_r), full(b_r)],
        out_specs=[row(d), row(d // 2), row(LANES), pl.BlockSpec((1, LANES), lambda i: (0, 0))],
        out_shape=[jax.ShapeDtypeStruct((m, d), F32), jax.ShapeDtypeStruct((m, d // 2), jnp.uint32),
                   jax.ShapeDtypeStruct((m, LANES), F32), jax.ShapeDtypeStruct((1, LANES), F32)],
        scratch_shapes=[pltpu.VMEM((1, LANES), F32)],
        compiler_params=_cparams(("arbitrary",)),
        name="cross_router",
    )(h1, gc, wq, kvm, wo, gf, w_r, b_r)


def _gather_rows(src_hbm, idx_ref, base, dst, sem, n_rows):
    def issue(g, carry):
        for u in range(GATHER_UNROLL):
            r = g * GATHER_UNROLL + u
            src_row = idx_ref[base + r]
            pltpu.make_async_copy(src_hbm.at[pl.ds(src_row, 1), :], dst.at[pl.ds(r, 1), :], sem).start()
        return carry
    lax.fori_loop(0, n_rows // GATHER_UNROLL, issue, 0)


def _wait_rows(src_hbm, dst, sem, n_rows):
    pltpu.make_async_copy(src_hbm.at[pl.ds(0, n_rows), :], dst, sem).wait()


def _moe_dispatch_kernel(pos1, pos2, x_ref, xs_zero, xs_ref, sem):
    del xs_zero
    i = pl.program_id(0)
    tm = x_ref.shape[0]

    def issue(g, carry):
        for u in range(GATHER_UNROLL):
            r = g * GATHER_UNROLL + u
            src = x_ref.at[pl.ds(r, 1), :]
            pltpu.make_async_copy(src, xs_ref.at[pl.ds(pos1[i * tm + r], 1), :], sem.at[0]).start()
            pltpu.make_async_copy(src, xs_ref.at[pl.ds(pos2[i * tm + r], 1), :], sem.at[1]).start()
        return carry

    lax.fori_loop(0, tm // GATHER_UNROLL, issue, 0)
    for k in range(2):
        pltpu.make_async_copy(x_ref, xs_ref.at[pl.ds(0, tm), :], sem.at[k]).wait()


def _moe_dispatch(n3p, pos1, pos2, p_max, tm=512):
    m, w = n3p.shape
    grid_spec = pltpu.PrefetchScalarGridSpec(
        num_scalar_prefetch=2,
        grid=(m // tm,),
        in_specs=[pl.BlockSpec((tm, w), lambda i, p1, p2: (i, 0)),
                  pl.BlockSpec(memory_space=pl.ANY)],
        out_specs=pl.BlockSpec(memory_space=pl.ANY),
        scratch_shapes=[pltpu.SemaphoreType.DMA((2,))],
    )
    return pl.pallas_call(
        _moe_dispatch_kernel,
        grid_spec=grid_spec,
        out_shape=jax.ShapeDtypeStruct((p_max, w), n3p.dtype),
        input_output_aliases={3: 0},
        compiler_params=_cparams(("arbitrary",)),
        name="moe_dispatch",
    )(pos1, pos2, n3p, jnp.zeros((p_max, w), n3p.dtype))


def _moe_expert_kernel(tile_expert, n_valid, x_ref, wg_ref, wu_ref, wd_ref, y_ref, wg_b, wu_b, wd_b):
    t = pl.program_id(0)
    changed = jnp.logical_or(t == 0, tile_expert[t] != tile_expert[jnp.maximum(t - 1, 0)])

    @pl.when(changed)
    def _():
        wg_b[...] = wg_ref[...].astype(BF16)
        wu_b[...] = wu_ref[...].astype(BF16)
        wd_b[...] = wd_ref[...].astype(BF16)

    @pl.when(t < n_valid[0])
    def _():
        x_lo, x_hi = _unpack_bf16_pair(x_ref[...])
        half = x_lo.shape[1]
        hg = _dot(x_lo, wg_b[:half, :]) + _dot(x_hi, wg_b[half:, :])
        hu = _dot(x_lo, wu_b[:half, :]) + _dot(x_hi, wu_b[half:, :])
        hid = (hg * _sigmoid(hg) * hu).astype(BF16)
        y_ref[...] = _dot(hid, wd_b[...])

    @pl.when(t >= n_valid[0])
    def _():
        y_ref[...] = jnp.zeros_like(y_ref)


def _moe_experts(xs, tile_expert, n_valid, w_eg, w_eu, w_ed):
    p_max, w = xs.shape
    ne, d, f = w_eg.shape
    last = lambda t, nv: jnp.minimum(t, nv[0] - 1)
    grid_spec = pltpu.PrefetchScalarGridSpec(
        num_scalar_prefetch=2,
        grid=(p_max // MOE_TILE,),
        in_specs=[pl.BlockSpec((MOE_TILE, w), lambda t, te, nv: (last(t, nv), 0)),
                  pl.BlockSpec((None, d, f), lambda t, te, nv: (te[t], 0, 0)),
                  pl.BlockSpec((None, d, f), lambda t, te, nv: (te[t], 0, 0)),
                  pl.BlockSpec((None, f, d), lambda t, te, nv: (te[t], 0, 0))],
        out_specs=pl.BlockSpec((MOE_TILE, d), lambda t, te, nv: (t, 0)),
        scratch_shapes=[pltpu.VMEM((d, f), BF16), pltpu.VMEM((d, f), BF16), pltpu.VMEM((f, d), BF16)],
    )
    return pl.pallas_call(
        _moe_expert_kernel,
        grid_spec=grid_spec,
        out_shape=jax.ShapeDtypeStruct((p_max, d), F32),
        compiler_params=_cparams(("arbitrary",)),
        name="moe_experts",
    )(tile_expert, n_valid, xs, w_eg, w_eu, w_ed)


def _moe_combine_kernel(pos1, pos2, y_hbm, h_ref, info_ref, g_ref, o_ref, buf1, buf2, sem):
    i = pl.program_id(0)
    ni = pl.num_programs(0)
    tm = h_ref.shape[0]
    slot = lax.rem(i, 2)

    def start(step, s):
        _gather_rows(y_hbm, pos1, step * tm, buf1.at[s], sem.at[0, s], tm)
        _gather_rows(y_hbm, pos2, step * tm, buf2.at[s], sem.at[1, s], tm)

    @pl.when(i == 0)
    def _():
        start(0, 0)

    @pl.when(i + 1 < ni)
    def _():
        start(i + 1, 1 - slot)

    _wait_rows(y_hbm, buf1.at[slot], sem.at[0, slot], tm)
    _wait_rows(y_hbm, buf2.at[slot], sem.at[1, slot], tm)
    info = info_ref[...]
    y = h_ref[...] + info[:, 2:3] * buf1[slot] + info[:, 3:4] * buf2[slot]
    o_ref[...] = y * lax.rsqrt(jnp.mean(y * y, axis=-1, keepdims=True) + RMS_EPS) * g_ref[...]


def _moe_combine(ys, pos1, pos2, h2, info, g_final, tm=256):
    m, d = h2.shape
    grid_spec = pltpu.PrefetchScalarGridSpec(
        num_scalar_prefetch=2,
        grid=(m // tm,),
        in_specs=[pl.BlockSpec(memory_space=pl.ANY),
                  pl.BlockSpec((tm, d), lambda i, p1, p2: (i, 0)),
                  pl.BlockSpec((tm, LANES), lambda i, p1, p2: (i, 0)),
                  pl.BlockSpec((1, d), lambda i, p1, p2: (0, 0))],
        out_specs=pl.BlockSpec((tm, d), lambda i, p1, p2: (i, 0)),
        scratch_shapes=[pltpu.VMEM((2, tm, d), F32), pltpu.VMEM((2, tm, d), F32),
                        pltpu.SemaphoreType.DMA((2, 2))],
    )
    return pl.pallas_call(
        _moe_combine_kernel,
        grid_spec=grid_spec,
        out_shape=jax.ShapeDtypeStruct((m, d), F32),
        compiler_params=_cparams(("arbitrary",)),
        name="moe_combine",
    )(pos1, pos2, ys, h2, info, g_final.reshape(1, d))


def _moe_plan(info, counts):
    m = info.shape[0]
    p_max = 2 * m + N_EXPERTS * MOE_TILE
    n_tiles = p_max // MOE_TILE
    e1 = info[:, 0].astype(jnp.int32)
    e2 = info[:, 1].astype(jnp.int32)
    rank1 = info[:, 4].astype(jnp.int32)
    rank2 = info[:, 5].astype(jnp.int32)
    cnt = counts[0, :N_EXPERTS].astype(jnp.int32)
    tiles_e = (cnt + MOE_TILE - 1) // MOE_TILE
    tile_end = jnp.cumsum(tiles_e)
    off_pad = (tile_end - tiles_e) * MOE_TILE
    pos1 = off_pad[e1] + rank1
    pos2 = off_pad[e2] + rank2
    n_valid = tile_end[-1:]
    t_idx = jnp.arange(n_tiles, dtype=jnp.int32)
    t_clip = jnp.minimum(t_idx, n_valid[0] - 1)
    tile_expert = jnp.sum((tile_end[None, :] <= t_clip[:, None]).astype(jnp.int32), axis=1)
    return pos1, pos2, n_valid, tile_expert, p_max


def _pad_cols(w, n):
    return jnp.pad(w, ((0, 0), (0, n - w.shape[1])))


def _pad_rows(w, n):
    return jnp.pad(w, ((0, n - w.shape[0]), (0, 0)))


def _split_lora(w):
    o = 0
    parts = []
    for width in (DECAY_LORA, DECAY_LORA, ICLR_LORA, ICLR_LORA):
        parts.append(_pad_cols(w[:, o:o + width], LORA_PAD))
        o += width
    parts.append(w[:, o:o + GATE_LORA])
    return jnp.concatenate(parts, axis=1)


def kernel(x, mem, positions, g_mix, w_in, g_q, w_uq, g_kv, w_ukv, shift_conv, w0_f, w2_f, w0_b, w2_b, a0_f, a2_f, a0_b, a2_b, g2, k_k, k_a, r_k, lnx_g, lnx_b, w_up_mla, w_up_rwkv, w_out, g_cross, g_mem, wq_c, wkv_c, wo_c, g_ffn, w_rg, b_rg, w_re, b_re, w_eg, w_eu, w_ed, g_final):
    bsz, seq, d = x.shape
    m = bsz * seq
    depth = w_in.shape[0]
    h = x.reshape(m, d)
    pos = positions.reshape(m, 1)
    lane = np.arange(LANES)
    invf = jnp.asarray(np.where(lane < ROPE_DIM, 1.0, 0.0), F32) * (
        ROPE_BASE ** (-jnp.asarray(lane % (ROPE_DIM // 2), F32) * (2.0 / ROPE_DIM)))
    invf = invf.reshape(1, LANES)
    assert depth == 1, "the MoE kernel applies the final norm, so it must be the last layer"
    for l in range(depth):
        wi = w_in[l]
        w_mla = _pad_cols(wi[:, :MLA_IN], MLA_IN + (LANES - ROPE_DIM)).astype(BF16)
        rw = wi[:, MLA_IN:MLA_IN + RWKV_IN]
        w_rkv = rw[:, :3 * RWKV_WIDTH].astype(BF16)
        w_lora = _split_lora(rw[:, 3 * RWKV_WIDTH:]).astype(BF16)
        w_gate = wi[:, MLA_IN + RWKV_IN:].astype(BF16)
        sc = shift_conv[l]
        sc_rkv = sc[:, :3 * RWKV_WIDTH]
        sc_lora = _split_lora(sc[:, 3 * RWKV_WIDTH:])
        wq = w_uq[l].reshape(Q_LORA, MLA_HEADS, QK_DIM)
        wq = jnp.pad(wq, ((0, 0), (0, 0), (0, QK_PAD - QK_DIM))).reshape(Q_LORA, MLA_HEADS * QK_PAD)
        wkv = w_ukv[l].reshape(KV_LORA, MLA_HEADS, NOPE_DIM + V_DIM)
        wkv = jnp.concatenate([wkv[:, :, :NOPE_DIM].reshape(KV_LORA, -1),
                               wkv[:, :, NOPE_DIM:].reshape(KV_LORA, -1)], axis=1)
        lora_rows = lambda w: _pad_rows(w, LORA_PAD).astype(BF16)
        w_router = _pad_cols(jnp.concatenate(
            [jnp.moveaxis(w_re[l], 0, 1).reshape(d, N_EXPERTS), w_rg[l]], axis=1), LANES)
        b_router = _pad_cols(jnp.concatenate([b_re[l].reshape(1, N_EXPERTS), b_rg[l].reshape(1, N_GROUPS)],
                                             axis=1), LANES)

        n1 = _rmsnorm(h, g_mix[l])
        z_mla = _matmul(n1, w_mla, F32, 1024, w_mla.shape[1], name="in_proj_mla")
        z_rkv = _matmul(n1, w_rkv, F32, 1024, 512, name="in_proj_rkv")
        z_lora = _matmul(n1, w_lora, F32, 1024, w_lora.shape[1], name="in_proj_lora")
        z_gate = _matmul(n1, w_gate, F32, 1024, 512, name="in_proj_gate")

        q_cat, k_cat, v_mla = _mla_proj(z_mla, pos, invf, g_q[l], g_kv[l], wq.astype(BF16), wkv.astype(BF16))
        o_mla = _mla_attn(q_cat.reshape(bsz, seq, -1), k_cat.reshape(bsz, seq, -1),
                          v_mla.reshape(bsz, seq, -1)).reshape(m, -1)

        (r, v, kk, lw_f, lw_b, al_f, al_b, kd_f, kd_b, gate) = _rwkv_prep(
            z_rkv, z_lora, seq, sc_rkv, sc_lora, lora_rows(w2_f[l]), lora_rows(w2_b[l]),
            lora_rows(a2_f[l]), lora_rows(a2_b[l]), g2[l].astype(BF16),
            w0_f[l], w0_b[l], a0_f[l], a0_b[l], k_k[l], k_a[l])
        sh = lambda t: t.reshape(bsz, seq, RWKV_WIDTH)
        y_f, y_b = _rwkv_scan(sh(r), sh(v), sh(kk), sh(lw_f), sh(lw_b), sh(al_f), sh(al_b),
                              sh(kd_f), sh(kd_b), r_k[l])
        o_rwkv = _rwkv_post(y_f.reshape(m, -1), y_b.reshape(m, -1), gate, lnx_g[l], lnx_b[l])

        merged = _merge(o_mla, w_up_mla[l].astype(BF16), o_rwkv, w_up_rwkv[l].astype(BF16), z_gate)
        h1 = _matmul(merged, w_out[l].astype(BF16), F32, 1024, 512, res=h, name="out_proj")

        mem_n = _rmsnorm(mem.reshape(bsz * MEM_LEN, d), g_mem[l], tm=MEM_LEN)
        kvm = _matmul(mem_n, wkv_c[l].astype(BF16), BF16, bsz * MEM_LEN, 512, name="mem_kv")
        h2, n3, info, counts = _cross_router(h1, seq, g_cross[l], wq_c[l].astype(BF16),
                                             kvm.reshape(bsz, MEM_LEN, 2 * CROSS_WIDTH),
                                             wo_c[l].astype(BF16), g_ffn[l], w_router, b_router)

        pos1, pos2, n_valid, tile_expert, p_max = _moe_plan(info, counts)
        flat = lambda w: w.reshape((N_EXPERTS,) + w.shape[2:])
        xs = _moe_dispatch(n3, pos1, pos2, p_max)
        ys = _moe_experts(xs, tile_expert, n_valid, flat(w_eg[l]), flat(w_eu[l]), flat(w_ed[l]))
        out = _moe_combine(ys, pos1, pos2, h2, info, g_final)
    return out.reshape(bsz, seq, d)
```

```python
import functools

import jax
import jax.numpy as jnp
import numpy as np
from jax import lax
from jax.experimental import pallas as pl
from jax.experimental.pallas import tpu as pltpu

F32 = jnp.float32
BF16 = jnp.bfloat16

D_MODEL = 2048
MEM_LEN = 256
RMS_EPS = 1e-6
MLA_HEADS = 16
Q_LORA = 512
KV_LORA = 256
NOPE_DIM = 128
ROPE_DIM = 64
V_DIM = 128
QK_DIM = NOPE_DIM + ROPE_DIM
ROPE_BASE = 10000.0
RWKV_HEAD = 64
RWKV_HEADS = D_MODEL // RWKV_HEAD
RWKV_WIDTH = D_MODEL
DECAY_LORA = 96
ICLR_LORA = 96
GATE_LORA = 256
LNX_EPS = 64e-5
CROSS_HEADS = 4
CROSS_HEAD_DIM = 128
CROSS_WIDTH = CROSS_HEADS * CROSS_HEAD_DIM
N_GROUPS = 4
EXPERTS_PER_GROUP = 8
N_EXPERTS = N_GROUPS * EXPERTS_PER_GROUP
D_EXPERT = 512
MLA_IN = Q_LORA + KV_LORA + ROPE_DIM
RWKV_IN = 3 * RWKV_WIDTH + 2 * DECAY_LORA + 2 * ICLR_LORA + GATE_LORA

LANES = 128
SUBLANES = 8
QK_PAD = 256
LORA_PAD = LANES
QUAD = 4
QW = QUAD * RWKV_HEAD
CHUNK = 64
ATTN_TK = 1024
MOE_TILE = 256
GATHER_UNROLL = 8
LOG2E = 1.4426950408889634
VMEM_LIMIT = 56 * 1024 * 1024


def _cparams(sem, vmem=VMEM_LIMIT, flags=None):
    return pltpu.CompilerParams(dimension_semantics=sem, vmem_limit_bytes=vmem, flags=flags)


def _iota(shape, dim):
    return lax.broadcasted_iota(jnp.int32, shape, dim)


def _sigmoid(x):
    return 1.0 / (1.0 + jnp.exp(-x))


def _dot(a, b):
    return jnp.dot(a, b, preferred_element_type=F32)


def _dot_nt(a, b):
    return lax.dot_general(a, b, (((1,), (1,)), ((), ())), preferred_element_type=F32)


def _dot_tn(a, b):
    return lax.dot_general(a, b, (((0,), (0,)), ((), ())), preferred_element_type=F32)


def _head_ones():
    r = lax.shift_right_logical(_iota((LANES, LANES), 0), 6)
    c = lax.shift_right_logical(_iota((LANES, LANES), 1), 6)
    return jnp.where(r == c, 1.0, 0.0).astype(BF16)


def _head_sum(x, ones_bd):
    w = ones_bd.shape[0]
    hi = x.astype(BF16)
    lo = (x - hi.astype(F32)).astype(BF16)
    parts = [_dot(hi[:, j:j + w], ones_bd) + _dot(lo[:, j:j + w], ones_bd) for j in range(0, x.shape[1], w)]
    return parts[0] if len(parts) == 1 else jnp.concatenate(parts, axis=1)


def _rmsnorm_kernel(x_ref, g_ref, o_ref):
    x = x_ref[...]
    y = x * lax.rsqrt(jnp.mean(x * x, axis=-1, keepdims=True) + RMS_EPS)
    o_ref[...] = (y * g_ref[...]).astype(o_ref.dtype)


def _rmsnorm(x, g, tm=512):
    m, d = x.shape
    return pl.pallas_call(
        _rmsnorm_kernel,
        grid=(m // tm,),
        in_specs=[pl.BlockSpec((tm, d), lambda i: (i, 0)),
                  pl.BlockSpec((1, d), lambda i: (0, 0))],
        out_specs=pl.BlockSpec((tm, d), lambda i: (i, 0)),
        out_shape=jax.ShapeDtypeStruct((m, d), BF16),
        compiler_params=_cparams(("parallel",)),
        name="rmsnorm",
    )(x, g.reshape(1, d))


def _mm_kernel(a_ref, b_ref, o_ref):
    o_ref[...] = _dot(a_ref[...], b_ref[...]).astype(o_ref.dtype)


def _mm_res_kernel(a_ref, b_ref, r_ref, o_ref):
    o_ref[...] = (r_ref[...] + _dot(a_ref[...], b_ref[...])).astype(o_ref.dtype)


def _matmul(a, b, out_dtype, tm, tn, res=None, name="matmul"):
    m, k = a.shape
    n = b.shape[1]
    in_specs = [pl.BlockSpec((tm, k), lambda i, j: (i, 0)),
                pl.BlockSpec((k, tn), lambda i, j: (0, j))]
    args = [a, b]
    kern = _mm_kernel
    if res is not None:
        in_specs.append(pl.BlockSpec((tm, tn), lambda i, j: (i, j)))
        args.append(res)
        kern = _mm_res_kernel
    return pl.pallas_call(
        kern,
        grid=(m // tm, n // tn),
        in_specs=in_specs,
        out_specs=pl.BlockSpec((tm, tn), lambda i, j: (i, j)),
        out_shape=jax.ShapeDtypeStruct((m, n), out_dtype),
        compiler_params=_cparams(("parallel", "arbitrary")),
        name=name,
    )(*args)


def _mla_proj_kernel(z_ref, pos_ref, invf_ref, gq_ref, gkv_ref, wq_ref, wkv_ref,
                     q_ref, k_ref, v_ref):
    tm = z_ref.shape[0]
    z = z_ref[...]

    def norm(c, g):
        return (c * lax.rsqrt(jnp.mean(c * c, axis=-1, keepdims=True) + RMS_EPS) * g).astype(BF16)

    cq = norm(z[:, :Q_LORA], gq_ref[...])
    ckv = norm(z[:, Q_LORA:Q_LORA + KV_LORA], gkv_ref[...])
    q = _dot(cq, wq_ref[...]) * (QK_DIM ** -0.5 * LOG2E)
    kv = _dot(ckv, wkv_ref[...])

    ang = pos_ref[...].astype(F32) * invf_ref[...]
    lane = _iota((tm, LANES), 1)
    half = ROPE_DIM // 2
    cos, sin = jnp.cos(ang), jnp.sin(ang)
    c_tab = jnp.where(lane < ROPE_DIM, cos, 0.0)
    s_up = jnp.where((lane >= half) & (lane < ROPE_DIM), sin, 0.0)
    s_dn = jnp.where(lane < half, -sin, 0.0)

    def rope(x):
        return (x * c_tab + pltpu.roll(x, half, 1) * s_up
                + pltpu.roll(x, LANES - half, 1) * s_dn)

    k_r = rope(z[:, Q_LORA + KV_LORA:]).astype(BF16)
    for h in range(MLA_HEADS):
        lo = h * QK_PAD
        q_ref[:, lo:lo + NOPE_DIM] = q[:, lo:lo + NOPE_DIM].astype(BF16)
        q_ref[:, lo + NOPE_DIM:lo + QK_PAD] = rope(q[:, lo + NOPE_DIM:lo + QK_PAD]).astype(BF16)
        k_ref[:, lo:lo + NOPE_DIM] = kv[:, h * NOPE_DIM:(h + 1) * NOPE_DIM].astype(BF16)
        k_ref[:, lo + NOPE_DIM:lo + QK_PAD] = k_r
    v_ref[...] = kv[:, MLA_HEADS * NOPE_DIM:].astype(BF16)


def _mla_proj(z_mla, pos, invf, g_q, g_kv, wq, wkv, tm=256):
    m, w = z_mla.shape
    full = lambda shape: pl.BlockSpec(shape, lambda i: (0, 0))
    row = lambda n: pl.BlockSpec((tm, n), lambda i: (i, 0))
    return pl.pallas_call(
        _mla_proj_kernel,
        grid=(m // tm,),
        in_specs=[row(w), row(1), full((1, LANES)), full((1, Q_LORA)), full((1, KV_LORA)),
                  full(wq.shape), full(wkv.shape)],
        out_specs=[row(MLA_HEADS * QK_PAD), row(MLA_HEADS * QK_PAD), row(MLA_HEADS * V_DIM)],
        out_shape=[jax.ShapeDtypeStruct((m, MLA_HEADS * QK_PAD), BF16),
                   jax.ShapeDtypeStruct((m, MLA_HEADS * QK_PAD), BF16),
                   jax.ShapeDtypeStruct((m, MLA_HEADS * V_DIM), BF16)],
        compiler_params=_cparams(("parallel",)),
        name="mla_proj",
    )(z_mla, pos, invf, g_q.reshape(1, -1), g_kv.reshape(1, -1), wq, wkv)


def _mla_attn_kernel(q_ref, k_ref, v_ref, o_ref):
    tq = q_ref.shape[0]
    nk = k_ref.shape[0] // ATTN_TK
    q = q_ref[...]

    def scores(j):
        return _dot_nt(q, k_ref[j * ATTN_TK:(j + 1) * ATTN_TK, :])

    m = jnp.full((tq, 1), -jnp.inf, F32)
    l = jnp.zeros((tq, 1), F32)
    acc = jnp.zeros((tq, V_DIM), F32)
    s_next = scores(0)
    for j in range(nk):
        s = s_next
        if j + 1 < nk:
            s_next = scores(j + 1)
        m_new = jnp.maximum(m, jnp.max(s, axis=-1, keepdims=True))
        alpha = jnp.exp2(m - m_new)
        p = jnp.exp2(s - m_new)
        l = alpha * l + jnp.sum(p, axis=-1, keepdims=True)
        acc = alpha * acc + _dot(p.astype(BF16), v_ref[j * ATTN_TK:(j + 1) * ATTN_TK, :])
        m = m_new
    o_ref[...] = (acc / l).astype(o_ref.dtype)


def _mla_attn(q, k, v, tq=512):
    b, s, _ = q.shape
    return pl.pallas_call(
        _mla_attn_kernel,
        grid=(b, MLA_HEADS, s // tq),
        in_specs=[pl.BlockSpec((None, tq, QK_PAD), lambda b_, h, i: (b_, i, h)),
                  pl.BlockSpec((None, s, QK_PAD), lambda b_, h, i: (b_, 0, h)),
                  pl.BlockSpec((None, s, V_DIM), lambda b_, h, i: (b_, 0, h))],
        out_specs=pl.BlockSpec((None, tq, V_DIM), lambda b_, h, i: (b_, i, h)),
        out_shape=jax.ShapeDtypeStruct((b, s, MLA_HEADS * V_DIM), BF16),
        compiler_params=_cparams(("parallel", "parallel", "arbitrary")),
        name="mla_attn",
    )(q, k, v)


def _rwkv_prep_kernel(seq_len,
                      r_ref, rp_ref, rn_ref, k_ref, kp_ref, kn_ref, v_ref, vp_ref, vn_ref,
                      l_ref, lp_ref, ln_ref, scr_ref, sck_ref, scv_ref, scl_ref,
                      w2f_ref, w2b_ref, a2f_ref, a2b_ref, g2_ref,
                      w0f_ref, w0b_ref, a0f_ref, a0b_ref, kk_w_ref, ka_ref,
                      ro_ref, vo_ref, kko_ref, lwf_ref, lwb_ref, alf_ref, alb_ref,
                      kdf_ref, kdb_ref, g_ref):
    tm = r_ref.shape[0]
    i = pl.program_id(0)
    first = lax.rem(i * tm, seq_len) == 0
    last = lax.rem((i + 1) * tm, seq_len) == 0

    def shift(z_ref, zp_ref, zn_ref, w_ref):
        z = z_ref[...]
        rows = _iota(z.shape, 0)
        prev_row = jnp.where(first, 0.0, zp_ref[SUBLANES - 1:SUBLANES, :])
        next_row = jnp.where(last, 0.0, zn_ref[0:1, :])
        z_prev = jnp.where(rows == 0, prev_row, pltpu.roll(z, 1, 0))
        z_next = jnp.where(rows == tm - 1, next_row, pltpu.roll(z, tm - 1, 0))
        return w_ref[0:1, :] * z_prev + w_ref[1:2, :] * z + w_ref[2:3, :] * z_next

    r = shift(r_ref, rp_ref, rn_ref, scr_ref)
    k = shift(k_ref, kp_ref, kn_ref, sck_ref)
    v = shift(v_ref, vp_ref, vn_ref, scv_ref)
    lo = shift(l_ref, lp_ref, ln_ref, scl_ref)
    xw_f = lo[:, 0 * LORA_PAD:1 * LORA_PAD]
    xw_b = lo[:, 1 * LORA_PAD:2 * LORA_PAD]
    xa_f = lo[:, 2 * LORA_PAD:3 * LORA_PAD]
    xa_b = lo[:, 3 * LORA_PAD:4 * LORA_PAD]
    xg = lo[:, 4 * LORA_PAD:]

    def log_decay(xw, w0_ref, w2_ref):
        y = -(w0_ref[...] + _dot(jnp.tanh(xw).astype(BF16), w2_ref[...]))
        softplus = jnp.maximum(y, 0.0) + jnp.log(1.0 + jnp.exp(-jnp.abs(y)))
        return -jnp.exp(-softplus - 0.5)

    def rate(xa, a0_ref, a2_ref):
        return _sigmoid(a0_ref[...] + _dot(xa.astype(BF16), a2_ref[...]))

    al_f = rate(xa_f, a0f_ref, a2f_ref)
    al_b = rate(xa_b, a0b_ref, a2b_ref)
    kk = k * kk_w_ref[...]
    kk = kk * lax.rsqrt(_head_sum(kk * kk, _head_ones()) + 1e-12)
    ka = ka_ref[...]

    ro_ref[...] = r
    vo_ref[...] = v
    kko_ref[...] = kk
    lwf_ref[...] = log_decay(xw_f, w0f_ref, w2f_ref)
    lwb_ref[...] = log_decay(xw_b, w0b_ref, w2b_ref)
    alf_ref[...] = al_f
    alb_ref[...] = al_b
    kdf_ref[...] = k * (1.0 + (al_f - 1.0) * ka)
    kdb_ref[...] = k * (1.0 + (al_b - 1.0) * ka)
    g_ref[...] = _dot(_sigmoid(xg).astype(BF16), g2_ref[...])


def _rwkv_prep(z_rkv, z_lora, seq_len, sc_rkv, sc_lora, w2f, w2b, a2f, a2b, g2,
               w0f, w0b, a0f, a0b, k_k, k_a, tm=512, tw=512):
    m = z_rkv.shape[0]
    wl = z_lora.shape[1]
    nb = RWKV_WIDTH // tw
    rb = tm // SUBLANES
    nrb = m // SUBLANES

    def main(seg):
        return pl.BlockSpec((tm, tw), lambda i, p: (i, seg * nb + p))

    def prev(seg):
        return pl.BlockSpec((SUBLANES, tw),
                            lambda i, p: (jnp.maximum(i * rb - 1, 0), seg * nb + p))

    def nxt(seg):
        return pl.BlockSpec((SUBLANES, tw),
                            lambda i, p: (jnp.minimum((i + 1) * rb, nrb - 1), seg * nb + p))

    def sc(seg):
        return pl.BlockSpec((3, tw), lambda i, p: (0, seg * nb + p))

    colblk = lambda rows: pl.BlockSpec((rows, tw), lambda i, p: (0, p))
    in_specs = []
    for seg in range(3):
        in_specs += [main(seg), prev(seg), nxt(seg)]
    in_specs += [pl.BlockSpec((tm, wl), lambda i, p: (i, 0)),
                 pl.BlockSpec((SUBLANES, wl), lambda i, p: (jnp.maximum(i * rb - 1, 0), 0)),
                 pl.BlockSpec((SUBLANES, wl), lambda i, p: (jnp.minimum((i + 1) * rb, nrb - 1), 0))]
    in_specs += [sc(0), sc(1), sc(2), pl.BlockSpec((3, wl), lambda i, p: (0, 0))]
    in_specs += [colblk(LORA_PAD)] * 4 + [colblk(GATE_LORA)] + [colblk(1)] * 6
    out_spec = pl.BlockSpec((tm, tw), lambda i, p: (i, p))
    n_out = 10
    row = lambda a: a.reshape(1, -1)
    return pl.pallas_call(
        functools.partial(_rwkv_prep_kernel, seq_len),
        grid=(m // tm, nb),
        in_specs=in_specs,
        out_specs=[out_spec] * n_out,
        out_shape=[jax.ShapeDtypeStruct((m, RWKV_WIDTH), F32)] * n_out,
        compiler_params=_cparams(("parallel", "arbitrary")),
        name="rwkv_prep",
    )(z_rkv, z_rkv, z_rkv, z_rkv, z_rkv, z_rkv, z_rkv, z_rkv, z_rkv,
      z_lora, z_lora, z_lora, sc_rkv, sc_rkv, sc_rkv, sc_lora,
      w2f, w2b, a2f, a2b, g2, row(w0f), row(w0b), row(a0f), row(a0b), row(k_k), row(k_a))


def _scan_chunks(chains):
    c = CHUNK
    nc = range(len(chains))
    r, v, kk, lw, al, kd, rk, ht, rev = [list(t) for t in zip(*chains)]
    t_i = _iota((c, c), 0)
    s_i = _iota((c, c), 1)
    tri = {False: jnp.where(s_i <= t_i, 1.0, 0.0).astype(BF16),
           True: jnp.where(s_i >= t_i, 1.0, 0.0).astype(BF16)}
    row = _iota((c, QW), 0)
    col = jnp.bitwise_and(_iota((c, QW), 1), RWKV_HEAD - 1)
    strict = {False: col < row, True: col > row}
    incl = {False: col <= row, True: col >= row}
    eye = jnp.where(row == col, 1.0, 0.0)
    lane_head = lax.shift_right_logical(_iota((c, QW), 1), 6)
    head_mask = [lane_head == h for h in range(QUAD)]
    on_diag = (lax.shift_right_logical(_iota((QW, QW), 0), 6)
               == lax.shift_right_logical(_iota((QW, QW), 1), 6))

    def blockdiag(x):
        xb = x.astype(BF16)
        zero = jnp.zeros_like(xb)
        return jnp.concatenate([jnp.where(head_mask[h], xb, zero) for h in range(QUAD)], axis=0)

    def cat(x, y, axis=0):
        return jnp.concatenate([x, y], axis=axis)

    a = [-kk[i] for i in nc]
    b = [kk[i] * al[i] for i in nc]
    lw_hi = [lw[i].astype(BF16) for i in nc]
    lw_lo = [(lw[i] - lw_hi[i].astype(F32)).astype(BF16) for i in nc]
    cum = [_dot(tri[rev[i]], lw_hi[i]) + _dot(tri[rev[i]], lw_lo[i]) for i in nc]
    cum_prev = [cum[i] - lw[i] for i in nc]
    ref_row = [c // 2 if rev[i] else c // 2 - 1 for i in nc]
    tot_row = [0 if rev[i] else c - 1 for i in nc]
    c_ref = [cum[i][ref_row[i]:ref_row[i] + 1, :] for i in nc]
    c_tot = [cum[i][tot_row[i]:tot_row[i] + 1, :] for i in nc]
    e_inv = [jnp.exp(c_ref[i] - cum[i]) for i in nc]
    e_out = [jnp.exp(c_tot[i] - cum[i]) for i in nc]
    e_cur = [jnp.exp(cum[i] - c_ref[i]) for i in nc]
    lhs = [jnp.concatenate([a[i] * jnp.exp(cum_prev[i] - c_ref[i]), r[i] * e_cur[i], r[i] * rk[i] * e_cur[i]],
                           axis=0).astype(BF16) for i in nc]
    rhs = [cat(blockdiag(b[i] * e_inv[i]), blockdiag(kd[i] * e_inv[i])) for i in nc]
    nmat = [_dot_nt(lhs[i], rhs[i]) for i in nc]
    n_ab = [jnp.where(strict[rev[i]], nmat[i][:c, :QW], 0.0) for i in nc]
    n_ak = [jnp.where(strict[rev[i]], nmat[i][:c, QW:], 0.0) for i in nc]
    n_rb = [jnp.where(incl[rev[i]], nmat[i][c:2 * c, :QW], 0.0).astype(BF16) for i in nc]
    n_rk = [jnp.where(incl[rev[i]], nmat[i][c:2 * c, QW:], 0.0) + jnp.where(row == col, nmat[i][2 * c:, QW:], 0.0)
            for i in nc]

    n_sq = int(np.log2(c)) - 1
    pw = n_ab
    tinv = [eye + pw[i] for i in nc]
    pw = [_dot(pw[i].astype(BF16), blockdiag(pw[i])) for i in nc]
    for j in range(n_sq):
        if j + 1 < n_sq:
            both = [_dot(cat(tinv[i], pw[i]).astype(BF16), blockdiag(pw[i])) for i in nc]
            tinv = [tinv[i] + both[i][:c] for i in nc]
            pw = [both[i][c:] for i in nc]
        else:
            tinv = [tinv[i] + _dot(tinv[i].astype(BF16), blockdiag(pw[i])) for i in nc]

    v_bd = [blockdiag(v[i]) for i in nc]
    state_lhs = [cat(a[i] * jnp.exp(cum_prev[i]), r[i] * jnp.exp(cum[i])).astype(BF16) for i in nc]
    from_state = [_dot_nt(state_lhs[i], ht[i].astype(BF16)) for i in nc]
    from_v = [_dot(cat(n_ak[i], n_rk[i]).astype(BF16), v_bd[i]) for i in nc]
    u = [_dot(tinv[i].astype(BF16), blockdiag(from_state[i][:c] + from_v[i][:c])) for i in nc]
    o = [from_state[i][c:] + from_v[i][c:] + _dot(n_rb[i], blockdiag(u[i])) for i in nc]
    u_b = [u[i].astype(BF16) for i in nc]
    v_b = [v[i].astype(BF16) for i in nc]
    upd = [_dot_tn(u_b[i], (b[i] * e_out[i]).astype(BF16)) + _dot_tn(v_b[i], (kd[i] * e_out[i]).astype(BF16))
           for i in nc]
    ht_new = [ht[i] * jnp.exp(c_tot[i]) + jnp.where(on_diag, upd[i], 0.0) for i in nc]
    return [(o[i], ht_new[i]) for i in nc]


def _rwkv_scan_kernel(rf, vf, kkf, lwf, alf, kdf, rb, vb, kkb, lwb, alb, kdb, rk_ref,
                      yf_ref, yb_ref, h_ref):
    @pl.when(pl.program_id(2) == 0)
    def _():
        h_ref[...] = jnp.zeros_like(h_ref)

    n_chunks = rf.shape[0] // CHUNK
    n_groups = rf.shape[1] // QW

    def body(ci, carry):
        chains = []
        for d, (refs, y_ref) in enumerate((((rf, vf, kkf, lwf, alf, kdf), yf_ref),
                                           ((rb, vb, kkb, lwb, alb, kdb), yb_ref))):
            cidx = ci if d == 0 else n_chunks - 1 - ci
            rows = pl.ds(pl.multiple_of(cidx * CHUNK, CHUNK), CHUNK)
            for g in range(n_groups):
                lanes = slice(g * QW, (g + 1) * QW)
                vals = [x[rows, lanes] for x in refs] + [rk_ref[:, lanes], h_ref[d, g]]
                chains.append((d, g, y_ref, rows, lanes, vals))
        results = _scan_chunks([tuple(vals) + (d == 1,) for d, _, _, _, _, vals in chains])
        for (d, g, y_ref, rows, lanes, _), (y, ht) in zip(chains, results):
            y_ref[rows, lanes] = y
            h_ref[d, g] = ht
        return carry

    lax.fori_loop(0, n_chunks, body, 0)


def _rwkv_scan(r, v, kk, lw_f, lw_b, al_f, al_b, kd_f, kd_b, r_k, tb=128, groups=8):
    b, s, w = r.shape
    nblk = s // tb
    wb = groups * QW
    fwd = pl.BlockSpec((None, tb, wb), lambda b_, p, j: (b_, j, p))
    bwd = pl.BlockSpec((None, tb, wb), lambda b_, p, j: (b_, nblk - 1 - j, p))
    return pl.pallas_call(
        _rwkv_scan_kernel,
        grid=(b, w // wb, nblk),
        in_specs=[fwd] * 6 + [bwd] * 6 + [pl.BlockSpec((1, wb), lambda b_, p, j: (0, p))],
        out_specs=[fwd, bwd],
        out_shape=[jax.ShapeDtypeStruct((b, s, w), F32)] * 2,
        scratch_shapes=[pltpu.VMEM((2, groups, QW, QW), F32)],
        compiler_params=_cparams(("parallel", "parallel", "arbitrary")),
        name="rwkv_scan",
    )(r, v, kk, lw_f, al_f, kd_f, r, v, kk, lw_b, al_b, kd_b, r_k.reshape(1, -1))


def _rwkv_post_kernel(yf_ref, yb_ref, g_ref, lg_ref, lb_ref, o_ref):
    ones_bd = _head_ones()
    y = yf_ref[...] + yb_ref[...]
    mu = _head_sum(y, ones_bd) * (1.0 / RWKV_HEAD)
    d = y - mu
    var = _head_sum(d * d, ones_bd) * (1.0 / RWKV_HEAD)
    yn = d * lax.rsqrt(var + LNX_EPS) * lg_ref[...] + lb_ref[...]
    o_ref[...] = (yn * g_ref[...]).astype(o_ref.dtype)


def _rwkv_post(y_f, y_b, g, lnx_g, lnx_b, tm=512, tw=512):
    m, w = y_f.shape
    blk = pl.BlockSpec((tm, tw), lambda i, p: (i, p))
    vec = pl.BlockSpec((1, tw), lambda i, p: (0, p))
    return pl.pallas_call(
        _rwkv_post_kernel,
        grid=(m // tm, w // tw),
        in_specs=[blk, blk, blk, vec, vec],
        out_specs=blk,
        out_shape=jax.ShapeDtypeStruct((m, w), BF16),
        compiler_params=_cparams(("parallel", "parallel")),
        name="rwkv_post",
    )(y_f, y_b, g, lnx_g.reshape(1, -1), lnx_b.reshape(1, -1))


def _merge_kernel(a1_ref, w1_ref, a2_ref, w2_ref, g1_ref, g2_ref, o_ref):
    m1 = _dot(a1_ref[...], w1_ref[...])
    m2 = _dot(a2_ref[...], w2_ref[...])
    o_ref[...] = (_sigmoid(g1_ref[...]) * m1 + _sigmoid(g2_ref[...]) * m2).astype(o_ref.dtype)


def _merge(o_mla, w_up_mla, o_rwkv, w_up_rwkv, z_gate, tm=1024, tn=512):
    m, k = o_mla.shape
    n = w_up_mla.shape[1]
    nj = n // tn
    a_spec = pl.BlockSpec((tm, k), lambda i, j: (i, 0))
    w_spec = pl.BlockSpec((k, tn), lambda i, j: (0, j))
    return pl.pallas_call(
        _merge_kernel,
        grid=(m // tm, nj),
        in_specs=[a_spec, w_spec, a_spec, w_spec,
                  pl.BlockSpec((tm, tn), lambda i, j: (i, j)),
                  pl.BlockSpec((tm, tn), lambda i, j: (i, nj + j))],
        out_specs=pl.BlockSpec((tm, tn), lambda i, j: (i, j)),
        out_shape=jax.ShapeDtypeStruct((m, n), BF16),
        compiler_params=_cparams(("parallel", "arbitrary")),
        name="merge",
    )(o_mla, w_up_mla, o_rwkv, w_up_rwkv, z_gate, z_gate)


def _cross_router_kernel(h_ref, gc_ref, wq_ref, kv_ref, wo_ref, gf_ref, wr_ref, br_ref,
                         h2_ref, n3_ref, info_ref, cnt_ref, carry_ref):
    @pl.when(pl.program_id(0) == 0)
    def _():
        carry_ref[...] = jnp.zeros_like(carry_ref)

    h = h_ref[...]
    hn = (h * lax.rsqrt(jnp.mean(h * h, axis=-1, keepdims=True) + RMS_EPS) * gc_ref[...]).astype(BF16)
    q = (_dot(hn, wq_ref[...]) * (CROSS_HEAD_DIM ** -0.5)).astype(BF16)
    kv = kv_ref[...]
    outs = []
    for hd in range(CROSS_HEADS):
        lo = hd * CROSS_HEAD_DIM
        s = _dot_nt(q[:, lo:lo + CROSS_HEAD_DIM], kv[:, lo:lo + CROSS_HEAD_DIM])
        p = jnp.exp(s - jnp.max(s, axis=-1, keepdims=True))
        p = p / jnp.sum(p, axis=-1, keepdims=True)
        outs.append(_dot(p.astype(BF16), kv[:, CROSS_WIDTH + lo:CROSS_WIDTH + lo + CROSS_HEAD_DIM]))
    o = jnp.concatenate(outs, axis=-1).astype(BF16)
    h2 = h + _dot(o, wo_ref[...])
    h2_ref[...] = h2

    n3 = h2 * lax.rsqrt(jnp.mean(h2 * h2, axis=-1, keepdims=True) + RMS_EPS) * gf_ref[...]
    n3_ref[...] = n3

    wr = wr_ref[...]
    n_hi, w_hi = n3.astype(BF16), wr.astype(BF16)
    n_lo = (n3 - n_hi.astype(F32)).astype(BF16)
    w_lo = (wr - w_hi.astype(F32)).astype(BF16)
    logits = _dot(n_hi, w_hi) + (_dot(n_hi, w_lo) + _dot(n_lo, w_hi)) + br_ref[...]
    lane = _iota(logits.shape, 1)
    lane_f = lane.astype(F32)
    neg = jnp.float32(-jnp.inf)
    big = jnp.float32(1e9)

    def masked_softmax(mask):
        x = jnp.where(mask, logits, neg)
        e = jnp.exp(x - jnp.max(x, axis=-1, keepdims=True))
        return e / jnp.sum(e, axis=-1, keepdims=True)

    def top1(prob, mask):
        pmax = jnp.max(jnp.where(mask, prob, -1.0), axis=-1, keepdims=True)
        idx = jnp.min(jnp.where(mask & (prob == pmax), lane_f, big), axis=-1, keepdims=True)
        return pmax, idx

    g_mask = (lane >= N_EXPERTS) & (lane < N_EXPERTS + N_GROUPS)
    p_group, g_idx = top1(masked_softmax(g_mask), g_mask)
    g_sel = g_idx - float(N_EXPERTS)
    e_mask = (lane < N_EXPERTS) & (lax.shift_right_logical(lane, 3).astype(F32) == g_sel)
    e_prob = masked_softmax(e_mask)
    p1, i1 = top1(e_prob, e_mask)
    rest = e_mask & (lane_f != i1)
    p2, i2 = top1(e_prob, rest)
    denom = p1 + p2
    w1 = p_group * (p1 / denom)
    w2 = p_group * (p2 / denom)

    tm = logits.shape[0]
    oh1 = jnp.where(lane_f == i1, 1.0, 0.0)
    oh2 = jnp.where(lane_f == i2, 1.0, 0.0)
    before = jnp.where(_iota((tm, tm), 1) < _iota((tm, tm), 0), 1.0, 0.0).astype(BF16)
    carry = carry_ref[...]
    cnt1 = jnp.sum(oh1, axis=0, keepdims=True)
    cnt2 = jnp.sum(oh2, axis=0, keepdims=True)
    rank1 = jnp.sum(oh1 * (carry + _dot(before, oh1.astype(BF16))), axis=-1, keepdims=True)
    rank2 = jnp.sum(oh2 * (carry + cnt1 + _dot(before, oh2.astype(BF16))), axis=-1, keepdims=True)
    carry = carry + cnt1 + cnt2
    carry_ref[...] = carry
    cnt_ref[...] = carry
    info = jnp.zeros_like(logits)
    for k, val in enumerate((i1, i2, w1, w2, rank1, rank2)):
        info = jnp.where(lane == k, val, info)
    info_ref[...] = info


def _cross_router(h1, seq_len, g_cross, wq, kvm, wo, g_ffn, w_r, b_r, tm=512):
    m, d = h1.shape
    full = lambda a: pl.BlockSpec(a.shape, lambda i: (0,) * a.ndim)
    row = lambda n: pl.BlockSpec((tm, n), lambda i: (i, 0))
    gc, gf = g_cross.reshape(1, d), g_ffn.reshape(1, d)
    per_seq = seq_len // tm
    return pl.pallas_call(
        _cross_router_kernel,
        grid=(m // tm,),
        in_specs=[row(d), full(gc), full(wq),
                  pl.BlockSpec((None,) + kvm.shape[1:], lambda i: (i // per_seq, 0, 0)),
                  full(wo), full(gf), full(w_r), full(b_r)],
        out_specs=[row(d), row(d), row(LANES), pl.BlockSpec((1, LANES), lambda i: (0, 0))],
        out_shape=[jax.ShapeDtypeStruct((m, d), F32), jax.ShapeDtypeStruct((m, d), F32),
                   jax.ShapeDtypeStruct((m, LANES), F32), jax.ShapeDtypeStruct((1, LANES), F32)],
        scratch_shapes=[pltpu.VMEM((1, LANES), F32)],
        compiler_params=_cparams(("arbitrary",)),
        name="cross_router",
    )(h1, gc, wq, kvm, wo, gf, w_r, b_r)


def _gather_rows(src_hbm, idx_ref, base, dst, sem, n_rows):
    def issue(g, carry):
        for u in range(GATHER_UNROLL):
            r = g * GATHER_UNROLL + u
            src_row = idx_ref[base + r]
            pltpu.make_async_copy(src_hbm.at[pl.ds(src_row, 1), :], dst.at[pl.ds(r, 1), :], sem).start()
        return carry
    lax.fori_loop(0, n_rows // GATHER_UNROLL, issue, 0)


def _wait_rows(src_hbm, dst, sem, n_rows):
    pltpu.make_async_copy(src_hbm.at[pl.ds(0, n_rows), :], dst, sem).wait()


def _moe_dispatch_kernel(pos1, pos2, x_ref, xs_zero, xs_ref, sem):
    del xs_zero
    i = pl.program_id(0)
    tm = x_ref.shape[0]

    def issue(g, carry):
        for u in range(GATHER_UNROLL):
            r = g * GATHER_UNROLL + u
            src = x_ref.at[pl.ds(r, 1), :]
            pltpu.make_async_copy(src, xs_ref.at[pl.ds(pos1[i * tm + r], 1), :], sem.at[0]).start()
            pltpu.make_async_copy(src, xs_ref.at[pl.ds(pos2[i * tm + r], 1), :], sem.at[1]).start()
        return carry

    lax.fori_loop(0, tm // GATHER_UNROLL, issue, 0)
    for k in range(2):
        pltpu.make_async_copy(x_ref, xs_ref.at[pl.ds(0, tm), :], sem.at[k]).wait()


def _moe_dispatch(n3p, pos1, pos2, p_max, tm=512):
    m, w = n3p.shape
    grid_spec = pltpu.PrefetchScalarGridSpec(
        num_scalar_prefetch=2,
        grid=(m // tm,),
        in_specs=[pl.BlockSpec((tm, w), lambda i, p1, p2: (i, 0)),
                  pl.BlockSpec(memory_space=pl.ANY)],
        out_specs=pl.BlockSpec(memory_space=pl.ANY),
        scratch_shapes=[pltpu.SemaphoreType.DMA((2,))],
    )
    return pl.pallas_call(
        _moe_dispatch_kernel,
        grid_spec=grid_spec,
        out_shape=jax.ShapeDtypeStruct((p_max, w), n3p.dtype),
        input_output_aliases={3: 0},
        compiler_params=_cparams(("arbitrary",)),
        name="moe_dispatch",
    )(pos1, pos2, n3p, jnp.zeros((p_max, w), n3p.dtype))


def _tile_expert(t, tile_end):
    t = jnp.minimum(t, tile_end[N_EXPERTS - 1] - 1)
    e = jnp.int32(0)
    for k in range(N_EXPERTS - 1):
        e = e + (tile_end[k] <= t).astype(jnp.int32)
    return e


def _moe_expert_kernel(tile_end, x_ref, wg_ref, wu_ref, wd_ref, y_ref, wg_b, wu_b, wd_b):
    t = pl.program_id(0)
    n_valid = tile_end[N_EXPERTS - 1]
    changed = jnp.logical_or(t == 0, _tile_expert(t, tile_end) != _tile_expert(jnp.maximum(t - 1, 0), tile_end))

    @pl.when(changed)
    def _():
        wg_b[...] = wg_ref[...].astype(BF16)
        wu_b[...] = wu_ref[...].astype(BF16)
        wd_b[...] = wd_ref[...].astype(BF16)

    @pl.when(t < n_valid)
    def _():
        x = x_ref[...].astype(BF16)
        hg = _dot(x, wg_b[...])
        hu = _dot(x, wu_b[...])
        hid = (hg * _sigmoid(hg) * hu).astype(BF16)
        y_ref[...] = _dot(hid, wd_b[...])

    @pl.when(t >= n_valid)
    def _():
        y_ref[...] = jnp.zeros_like(y_ref)


def _moe_experts(xs, tile_end, w_eg, w_eu, w_ed):
    p_max, w = xs.shape
    ne, d, f = w_eg.shape
    last = lambda t, te: jnp.minimum(t, te[N_EXPERTS - 1] - 1)
    grid_spec = pltpu.PrefetchScalarGridSpec(
        num_scalar_prefetch=1,
        grid=(p_max // MOE_TILE,),
        in_specs=[pl.BlockSpec((MOE_TILE, w), lambda t, te: (last(t, te), 0)),
                  pl.BlockSpec((None, d, f), lambda t, te: (_tile_expert(t, te), 0, 0)),
                  pl.BlockSpec((None, d, f), lambda t, te: (_tile_expert(t, te), 0, 0)),
                  pl.BlockSpec((None, f, d), lambda t, te: (_tile_expert(t, te), 0, 0))],
        out_specs=pl.BlockSpec((MOE_TILE, d), lambda t, te: (t, 0)),
        scratch_shapes=[pltpu.VMEM((d, f), BF16), pltpu.VMEM((d, f), BF16), pltpu.VMEM((f, d), BF16)],
    )
    return pl.pallas_call(
        _moe_expert_kernel,
        grid_spec=grid_spec,
        out_shape=jax.ShapeDtypeStruct((p_max, d), F32),
        compiler_params=_cparams(("arbitrary",)),
        name="moe_experts",
    )(tile_end, xs, w_eg, w_eu, w_ed)


def _moe_combine_kernel(pos1, pos2, y_hbm, h_ref, info_ref, g_ref, o_ref, buf1, buf2, sem):
    i = pl.program_id(0)
    ni = pl.num_programs(0)
    tm = h_ref.shape[0]
    slot = lax.rem(i, 2)

    def start(step, s):
        _gather_rows(y_hbm, pos1, step * tm, buf1.at[s], sem.at[0, s], tm)
        _gather_rows(y_hbm, pos2, step * tm, buf2.at[s], sem.at[1, s], tm)

    @pl.when(i == 0)
    def _():
        start(0, 0)

    @pl.when(i + 1 < ni)
    def _():
        start(i + 1, 1 - slot)

    _wait_rows(y_hbm, buf1.at[slot], sem.at[0, slot], tm)
    _wait_rows(y_hbm, buf2.at[slot], sem.at[1, slot], tm)
    info = info_ref[...]
    y = h_ref[...] + info[:, 2:3] * buf1[slot] + info[:, 3:4] * buf2[slot]
    o_ref[...] = y * lax.rsqrt(jnp.mean(y * y, axis=-1, keepdims=True) + RMS_EPS) * g_ref[...]


def _moe_combine(ys, pos1, pos2, h2, info, g_final, tm=256):
    m, d = h2.shape
    grid_spec = pltpu.PrefetchScalarGridSpec(
        num_scalar_prefetch=2,
        grid=(m // tm,),
        in_specs=[pl.BlockSpec(memory_space=pl.ANY),
                  pl.BlockSpec((tm, d), lambda i, p1, p2: (i, 0)),
                  pl.BlockSpec((tm, LANES), lambda i, p1, p2: (i, 0)),
                  pl.BlockSpec((1, d), lambda i, p1, p2: (0, 0))],
        out_specs=pl.BlockSpec((tm, d), lambda i, p1, p2: (i, 0)),
        scratch_shapes=[pltpu.VMEM((2, tm, d), F32), pltpu.VMEM((2, tm, d), F32),
                        pltpu.SemaphoreType.DMA((2, 2))],
    )
    return pl.pallas_call(
        _moe_combine_kernel,
        grid_spec=grid_spec,
        out_shape=jax.ShapeDtypeStruct((m, d), F32),
        compiler_params=_cparams(("arbitrary",)),
        name="moe_combine",
    )(pos1, pos2, ys, h2, info, g_final.reshape(1, d))


def _moe_plan(info, counts):
    m = info.shape[0]
    p_max = 2 * m + N_EXPERTS * MOE_TILE
    e1 = info[:, 0].astype(jnp.int32)
    e2 = info[:, 1].astype(jnp.int32)
    rank1 = info[:, 4].astype(jnp.int32)
    rank2 = info[:, 5].astype(jnp.int32)
    cnt = counts[0, :N_EXPERTS].astype(jnp.int32)
    tiles_e = (cnt + MOE_TILE - 1) // MOE_TILE
    tile_end = jnp.cumsum(tiles_e)
    off_pad = (tile_end - tiles_e) * MOE_TILE
    pos1 = off_pad[e1] + rank1
    pos2 = off_pad[e2] + rank2
    return pos1, pos2, tile_end, p_max


def _pad_cols(w, n):
    return jnp.pad(w, ((0, 0), (0, n - w.shape[1])))


def _pad_rows(w, n):
    return jnp.pad(w, ((0, n - w.shape[0]), (0, 0)))


def _split_lora(w):
    o = 0
    parts = []
    for width in (DECAY_LORA, DECAY_LORA, ICLR_LORA, ICLR_LORA):
        parts.append(_pad_cols(w[:, o:o + width], LORA_PAD))
        o += width
    parts.append(w[:, o:o + GATE_LORA])
    return jnp.concatenate(parts, axis=1)


def kernel(x, mem, positions, g_mix, w_in, g_q, w_uq, g_kv, w_ukv, shift_conv, w0_f, w2_f, w0_b, w2_b, a0_f, a2_f, a0_b, a2_b, g2, k_k, k_a, r_k, lnx_g, lnx_b, w_up_mla, w_up_rwkv, w_out, g_cross, g_mem, wq_c, wkv_c, wo_c, g_ffn, w_rg, b_rg, w_re, b_re, w_eg, w_eu, w_ed, g_final):
    bsz, seq, d = x.shape
    m = bsz * seq
    depth = w_in.shape[0]
    h = x.reshape(m, d)
    pos = positions.reshape(m, 1)
    lane = np.arange(LANES)
    invf = jnp.asarray(np.where(lane < ROPE_DIM, 1.0, 0.0), F32) * (
        ROPE_BASE ** (-jnp.asarray(lane % (ROPE_DIM // 2), F32) * (2.0 / ROPE_DIM)))
    invf = invf.reshape(1, LANES)
    assert depth == 1, "the MoE kernel applies the final norm, so it must be the last layer"
    for l in range(depth):
        wi = w_in[l]
        w_mla = _pad_cols(wi[:, :MLA_IN], MLA_IN + (LANES - ROPE_DIM)).astype(BF16)
        rw = wi[:, MLA_IN:MLA_IN + RWKV_IN]
        w_rkv = rw[:, :3 * RWKV_WIDTH].astype(BF16)
        w_lora = _split_lora(rw[:, 3 * RWKV_WIDTH:]).astype(BF16)
        w_gate = wi[:, MLA_IN + RWKV_IN:].astype(BF16)
        sc = shift_conv[l]
        sc_rkv = sc[:, :3 * RWKV_WIDTH]
        sc_lora = _split_lora(sc[:, 3 * RWKV_WIDTH:])
        wq = w_uq[l].reshape(Q_LORA, MLA_HEADS, QK_DIM)
        wq = jnp.pad(wq, ((0, 0), (0, 0), (0, QK_PAD - QK_DIM))).reshape(Q_LORA, MLA_HEADS * QK_PAD)
        wkv = w_ukv[l].reshape(KV_LORA, MLA_HEADS, NOPE_DIM + V_DIM)
        wkv = jnp.concatenate([wkv[:, :, :NOPE_DIM].reshape(KV_LORA, -1),
                               wkv[:, :, NOPE_DIM:].reshape(KV_LORA, -1)], axis=1)
        lora_rows = lambda w: _pad_rows(w, LORA_PAD).astype(BF16)
        w_router = _pad_cols(jnp.concatenate(
            [jnp.moveaxis(w_re[l], 0, 1).reshape(d, N_EXPERTS), w_rg[l]], axis=1), LANES)
        b_router = _pad_cols(jnp.concatenate([b_re[l].reshape(1, N_EXPERTS), b_rg[l].reshape(1, N_GROUPS)],
                                             axis=1), LANES)

        n1 = _rmsnorm(h, g_mix[l])
        z_mla = _matmul(n1, w_mla, F32, 1024, w_mla.shape[1], name="in_proj_mla")
        z_rkv = _matmul(n1, w_rkv, F32, 1024, 512, name="in_proj_rkv")
        z_lora = _matmul(n1, w_lora, F32, 1024, w_lora.shape[1], name="in_proj_lora")
        z_gate = _matmul(n1, w_gate, F32, 1024, 512, name="in_proj_gate")

        q_cat, k_cat, v_mla = _mla_proj(z_mla, pos, invf, g_q[l], g_kv[l], wq.astype(BF16), wkv.astype(BF16))
        o_mla = _mla_attn(q_cat.reshape(bsz, seq, -1), k_cat.reshape(bsz, seq, -1),
                          v_mla.reshape(bsz, seq, -1)).reshape(m, -1)

        (r, v, kk, lw_f, lw_b, al_f, al_b, kd_f, kd_b, gate) = _rwkv_prep(
            z_rkv, z_lora, seq, sc_rkv, sc_lora, lora_rows(w2_f[l]), lora_rows(w2_b[l]),
            lora_rows(a2_f[l]), lora_rows(a2_b[l]), g2[l].astype(BF16),
            w0_f[l], w0_b[l], a0_f[l], a0_b[l], k_k[l], k_a[l])
        sh = lambda t: t.reshape(bsz, seq, RWKV_WIDTH)
        y_f, y_b = _rwkv_scan(sh(r), sh(v), sh(kk), sh(lw_f), sh(lw_b), sh(al_f), sh(al_b),
                              sh(kd_f), sh(kd_b), r_k[l])
        o_rwkv = _rwkv_post(y_f.reshape(m, -1), y_b.reshape(m, -1), gate, lnx_g[l], lnx_b[l])

        merged = _merge(o_mla, w_up_mla[l].astype(BF16), o_rwkv, w_up_rwkv[l].astype(BF16), z_gate)
        h1 = _matmul(merged, w_out[l].astype(BF16), F32, 1024, 512, res=h, name="out_proj")

        mem_n = _rmsnorm(mem.reshape(bsz * MEM_LEN, d), g_mem[l], tm=MEM_LEN)
        kvm = _matmul(mem_n, wkv_c[l].astype(BF16), BF16, bsz * MEM_LEN, 512, name="mem_kv")
        h2, n3, info, counts = _cross_router(h1, seq, g_cross[l], wq_c[l].astype(BF16),
                                             kvm.reshape(bsz, MEM_LEN, 2 * CROSS_WIDTH),
                                             wo_c[l].astype(BF16), g_ffn[l], w_router, b_router)

        pos1, pos2, tile_end, p_max = _moe_plan(info, counts)
        flat = lambda w: w.reshape((N_EXPERTS,) + w.shape[2:])
        xs = _moe_dispatch(n3, pos1, pos2, p_max)
        ys = _moe_experts(xs, tile_end, flat(w_eg[l]), flat(w_eu[l]), flat(w_ed[l]))
        out = _moe_combine(ys, pos1, pos2, h2, info, g_final)
    return out.reshape(bsz, seq, d)
```

```python
import functools

import jax
import jax.numpy as jnp
import numpy as np
from jax import lax
from jax.experimental import pallas as pl
from jax.experimental.pallas import tpu as pltpu

F32 = jnp.float32
BF16 = jnp.bfloat16

D_MODEL = 2048
MEM_LEN = 256
RMS_EPS = 1e-6
MLA_HEADS = 16
Q_LORA = 512
KV_LORA = 256
NOPE_DIM = 128
ROPE_DIM = 64
V_DIM = 128
QK_DIM = NOPE_DIM + ROPE_DIM
ROPE_BASE = 10000.0
RWKV_HEAD = 64
RWKV_HEADS = D_MODEL // RWKV_HEAD
RWKV_WIDTH = D_MODEL
DECAY_LORA = 96
ICLR_LORA = 96
GATE_LORA = 256
LNX_EPS = 64e-5
CROSS_HEADS = 4
CROSS_HEAD_DIM = 128
CROSS_WIDTH = CROSS_HEADS * CROSS_HEAD_DIM
N_GROUPS = 4
EXPERTS_PER_GROUP = 8
N_EXPERTS = N_GROUPS * EXPERTS_PER_GROUP
D_EXPERT = 512
MLA_IN = Q_LORA + KV_LORA + ROPE_DIM
RWKV_IN = 3 * RWKV_WIDTH + 2 * DECAY_LORA + 2 * ICLR_LORA + GATE_LORA

LANES = 128
SUBLANES = 8
QK_PAD = 256
LORA_PAD = LANES
QUAD = 4
QW = QUAD * RWKV_HEAD
CHUNK = 64
ATTN_TK = 1024
MOE_TILE = 512
GATHER_UNROLL = 8
LOG2E = 1.4426950408889634
VMEM_LIMIT = 56 * 1024 * 1024


def _cparams(sem, vmem=VMEM_LIMIT, flags=None):
    return pltpu.CompilerParams(dimension_semantics=sem, vmem_limit_bytes=vmem, flags=flags)


def _iota(shape, dim):
    return lax.broadcasted_iota(jnp.int32, shape, dim)


def _sigmoid(x):
    return 1.0 / (1.0 + jnp.exp(-x))


def _dot(a, b):
    return jnp.dot(a, b, preferred_element_type=F32)


def _dot_nt(a, b):
    return lax.dot_general(a, b, (((1,), (1,)), ((), ())), preferred_element_type=F32)


def _dot_tn(a, b):
    return lax.dot_general(a, b, (((0,), (0,)), ((), ())), preferred_element_type=F32)


def _head_ones():
    r = lax.shift_right_logical(_iota((LANES, LANES), 0), 6)
    c = lax.shift_right_logical(_iota((LANES, LANES), 1), 6)
    return jnp.where(r == c, 1.0, 0.0).astype(BF16)


def _head_sum(x, ones_bd):
    w = ones_bd.shape[0]
    hi = x.astype(BF16)
    lo = (x - hi.astype(F32)).astype(BF16)
    parts = [_dot(hi[:, j:j + w], ones_bd) + _dot(lo[:, j:j + w], ones_bd) for j in range(0, x.shape[1], w)]
    return parts[0] if len(parts) == 1 else jnp.concatenate(parts, axis=1)


def _rmsnorm_kernel(x_ref, g_ref, o_ref):
    x = x_ref[...]
    y = x * lax.rsqrt(jnp.mean(x * x, axis=-1, keepdims=True) + RMS_EPS)
    o_ref[...] = (y * g_ref[...]).astype(o_ref.dtype)


def _rmsnorm(x, g, tm=512):
    m, d = x.shape
    return pl.pallas_call(
        _rmsnorm_kernel,
        grid=(m // tm,),
        in_specs=[pl.BlockSpec((tm, d), lambda i: (i, 0)),
                  pl.BlockSpec((1, d), lambda i: (0, 0))],
        out_specs=pl.BlockSpec((tm, d), lambda i: (i, 0)),
        out_shape=jax.ShapeDtypeStruct((m, d), BF16),
        compiler_params=_cparams(("parallel",)),
        name="rmsnorm",
    )(x, g.reshape(1, d))


def _mm_kernel(a_ref, b_ref, o_ref):
    o_ref[...] = _dot(a_ref[...], b_ref[...]).astype(o_ref.dtype)


def _mm_res_kernel(a_ref, b_ref, r_ref, o_ref):
    o_ref[...] = (r_ref[...] + _dot(a_ref[...], b_ref[...])).astype(o_ref.dtype)


def _matmul(a, b, out_dtype, tm, tn, res=None, name="matmul"):
    m, k = a.shape
    n = b.shape[1]
    in_specs = [pl.BlockSpec((tm, k), lambda i, j: (i, 0)),
                pl.BlockSpec((k, tn), lambda i, j: (0, j))]
    args = [a, b]
    kern = _mm_kernel
    if res is not None:
        in_specs.append(pl.BlockSpec((tm, tn), lambda i, j: (i, j)))
        args.append(res)
        kern = _mm_res_kernel
    return pl.pallas_call(
        kern,
        grid=(m // tm, n // tn),
        in_specs=in_specs,
        out_specs=pl.BlockSpec((tm, tn), lambda i, j: (i, j)),
        out_shape=jax.ShapeDtypeStruct((m, n), out_dtype),
        compiler_params=_cparams(("parallel", "arbitrary")),
        name=name,
    )(*args)


def _mla_proj_kernel(z_ref, pos_ref, invf_ref, gq_ref, gkv_ref, wq_ref, wkv_ref,
                     q_ref, k_ref, v_ref):
    tm = z_ref.shape[0]
    z = z_ref[...]

    def norm(c, g):
        return (c * lax.rsqrt(jnp.mean(c * c, axis=-1, keepdims=True) + RMS_EPS) * g).astype(BF16)

    cq = norm(z[:, :Q_LORA], gq_ref[...])
    ckv = norm(z[:, Q_LORA:Q_LORA + KV_LORA], gkv_ref[...])
    q = _dot(cq, wq_ref[...]) * (QK_DIM ** -0.5 * LOG2E)
    kv = _dot(ckv, wkv_ref[...])

    ang = pos_ref[...].astype(F32) * invf_ref[...]
    lane = _iota((tm, LANES), 1)
    half = ROPE_DIM // 2
    cos, sin = jnp.cos(ang), jnp.sin(ang)
    c_tab = jnp.where(lane < ROPE_DIM, cos, 0.0)
    s_up = jnp.where((lane >= half) & (lane < ROPE_DIM), sin, 0.0)
    s_dn = jnp.where(lane < half, -sin, 0.0)

    def rope(x):
        return (x * c_tab + pltpu.roll(x, half, 1) * s_up
                + pltpu.roll(x, LANES - half, 1) * s_dn)

    k_r = rope(z[:, Q_LORA + KV_LORA:]).astype(BF16)
    for h in range(MLA_HEADS):
        lo = h * QK_PAD
        q_ref[:, lo:lo + NOPE_DIM] = q[:, lo:lo + NOPE_DIM].astype(BF16)
        q_ref[:, lo + NOPE_DIM:lo + QK_PAD] = rope(q[:, lo + NOPE_DIM:lo + QK_PAD]).astype(BF16)
        k_ref[:, lo:lo + NOPE_DIM] = kv[:, h * NOPE_DIM:(h + 1) * NOPE_DIM].astype(BF16)
        k_ref[:, lo + NOPE_DIM:lo + QK_PAD] = k_r
    v_ref[...] = kv[:, MLA_HEADS * NOPE_DIM:].astype(BF16)


def _mla_proj(z_mla, pos, invf, g_q, g_kv, wq, wkv, tm=256):
    m, w = z_mla.shape
    full = lambda shape: pl.BlockSpec(shape, lambda i: (0, 0))
    row = lambda n: pl.BlockSpec((tm, n), lambda i: (i, 0))
    return pl.pallas_call(
        _mla_proj_kernel,
        grid=(m // tm,),
        in_specs=[row(w), row(1), full((1, LANES)), full((1, Q_LORA)), full((1, KV_LORA)),
                  full(wq.shape), full(wkv.shape)],
        out_specs=[row(MLA_HEADS * QK_PAD), row(MLA_HEADS * QK_PAD), row(MLA_HEADS * V_DIM)],
        out_shape=[jax.ShapeDtypeStruct((m, MLA_HEADS * QK_PAD), BF16),
                   jax.ShapeDtypeStruct((m, MLA_HEADS * QK_PAD), BF16),
                   jax.ShapeDtypeStruct((m, MLA_HEADS * V_DIM), BF16)],
        compiler_params=_cparams(("parallel",)),
        name="mla_proj",
    )(z_mla, pos, invf, g_q.reshape(1, -1), g_kv.reshape(1, -1), wq, wkv)


def _mla_attn_kernel(q_ref, k_ref, v_ref, o_ref):
    tq = q_ref.shape[0]
    nk = k_ref.shape[0] // ATTN_TK
    q = q_ref[...]

    def scores(j):
        return _dot_nt(q, k_ref[j * ATTN_TK:(j + 1) * ATTN_TK, :])

    m = jnp.full((tq, 1), -jnp.inf, F32)
    l = jnp.zeros((tq, 1), F32)
    acc = jnp.zeros((tq, V_DIM), F32)
    s_next = scores(0)
    for j in range(nk):
        s = s_next
        if j + 1 < nk:
            s_next = scores(j + 1)
        m_new = jnp.maximum(m, jnp.max(s, axis=-1, keepdims=True))
        alpha = jnp.exp2(m - m_new)
        p = jnp.exp2(s - m_new)
        l = alpha * l + jnp.sum(p, axis=-1, keepdims=True)
        acc = alpha * acc + _dot(p.astype(BF16), v_ref[j * ATTN_TK:(j + 1) * ATTN_TK, :])
        m = m_new
    o_ref[...] = (acc / l).astype(o_ref.dtype)


def _mla_attn(q, k, v, tq=1024):
    b, s, _ = q.shape
    return pl.pallas_call(
        _mla_attn_kernel,
        grid=(b, MLA_HEADS, s // tq),
        in_specs=[pl.BlockSpec((None, tq, QK_PAD), lambda b_, h, i: (b_, i, h)),
                  pl.BlockSpec((None, s, QK_PAD), lambda b_, h, i: (b_, 0, h)),
                  pl.BlockSpec((None, s, V_DIM), lambda b_, h, i: (b_, 0, h))],
        out_specs=pl.BlockSpec((None, tq, V_DIM), lambda b_, h, i: (b_, i, h)),
        out_shape=jax.ShapeDtypeStruct((b, s, MLA_HEADS * V_DIM), BF16),
        compiler_params=_cparams(("parallel", "parallel", "arbitrary")),
        name="mla_attn",
    )(q, k, v)


def _rwkv_prep_kernel(seq_len,
                      r_ref, rp_ref, rn_ref, k_ref, kp_ref, kn_ref, v_ref, vp_ref, vn_ref,
                      l_ref, lp_ref, ln_ref, scr_ref, sck_ref, scv_ref, scl_ref,
                      w2f_ref, w2b_ref, a2f_ref, a2b_ref, g2_ref,
                      w0f_ref, w0b_ref, a0f_ref, a0b_ref, kk_w_ref, ka_ref,
                      ro_ref, vo_ref, kko_ref, lwf_ref, lwb_ref, alf_ref, alb_ref,
                      kdf_ref, kdb_ref, g_ref):
    tm = r_ref.shape[0]
    i = pl.program_id(0)
    first = lax.rem(i * tm, seq_len) == 0
    last = lax.rem((i + 1) * tm, seq_len) == 0

    def shift(z_ref, zp_ref, zn_ref, w_ref):
        z = z_ref[...]
        rows = _iota(z.shape, 0)
        prev_row = jnp.where(first, 0.0, zp_ref[SUBLANES - 1:SUBLANES, :])
        next_row = jnp.where(last, 0.0, zn_ref[0:1, :])
        z_prev = jnp.where(rows == 0, prev_row, pltpu.roll(z, 1, 0))
        z_next = jnp.where(rows == tm - 1, next_row, pltpu.roll(z, tm - 1, 0))
        return w_ref[0:1, :] * z_prev + w_ref[1:2, :] * z + w_ref[2:3, :] * z_next

    r = shift(r_ref, rp_ref, rn_ref, scr_ref)
    k = shift(k_ref, kp_ref, kn_ref, sck_ref)
    v = shift(v_ref, vp_ref, vn_ref, scv_ref)
    lo = shift(l_ref, lp_ref, ln_ref, scl_ref)
    xw_f = lo[:, 0 * LORA_PAD:1 * LORA_PAD]
    xw_b = lo[:, 1 * LORA_PAD:2 * LORA_PAD]
    xa_f = lo[:, 2 * LORA_PAD:3 * LORA_PAD]
    xa_b = lo[:, 3 * LORA_PAD:4 * LORA_PAD]
    xg = lo[:, 4 * LORA_PAD:]

    def log_decay(xw, w0_ref, w2_ref):
        y = -(w0_ref[...] + _dot(jnp.tanh(xw).astype(BF16), w2_ref[...]))
        softplus = jnp.maximum(y, 0.0) + jnp.log(1.0 + jnp.exp(-jnp.abs(y)))
        return -jnp.exp(-softplus - 0.5)

    def rate(xa, a0_ref, a2_ref):
        return _sigmoid(a0_ref[...] + _dot(xa.astype(BF16), a2_ref[...]))

    al_f = rate(xa_f, a0f_ref, a2f_ref)
    al_b = rate(xa_b, a0b_ref, a2b_ref)
    kk = k * kk_w_ref[...]
    kk = kk * lax.rsqrt(_head_sum(kk * kk, _head_ones()) + 1e-12)
    ka = ka_ref[...]

    ro_ref[...] = r
    vo_ref[...] = v
    kko_ref[...] = kk
    lwf_ref[...] = log_decay(xw_f, w0f_ref, w2f_ref)
    lwb_ref[...] = log_decay(xw_b, w0b_ref, w2b_ref)
    alf_ref[...] = al_f
    alb_ref[...] = al_b
    kdf_ref[...] = k * (1.0 + (al_f - 1.0) * ka)
    kdb_ref[...] = k * (1.0 + (al_b - 1.0) * ka)
    g_ref[...] = _dot(_sigmoid(xg).astype(BF16), g2_ref[...])


def _rwkv_prep(z_rkv, z_lora, seq_len, sc_rkv, sc_lora, w2f, w2b, a2f, a2b, g2,
               w0f, w0b, a0f, a0b, k_k, k_a, tm=512, tw=512):
    m = z_rkv.shape[0]
    wl = z_lora.shape[1]
    nb = RWKV_WIDTH // tw
    rb = tm // SUBLANES
    nrb = m // SUBLANES

    def main(seg):
        return pl.BlockSpec((tm, tw), lambda i, p: (i, seg * nb + p))

    def prev(seg):
        return pl.BlockSpec((SUBLANES, tw),
                            lambda i, p: (jnp.maximum(i * rb - 1, 0), seg * nb + p))

    def nxt(seg):
        return pl.BlockSpec((SUBLANES, tw),
                            lambda i, p: (jnp.minimum((i + 1) * rb, nrb - 1), seg * nb + p))

    def sc(seg):
        return pl.BlockSpec((3, tw), lambda i, p: (0, seg * nb + p))

    colblk = lambda rows: pl.BlockSpec((rows, tw), lambda i, p: (0, p))
    in_specs = []
    for seg in range(3):
        in_specs += [main(seg), prev(seg), nxt(seg)]
    in_specs += [pl.BlockSpec((tm, wl), lambda i, p: (i, 0)),
                 pl.BlockSpec((SUBLANES, wl), lambda i, p: (jnp.maximum(i * rb - 1, 0), 0)),
                 pl.BlockSpec((SUBLANES, wl), lambda i, p: (jnp.minimum((i + 1) * rb, nrb - 1), 0))]
    in_specs += [sc(0), sc(1), sc(2), pl.BlockSpec((3, wl), lambda i, p: (0, 0))]
    in_specs += [colblk(LORA_PAD)] * 4 + [colblk(GATE_LORA)] + [colblk(1)] * 6
    out_spec = pl.BlockSpec((tm, tw), lambda i, p: (i, p))
    n_out = 10
    row = lambda a: a.reshape(1, -1)
    return pl.pallas_call(
        functools.partial(_rwkv_prep_kernel, seq_len),
        grid=(m // tm, nb),
        in_specs=in_specs,
        out_specs=[out_spec] * n_out,
        out_shape=[jax.ShapeDtypeStruct((m, RWKV_WIDTH), F32)] * n_out,
        compiler_params=_cparams(("parallel", "arbitrary")),
        name="rwkv_prep",
    )(z_rkv, z_rkv, z_rkv, z_rkv, z_rkv, z_rkv, z_rkv, z_rkv, z_rkv,
      z_lora, z_lora, z_lora, sc_rkv, sc_rkv, sc_rkv, sc_lora,
      w2f, w2b, a2f, a2b, g2, row(w0f), row(w0b), row(a0f), row(a0b), row(k_k), row(k_a))


def _scan_chunks(chains):
    c = CHUNK
    nc = range(len(chains))
    r, v, kk, lw, al, kd, rk, ht, rev = [list(t) for t in zip(*chains)]
    t_i = _iota((c, c), 0)
    s_i = _iota((c, c), 1)
    tri = {False: jnp.where(s_i <= t_i, 1.0, 0.0).astype(BF16),
           True: jnp.where(s_i >= t_i, 1.0, 0.0).astype(BF16)}
    row = _iota((c, QW), 0)
    col = jnp.bitwise_and(_iota((c, QW), 1), RWKV_HEAD - 1)
    strict = {False: col < row, True: col > row}
    incl = {False: col <= row, True: col >= row}
    eye = jnp.where(row == col, 1.0, 0.0)
    lane_head = lax.shift_right_logical(_iota((c, QW), 1), 6)
    head_mask = [lane_head == h for h in range(QUAD)]
    on_diag = (lax.shift_right_logical(_iota((QW, QW), 0), 6)
               == lax.shift_right_logical(_iota((QW, QW), 1), 6))

    def blockdiag(x):
        xb = x.astype(BF16)
        zero = jnp.zeros_like(xb)
        return jnp.concatenate([jnp.where(head_mask[h], xb, zero) for h in range(QUAD)], axis=0)

    def cat(x, y, axis=0):
        return jnp.concatenate([x, y], axis=axis)

    a = [-kk[i] for i in nc]
    b = [kk[i] * al[i] for i in nc]
    lw_hi = [lw[i].astype(BF16) for i in nc]
    lw_lo = [(lw[i] - lw_hi[i].astype(F32)).astype(BF16) for i in nc]
    cum = [_dot(tri[rev[i]], lw_hi[i]) + _dot(tri[rev[i]], lw_lo[i]) for i in nc]
    cum_prev = [cum[i] - lw[i] for i in nc]
    ref_row = [c // 2 if rev[i] else c // 2 - 1 for i in nc]
    tot_row = [0 if rev[i] else c - 1 for i in nc]
    c_ref = [cum[i][ref_row[i]:ref_row[i] + 1, :] for i in nc]
    c_tot = [cum[i][tot_row[i]:tot_row[i] + 1, :] for i in nc]
    e_inv = [jnp.exp(c_ref[i] - cum[i]) for i in nc]
    e_out = [jnp.exp(c_tot[i] - cum[i]) for i in nc]
    e_cur = [jnp.exp(cum[i] - c_ref[i]) for i in nc]
    lhs = [jnp.concatenate([a[i] * jnp.exp(cum_prev[i] - c_ref[i]), r[i] * e_cur[i], r[i] * rk[i] * e_cur[i]],
                           axis=0).astype(BF16) for i in nc]
    rhs = [cat(blockdiag(b[i] * e_inv[i]), blockdiag(kd[i] * e_inv[i])) for i in nc]
    nmat = [_dot_nt(lhs[i], rhs[i]) for i in nc]
    n_ab = [jnp.where(strict[rev[i]], nmat[i][:c, :QW], 0.0) for i in nc]
    n_ak = [jnp.where(strict[rev[i]], nmat[i][:c, QW:], 0.0) for i in nc]
    n_rb = [jnp.where(incl[rev[i]], nmat[i][c:2 * c, :QW], 0.0).astype(BF16) for i in nc]
    n_rk = [jnp.where(incl[rev[i]], nmat[i][c:2 * c, QW:], 0.0) + jnp.where(row == col, nmat[i][2 * c:, QW:], 0.0)
            for i in nc]

    n_sq = int(np.log2(c)) - 1
    pw = n_ab
    tinv = [eye + pw[i] for i in nc]
    pw = [_dot(pw[i].astype(BF16), blockdiag(pw[i])) for i in nc]
    for j in range(n_sq):
        if j + 1 < n_sq:
            both = [_dot(cat(tinv[i], pw[i]).astype(BF16), blockdiag(pw[i])) for i in nc]
            tinv = [tinv[i] + both[i][:c] for i in nc]
            pw = [both[i][c:] for i in nc]
        else:
            tinv = [tinv[i] + _dot(tinv[i].astype(BF16), blockdiag(pw[i])) for i in nc]

    v_bd = [blockdiag(v[i]) for i in nc]
    state_lhs = [cat(a[i] * jnp.exp(cum_prev[i]), r[i] * jnp.exp(cum[i])).astype(BF16) for i in nc]
    from_state = [_dot_nt(state_lhs[i], ht[i].astype(BF16)) for i in nc]
    from_v = [_dot(cat(n_ak[i], n_rk[i]).astype(BF16), v_bd[i]) for i in nc]
    u = [_dot(tinv[i].astype(BF16), blockdiag(from_state[i][:c] + from_v[i][:c])) for i in nc]
    o = [from_state[i][c:] + from_v[i][c:] + _dot(n_rb[i], blockdiag(u[i])) for i in nc]
    u_b = [u[i].astype(BF16) for i in nc]
    v_b = [v[i].astype(BF16) for i in nc]
    upd = [_dot_tn(u_b[i], (b[i] * e_out[i]).astype(BF16)) + _dot_tn(v_b[i], (kd[i] * e_out[i]).astype(BF16))
           for i in nc]
    ht_new = [ht[i] * jnp.exp(c_tot[i]) + jnp.where(on_diag, upd[i], 0.0) for i in nc]
    return [(o[i], ht_new[i]) for i in nc]


def _rwkv_scan_kernel(rf, vf, kkf, lwf, alf, kdf, rb, vb, kkb, lwb, alb, kdb, rk_ref,
                      yf_ref, yb_ref, h_ref):
    @pl.when(pl.program_id(2) == 0)
    def _():
        h_ref[...] = jnp.zeros_like(h_ref)

    n_chunks = rf.shape[0] // CHUNK
    n_groups = rf.shape[1] // QW

    def body(ci, carry):
        chains = []
        for d, (refs, y_ref) in enumerate((((rf, vf, kkf, lwf, alf, kdf), yf_ref),
                                           ((rb, vb, kkb, lwb, alb, kdb), yb_ref))):
            cidx = ci if d == 0 else n_chunks - 1 - ci
            rows = pl.ds(pl.multiple_of(cidx * CHUNK, CHUNK), CHUNK)
            for g in range(n_groups):
                lanes = slice(g * QW, (g + 1) * QW)
                vals = [x[rows, lanes] for x in refs] + [rk_ref[:, lanes], h_ref[d, g]]
                chains.append((d, g, y_ref, rows, lanes, vals))
        results = _scan_chunks([tuple(vals) + (d == 1,) for d, _, _, _, _, vals in chains])
        for (d, g, y_ref, rows, lanes, _), (y, ht) in zip(chains, results):
            y_ref[rows, lanes] = y
            h_ref[d, g] = ht
        return carry

    lax.fori_loop(0, n_chunks, body, 0)


def _rwkv_scan(r, v, kk, lw_f, lw_b, al_f, al_b, kd_f, kd_b, r_k, tb=128, groups=8):
    b, s, w = r.shape
    nblk = s // tb
    wb = groups * QW
    fwd = pl.BlockSpec((None, tb, wb), lambda b_, p, j: (b_, j, p))
    bwd = pl.BlockSpec((None, tb, wb), lambda b_, p, j: (b_, nblk - 1 - j, p))
    return pl.pallas_call(
        _rwkv_scan_kernel,
        grid=(b, w // wb, nblk),
        in_specs=[fwd] * 6 + [bwd] * 6 + [pl.BlockSpec((1, wb), lambda b_, p, j: (0, p))],
        out_specs=[fwd, bwd],
        out_shape=[jax.ShapeDtypeStruct((b, s, w), F32)] * 2,
        scratch_shapes=[pltpu.VMEM((2, groups, QW, QW), F32)],
        compiler_params=_cparams(("parallel", "parallel", "arbitrary")),
        name="rwkv_scan",
    )(r, v, kk, lw_f, al_f, kd_f, r, v, kk, lw_b, al_b, kd_b, r_k.reshape(1, -1))


def _rwkv_post_kernel(yf_ref, yb_ref, g_ref, lg_ref, lb_ref, o_ref):
    ones_bd = _head_ones()
    y = yf_ref[...] + yb_ref[...]
    mu = _head_sum(y, ones_bd) * (1.0 / RWKV_HEAD)
    d = y - mu
    var = _head_sum(d * d, ones_bd) * (1.0 / RWKV_HEAD)
    yn = d * lax.rsqrt(var + LNX_EPS) * lg_ref[...] + lb_ref[...]
    o_ref[...] = (yn * g_ref[...]).astype(o_ref.dtype)


def _rwkv_post(y_f, y_b, g, lnx_g, lnx_b, tm=512, tw=512):
    m, w = y_f.shape
    blk = pl.BlockSpec((tm, tw), lambda i, p: (i, p))
    vec = pl.BlockSpec((1, tw), lambda i, p: (0, p))
    return pl.pallas_call(
        _rwkv_post_kernel,
        grid=(m // tm, w // tw),
        in_specs=[blk, blk, blk, vec, vec],
        out_specs=blk,
        out_shape=jax.ShapeDtypeStruct((m, w), BF16),
        compiler_params=_cparams(("parallel", "parallel")),
        name="rwkv_post",
    )(y_f, y_b, g, lnx_g.reshape(1, -1), lnx_b.reshape(1, -1))


def _merge_kernel(a1_ref, w1_ref, a2_ref, w2_ref, g1_ref, g2_ref, o_ref):
    m1 = _dot(a1_ref[...], w1_ref[...])
    m2 = _dot(a2_ref[...], w2_ref[...])
    o_ref[...] = (_sigmoid(g1_ref[...]) * m1 + _sigmoid(g2_ref[...]) * m2).astype(o_ref.dtype)


def _merge(o_mla, w_up_mla, o_rwkv, w_up_rwkv, z_gate, tm=1024, tn=512):
    m, k = o_mla.shape
    n = w_up_mla.shape[1]
    nj = n // tn
    a_spec = pl.BlockSpec((tm, k), lambda i, j: (i, 0))
    w_spec = pl.BlockSpec((k, tn), lambda i, j: (0, j))
    return pl.pallas_call(
        _merge_kernel,
        grid=(m // tm, nj),
        in_specs=[a_spec, w_spec, a_spec, w_spec,
                  pl.BlockSpec((tm, tn), lambda i, j: (i, j)),
                  pl.BlockSpec((tm, tn), lambda i, j: (i, nj + j))],
        out_specs=pl.BlockSpec((tm, tn), lambda i, j: (i, j)),
        out_shape=jax.ShapeDtypeStruct((m, n), BF16),
        compiler_params=_cparams(("parallel", "arbitrary")),
        name="merge",
    )(o_mla, w_up_mla, o_rwkv, w_up_rwkv, z_gate, z_gate)


def _cross_router_kernel(h_ref, gc_ref, wq_ref, kv_ref, wo_ref, gf_ref, wr_ref, br_ref,
                         h2_ref, n3_ref, info_ref, cnt_ref, carry_ref):
    @pl.when(pl.program_id(0) == 0)
    def _():
        carry_ref[...] = jnp.zeros_like(carry_ref)

    h = h_ref[...]
    hn = (h * lax.rsqrt(jnp.mean(h * h, axis=-1, keepdims=True) + RMS_EPS) * gc_ref[...]).astype(BF16)
    q = (_dot(hn, wq_ref[...]) * (CROSS_HEAD_DIM ** -0.5)).astype(BF16)
    kv = kv_ref[...]
    outs = []
    for hd in range(CROSS_HEADS):
        lo = hd * CROSS_HEAD_DIM
        s = _dot_nt(q[:, lo:lo + CROSS_HEAD_DIM], kv[:, lo:lo + CROSS_HEAD_DIM])
        p = jnp.exp(s - jnp.max(s, axis=-1, keepdims=True))
        p = p / jnp.sum(p, axis=-1, keepdims=True)
        outs.append(_dot(p.astype(BF16), kv[:, CROSS_WIDTH + lo:CROSS_WIDTH + lo + CROSS_HEAD_DIM]))
    o = jnp.concatenate(outs, axis=-1).astype(BF16)
    h2 = h + _dot(o, wo_ref[...])
    h2_ref[...] = h2

    n3 = h2 * lax.rsqrt(jnp.mean(h2 * h2, axis=-1, keepdims=True) + RMS_EPS) * gf_ref[...]
    n3_ref[...] = n3

    wr = wr_ref[...]
    n_hi, w_hi = n3.astype(BF16), wr.astype(BF16)
    n_lo = (n3 - n_hi.astype(F32)).astype(BF16)
    w_lo = (wr - w_hi.astype(F32)).astype(BF16)
    logits = _dot(n_hi, w_hi) + (_dot(n_hi, w_lo) + _dot(n_lo, w_hi)) + br_ref[...]
    lane = _iota(logits.shape, 1)
    lane_f = lane.astype(F32)
    neg = jnp.float32(-jnp.inf)
    big = jnp.float32(1e9)

    def masked_softmax(mask):
        x = jnp.where(mask, logits, neg)
        e = jnp.exp(x - jnp.max(x, axis=-1, keepdims=True))
        return e / jnp.sum(e, axis=-1, keepdims=True)

    def top1(prob, mask):
        pmax = jnp.max(jnp.where(mask, prob, -1.0), axis=-1, keepdims=True)
        idx = jnp.min(jnp.where(mask & (prob == pmax), lane_f, big), axis=-1, keepdims=True)
        return pmax, idx

    g_mask = (lane >= N_EXPERTS) & (lane < N_EXPERTS + N_GROUPS)
    p_group, g_idx = top1(masked_softmax(g_mask), g_mask)
    g_sel = g_idx - float(N_EXPERTS)
    e_mask = (lane < N_EXPERTS) & (lax.shift_right_logical(lane, 3).astype(F32) == g_sel)
    e_prob = masked_softmax(e_mask)
    p1, i1 = top1(e_prob, e_mask)
    rest = e_mask & (lane_f != i1)
    p2, i2 = top1(e_prob, rest)
    denom = p1 + p2
    w1 = p_group * (p1 / denom)
    w2 = p_group * (p2 / denom)

    tm = logits.shape[0]
    oh1 = jnp.where(lane_f == i1, 1.0, 0.0)
    oh2 = jnp.where(lane_f == i2, 1.0, 0.0)
    before = jnp.where(_iota((tm, tm), 1) < _iota((tm, tm), 0), 1.0, 0.0).astype(BF16)
    carry = carry_ref[...]
    cnt1 = jnp.sum(oh1, axis=0, keepdims=True)
    cnt2 = jnp.sum(oh2, axis=0, keepdims=True)
    rank1 = jnp.sum(oh1 * (carry + _dot(before, oh1.astype(BF16))), axis=-1, keepdims=True)
    rank2 = jnp.sum(oh2 * (carry + cnt1 + _dot(before, oh2.astype(BF16))), axis=-1, keepdims=True)
    carry = carry + cnt1 + cnt2
    carry_ref[...] = carry
    cnt_ref[...] = carry
    info = jnp.zeros_like(logits)
    for k, val in enumerate((i1, i2, w1, w2, rank1, rank2)):
        info = jnp.where(lane == k, val, info)
    info_ref[...] = info


def _cross_router(h1, seq_len, g_cross, wq, kvm, wo, g_ffn, w_r, b_r, tm=512):
    m, d = h1.shape
    full = lambda a: pl.BlockSpec(a.shape, lambda i: (0,) * a.ndim)
    row = lambda n: pl.BlockSpec((tm, n), lambda i: (i, 0))
    gc, gf = g_cross.reshape(1, d), g_ffn.reshape(1, d)
    per_seq = seq_len // tm
    return pl.pallas_call(
        _cross_router_kernel,
        grid=(m // tm,),
        in_specs=[row(d), full(gc), full(wq),
                  pl.BlockSpec((None,) + kvm.shape[1:], lambda i: (i // per_seq, 0, 0)),
                  full(wo), full(gf), full(w_r), full(b_r)],
        out_specs=[row(d), row(d), row(LANES), pl.BlockSpec((1, LANES), lambda i: (0, 0))],
        out_shape=[jax.ShapeDtypeStruct((m, d), F32), jax.ShapeDtypeStruct((m, d), F32),
                   jax.ShapeDtypeStruct((m, LANES), F32), jax.ShapeDtypeStruct((1, LANES), F32)],
        scratch_shapes=[pltpu.VMEM((1, LANES), F32)],
        compiler_params=_cparams(("arbitrary",)),
        name="cross_router",
    )(h1, gc, wq, kvm, wo, gf, w_r, b_r)


def _gather_rows(src_hbm, idx_ref, base, dst, sem, n_rows):
    def issue(g, carry):
        for u in range(GATHER_UNROLL):
            r = g * GATHER_UNROLL + u
            src_row = idx_ref[base + r]
            pltpu.make_async_copy(src_hbm.at[pl.ds(src_row, 1), :], dst.at[pl.ds(r, 1), :], sem).start()
        return carry
    lax.fori_loop(0, n_rows // GATHER_UNROLL, issue, 0)


def _wait_rows(src_hbm, dst, sem, n_rows):
    pltpu.make_async_copy(src_hbm.at[pl.ds(0, n_rows), :], dst, sem).wait()


def _moe_dispatch_kernel(pos1, pos2, x_ref, xs_zero, xs_ref, sem):
    del xs_zero
    i = pl.program_id(0)
    tm = x_ref.shape[0]

    def issue(g, carry):
        for u in range(GATHER_UNROLL):
            r = g * GATHER_UNROLL + u
            src = x_ref.at[pl.ds(r, 1), :]
            pltpu.make_async_copy(src, xs_ref.at[pl.ds(pos1[i * tm + r], 1), :], sem.at[0]).start()
            pltpu.make_async_copy(src, xs_ref.at[pl.ds(pos2[i * tm + r], 1), :], sem.at[1]).start()
        return carry

    lax.fori_loop(0, tm // GATHER_UNROLL, issue, 0)
    for k in range(2):
        pltpu.make_async_copy(x_ref, xs_ref.at[pl.ds(0, tm), :], sem.at[k]).wait()


def _moe_dispatch(n3p, pos1, pos2, p_max, tm=512):
    m, w = n3p.shape
    grid_spec = pltpu.PrefetchScalarGridSpec(
        num_scalar_prefetch=2,
        grid=(m // tm,),
        in_specs=[pl.BlockSpec((tm, w), lambda i, p1, p2: (i, 0)),
                  pl.BlockSpec(memory_space=pl.ANY)],
        out_specs=pl.BlockSpec(memory_space=pl.ANY),
        scratch_shapes=[pltpu.SemaphoreType.DMA((2,))],
    )
    return pl.pallas_call(
        _moe_dispatch_kernel,
        grid_spec=grid_spec,
        out_shape=jax.ShapeDtypeStruct((p_max, w), n3p.dtype),
        input_output_aliases={3: 0},
        compiler_params=_cparams(("arbitrary",)),
        name="moe_dispatch",
    )(pos1, pos2, n3p, jnp.zeros((p_max, w), n3p.dtype))


def _tile_expert(t, tile_end):
    t = jnp.minimum(t, tile_end[N_EXPERTS - 1] - 1)
    e = jnp.int32(0)
    for k in range(N_EXPERTS - 1):
        e = e + (tile_end[k] <= t).astype(jnp.int32)
    return e


def _moe_expert_kernel(tile_end, x_ref, wg_ref, wu_ref, wd_ref, y_ref, wg_b, wu_b, wd_b):
    t = pl.program_id(0)
    n_valid = tile_end[N_EXPERTS - 1]
    changed = jnp.logical_or(t == 0, _tile_expert(t, tile_end) != _tile_expert(jnp.maximum(t - 1, 0), tile_end))

    @pl.when(changed)
    def _():
        wg_b[...] = wg_ref[...].astype(BF16)
        wu_b[...] = wu_ref[...].astype(BF16)
        wd_b[...] = wd_ref[...].astype(BF16)

    @pl.when(t < n_valid)
    def _():
        x = x_ref[...].astype(BF16)
        hg = _dot(x, wg_b[...])
        hu = _dot(x, wu_b[...])
        hid = (hg * _sigmoid(hg) * hu).astype(BF16)
        y_ref[...] = _dot(hid, wd_b[...])

    @pl.when(t >= n_valid)
    def _():
        y_ref[...] = jnp.zeros_like(y_ref)


def _moe_experts(xs, tile_end, w_eg, w_eu, w_ed):
    p_max, w = xs.shape
    ne, d, f = w_eg.shape
    last = lambda t, te: jnp.minimum(t, te[N_EXPERTS - 1] - 1)
    grid_spec = pltpu.PrefetchScalarGridSpec(
        num_scalar_prefetch=1,
        grid=(p_max // MOE_TILE,),
        in_specs=[pl.BlockSpec((MOE_TILE, w), lambda t, te: (last(t, te), 0)),
                  pl.BlockSpec((None, d, f), lambda t, te: (_tile_expert(t, te), 0, 0)),
                  pl.BlockSpec((None, d, f), lambda t, te: (_tile_expert(t, te), 0, 0)),
                  pl.BlockSpec((None, f, d), lambda t, te: (_tile_expert(t, te), 0, 0))],
        out_specs=pl.BlockSpec((MOE_TILE, d), lambda t, te: (t, 0)),
        scratch_shapes=[pltpu.VMEM((d, f), BF16), pltpu.VMEM((d, f), BF16), pltpu.VMEM((f, d), BF16)],
    )
    return pl.pallas_call(
        _moe_expert_kernel,
        grid_spec=grid_spec,
        out_shape=jax.ShapeDtypeStruct((p_max, d), F32),
        compiler_params=_cparams(("arbitrary",)),
        name="moe_experts",
    )(tile_end, xs, w_eg, w_eu, w_ed)


def _moe_combine_kernel(pos1, pos2, y_hbm, h_ref, info_ref, g_ref, o_ref, buf1, buf2, sem):
    i = pl.program_id(0)
    ni = pl.num_programs(0)
    tm = h_ref.shape[0]
    slot = lax.rem(i, 2)

    def start(step, s):
        _gather_rows(y_hbm, pos1, step * tm, buf1.at[s], sem.at[0, s], tm)
        _gather_rows(y_hbm, pos2, step * tm, buf2.at[s], sem.at[1, s], tm)

    @pl.when(i == 0)
    def _():
        start(0, 0)

    @pl.when(i + 1 < ni)
    def _():
        start(i + 1, 1 - slot)

    _wait_rows(y_hbm, buf1.at[slot], sem.at[0, slot], tm)
    _wait_rows(y_hbm, buf2.at[slot], sem.at[1, slot], tm)
    info = info_ref[...]
    y = h_ref[...] + info[:, 2:3] * buf1[slot] + info[:, 3:4] * buf2[slot]
    o_ref[...] = y * lax.rsqrt(jnp.mean(y * y, axis=-1, keepdims=True) + RMS_EPS) * g_ref[...]


def _moe_combine(ys, pos1, pos2, h2, info, g_final, tm=256):
    m, d = h2.shape
    grid_spec = pltpu.PrefetchScalarGridSpec(
        num_scalar_prefetch=2,
        grid=(m // tm,),
        in_specs=[pl.BlockSpec(memory_space=pl.ANY),
                  pl.BlockSpec((tm, d), lambda i, p1, p2: (i, 0)),
                  pl.BlockSpec((tm, LANES), lambda i, p1, p2: (i, 0)),
                  pl.BlockSpec((1, d), lambda i, p1, p2: (0, 0))],
        out_specs=pl.BlockSpec((tm, d), lambda i, p1, p2: (i, 0)),
        scratch_shapes=[pltpu.VMEM((2, tm, d), F32), pltpu.VMEM((2, tm, d), F32),
                        pltpu.SemaphoreType.DMA((2, 2))],
    )
    return pl.pallas_call(
        _moe_combine_kernel,
        grid_spec=grid_spec,
        out_shape=jax.ShapeDtypeStruct((m, d), F32),
        compiler_params=_cparams(("arbitrary",)),
        name="moe_combine",
    )(pos1, pos2, ys, h2, info, g_final.reshape(1, d))


def _moe_plan_kernel(info_ref, cnt_ref, pos_ref, end_ref):
    cnt = jnp.broadcast_to(cnt_ref[...], (SUBLANES, LANES))
    tiles = jnp.floor((cnt + (MOE_TILE - 1)) * (1.0 / MOE_TILE))
    upto = jnp.where(_iota((LANES, LANES), 0) <= _iota((LANES, LANES), 1), 1.0, 0.0).astype(BF16)
    tile_end = _dot(tiles.astype(BF16), upto)
    first_row = ((tile_end - tiles) * MOE_TILE)[0:1, :]
    info = info_ref[...]
    lane_f = _iota(info.shape, 1).astype(F32)

    def position(e, rank):
        return jnp.sum(jnp.where(lane_f == e, first_row, 0.0), axis=-1, keepdims=True) + rank

    pos1 = position(info[:, 0:1], info[:, 4:5])
    pos2 = position(info[:, 1:2], info[:, 5:6])
    lane = _iota(info.shape, 1)
    pos_ref[...] = jnp.where(lane == 0, pos1, jnp.where(lane == 1, pos2, 0.0)).astype(jnp.int32)
    end_ref[...] = tile_end.astype(jnp.int32)


def _moe_plan(info, counts, tm=1024):
    m = info.shape[0]
    p_max = 2 * m + N_EXPERTS * MOE_TILE
    pos, tile_end = pl.pallas_call(
        _moe_plan_kernel,
        grid=(m // tm,),
        in_specs=[pl.BlockSpec((tm, LANES), lambda i: (i, 0)), pl.BlockSpec((1, LANES), lambda i: (0, 0))],
        out_specs=[pl.BlockSpec((tm, LANES), lambda i: (i, 0)), pl.BlockSpec((SUBLANES, LANES), lambda i: (0, 0))],
        out_shape=[jax.ShapeDtypeStruct((m, LANES), jnp.int32), jax.ShapeDtypeStruct((SUBLANES, LANES), jnp.int32)],
        compiler_params=_cparams(("arbitrary",)),
        name="moe_plan",
    )(info, counts)
    return pos[:, 0], pos[:, 1], tile_end[0, :N_EXPERTS], p_max


def _pad_cols(w, n):
    return jnp.pad(w, ((0, 0), (0, n - w.shape[1])))


def _pad_rows(w, n):
    return jnp.pad(w, ((0, n - w.shape[0]), (0, 0)))


def _split_lora(w):
    o = 0
    parts = []
    for width in (DECAY_LORA, DECAY_LORA, ICLR_LORA, ICLR_LORA):
        parts.append(_pad_cols(w[:, o:o + width], LORA_PAD))
        o += width
    parts.append(w[:, o:o + GATE_LORA])
    return jnp.concatenate(parts, axis=1)


def kernel(x, mem, positions, g_mix, w_in, g_q, w_uq, g_kv, w_ukv, shift_conv, w0_f, w2_f, w0_b, w2_b, a0_f, a2_f, a0_b, a2_b, g2, k_k, k_a, r_k, lnx_g, lnx_b, w_up_mla, w_up_rwkv, w_out, g_cross, g_mem, wq_c, wkv_c, wo_c, g_ffn, w_rg, b_rg, w_re, b_re, w_eg, w_eu, w_ed, g_final):
    bsz, seq, d = x.shape
    m = bsz * seq
    depth = w_in.shape[0]
    h = x.reshape(m, d)
    pos = positions.reshape(m, 1)
    lane = np.arange(LANES)
    invf = jnp.asarray(np.where(lane < ROPE_DIM, 1.0, 0.0), F32) * (
        ROPE_BASE ** (-jnp.asarray(lane % (ROPE_DIM // 2), F32) * (2.0 / ROPE_DIM)))
    invf = invf.reshape(1, LANES)
    assert depth == 1, "the MoE kernel applies the final norm, so it must be the last layer"
    for l in range(depth):
        wi = w_in[l]
        w_mla = _pad_cols(wi[:, :MLA_IN], MLA_IN + (LANES - ROPE_DIM)).astype(BF16)
        rw = wi[:, MLA_IN:MLA_IN + RWKV_IN]
        w_rkv = rw[:, :3 * RWKV_WIDTH].astype(BF16)
        w_lora = _split_lora(rw[:, 3 * RWKV_WIDTH:]).astype(BF16)
        w_gate = wi[:, MLA_IN + RWKV_IN:].astype(BF16)
        sc = shift_conv[l]
        sc_rkv = sc[:, :3 * RWKV_WIDTH]
        sc_lora = _split_lora(sc[:, 3 * RWKV_WIDTH:])
        wq = w_uq[l].reshape(Q_LORA, MLA_HEADS, QK_DIM)
        wq = jnp.pad(wq, ((0, 0), (0, 0), (0, QK_PAD - QK_DIM))).reshape(Q_LORA, MLA_HEADS * QK_PAD)
        wkv = w_ukv[l].reshape(KV_LORA, MLA_HEADS, NOPE_DIM + V_DIM)
        wkv = jnp.concatenate([wkv[:, :, :NOPE_DIM].reshape(KV_LORA, -1),
                               wkv[:, :, NOPE_DIM:].reshape(KV_LORA, -1)], axis=1)
        lora_rows = lambda w: _pad_rows(w, LORA_PAD).astype(BF16)
        w_router = _pad_cols(jnp.concatenate(
            [jnp.moveaxis(w_re[l], 0, 1).reshape(d, N_EXPERTS), w_rg[l]], axis=1), LANES)
        b_router = _pad_cols(jnp.concatenate([b_re[l].reshape(1, N_EXPERTS), b_rg[l].reshape(1, N_GROUPS)],
                                             axis=1), LANES)

        n1 = _rmsnorm(h, g_mix[l])
        z_mla = _matmul(n1, w_mla, F32, 1024, w_mla.shape[1], name="in_proj_mla")
        z_rkv = _matmul(n1, w_rkv, F32, 1024, 512, name="in_proj_rkv")
        z_lora = _matmul(n1, w_lora, F32, 1024, w_lora.shape[1], name="in_proj_lora")
        z_gate = _matmul(n1, w_gate, F32, 1024, 512, name="in_proj_gate")

        q_cat, k_cat, v_mla = _mla_proj(z_mla, pos, invf, g_q[l], g_kv[l], wq.astype(BF16), wkv.astype(BF16))
        o_mla = _mla_attn(q_cat.reshape(bsz, seq, -1), k_cat.reshape(bsz, seq, -1),
                          v_mla.reshape(bsz, seq, -1)).reshape(m, -1)

        (r, v, kk, lw_f, lw_b, al_f, al_b, kd_f, kd_b, gate) = _rwkv_prep(
            z_rkv, z_lora, seq, sc_rkv, sc_lora, lora_rows(w2_f[l]), lora_rows(w2_b[l]),
            lora_rows(a2_f[l]), lora_rows(a2_b[l]), g2[l].astype(BF16),
            w0_f[l], w0_b[l], a0_f[l], a0_b[l], k_k[l], k_a[l])
        sh = lambda t: t.reshape(bsz, seq, RWKV_WIDTH)
        y_f, y_b = _rwkv_scan(sh(r), sh(v), sh(kk), sh(lw_f), sh(lw_b), sh(al_f), sh(al_b),
                              sh(kd_f), sh(kd_b), r_k[l])
        o_rwkv = _rwkv_post(y_f.reshape(m, -1), y_b.reshape(m, -1), gate, lnx_g[l], lnx_b[l])

        merged = _merge(o_mla, w_up_mla[l].astype(BF16), o_rwkv, w_up_rwkv[l].astype(BF16), z_gate)
        h1 = _matmul(merged, w_out[l].astype(BF16), F32, 1024, 512, res=h, name="out_proj")

        mem_n = _rmsnorm(mem.reshape(bsz * MEM_LEN, d), g_mem[l], tm=MEM_LEN)
        kvm = _matmul(mem_n, wkv_c[l].astype(BF16), BF16, bsz * MEM_LEN, 512, name="mem_kv")
        h2, n3, info, counts = _cross_router(h1, seq, g_cross[l], wq_c[l].astype(BF16),
                                             kvm.reshape(bsz, MEM_LEN, 2 * CROSS_WIDTH),
                                             wo_c[l].astype(BF16), g_ffn[l], w_router, b_router)

        pos1, pos2, tile_end, p_max = _moe_plan(info, counts)
        flat = lambda w: w.reshape((N_EXPERTS,) + w.shape[2:])
        xs = _moe_dispatch(n3, pos1, pos2, p_max)
        ys = _moe_experts(xs, tile_end, flat(w_eg[l]), flat(w_eu[l]), flat(w_ed[l]))
        out = _moe_combine(ys, pos1, pos2, h2, info, g_final)
    return out.reshape(bsz, seq, d)
```

```python
import functools

import jax
import jax.numpy as jnp
import numpy as np
from jax import lax
from jax.experimental import pallas as pl
from jax.experimental.pallas import tpu as pltpu

F32 = jnp.float32
BF16 = jnp.bfloat16

D_MODEL = 2048
MEM_LEN = 256
RMS_EPS = 1e-6
MLA_HEADS = 16
Q_LORA = 512
KV_LORA = 256
NOPE_DIM = 128
ROPE_DIM = 64
V_DIM = 128
QK_DIM = NOPE_DIM + ROPE_DIM
ROPE_BASE = 10000.0
RWKV_HEAD = 64
RWKV_HEADS = D_MODEL // RWKV_HEAD
RWKV_WIDTH = D_MODEL
DECAY_LORA = 96
ICLR_LORA = 96
GATE_LORA = 256
LNX_EPS = 64e-5
CROSS_HEADS = 4
CROSS_HEAD_DIM = 128
CROSS_WIDTH = CROSS_HEADS * CROSS_HEAD_DIM
N_GROUPS = 4
EXPERTS_PER_GROUP = 8
N_EXPERTS = N_GROUPS * EXPERTS_PER_GROUP
D_EXPERT = 512
MLA_IN = Q_LORA + KV_LORA + ROPE_DIM
RWKV_IN = 3 * RWKV_WIDTH + 2 * DECAY_LORA + 2 * ICLR_LORA + GATE_LORA

LANES = 128
SUBLANES = 8
QK_PAD = 256
LORA_PAD = LANES
QUAD = 4
QW = QUAD * RWKV_HEAD
CHUNK = 64
ATTN_TK = 1024
MOE_TILE = 512
GATHER_UNROLL = 8
LOG2E = 1.4426950408889634
VMEM_LIMIT = 56 * 1024 * 1024


def _cparams(sem, vmem=VMEM_LIMIT, flags=None):
    return pltpu.CompilerParams(dimension_semantics=sem, vmem_limit_bytes=vmem, flags=flags)


def _iota(shape, dim):
    return lax.broadcasted_iota(jnp.int32, shape, dim)


def _sigmoid(x):
    return 1.0 / (1.0 + jnp.exp(-x))


def _dot(a, b):
    return jnp.dot(a, b, preferred_element_type=F32)


def _dot_nt(a, b):
    return lax.dot_general(a, b, (((1,), (1,)), ((), ())), preferred_element_type=F32)


def _dot_tn(a, b):
    return lax.dot_general(a, b, (((0,), (0,)), ((), ())), preferred_element_type=F32)


def _head_ones():
    r = lax.shift_right_logical(_iota((LANES, LANES), 0), 6)
    c = lax.shift_right_logical(_iota((LANES, LANES), 1), 6)
    return jnp.where(r == c, 1.0, 0.0).astype(BF16)


def _head_sum(x, ones_bd):
    w = ones_bd.shape[0]
    hi = x.astype(BF16)
    lo = (x - hi.astype(F32)).astype(BF16)
    parts = [_dot(hi[:, j:j + w], ones_bd) + _dot(lo[:, j:j + w], ones_bd) for j in range(0, x.shape[1], w)]
    return parts[0] if len(parts) == 1 else jnp.concatenate(parts, axis=1)


def _rmsnorm_kernel(x_ref, g_ref, o_ref):
    x = x_ref[...]
    y = x * lax.rsqrt(jnp.mean(x * x, axis=-1, keepdims=True) + RMS_EPS)
    o_ref[...] = (y * g_ref[...]).astype(o_ref.dtype)


def _rmsnorm(x, g, tm=512):
    m, d = x.shape
    return pl.pallas_call(
        _rmsnorm_kernel,
        grid=(m // tm,),
        in_specs=[pl.BlockSpec((tm, d), lambda i: (i, 0)),
                  pl.BlockSpec((1, d), lambda i: (0, 0))],
        out_specs=pl.BlockSpec((tm, d), lambda i: (i, 0)),
        out_shape=jax.ShapeDtypeStruct((m, d), BF16),
        compiler_params=_cparams(("parallel",)),
        name="rmsnorm",
    )(x, g.reshape(1, d))


def _mm_kernel(a_ref, b_ref, o_ref):
    o_ref[...] = _dot(a_ref[...], b_ref[...]).astype(o_ref.dtype)


def _mm_res_kernel(a_ref, b_ref, r_ref, o_ref):
    o_ref[...] = (r_ref[...] + _dot(a_ref[...], b_ref[...])).astype(o_ref.dtype)


def _matmul(a, b, out_dtype, tm, tn, res=None, name="matmul"):
    m, k = a.shape
    n = b.shape[1]
    in_specs = [pl.BlockSpec((tm, k), lambda i, j: (i, 0)),
                pl.BlockSpec((k, tn), lambda i, j: (0, j))]
    args = [a, b]
    kern = _mm_kernel
    if res is not None:
        in_specs.append(pl.BlockSpec((tm, tn), lambda i, j: (i, j)))
        args.append(res)
        kern = _mm_res_kernel
    return pl.pallas_call(
        kern,
        grid=(m // tm, n // tn),
        in_specs=in_specs,
        out_specs=pl.BlockSpec((tm, tn), lambda i, j: (i, j)),
        out_shape=jax.ShapeDtypeStruct((m, n), out_dtype),
        compiler_params=_cparams(("parallel", "arbitrary")),
        name=name,
    )(*args)


def _mla_proj_kernel(z_ref, pos_ref, invf_ref, gq_ref, gkv_ref, wq_ref, wkv_ref,
                     q_ref, k_ref, v_ref):
    tm = z_ref.shape[0]
    z = z_ref[...]

    def norm(c, g):
        return (c * lax.rsqrt(jnp.mean(c * c, axis=-1, keepdims=True) + RMS_EPS) * g).astype(BF16)

    cq = norm(z[:, :Q_LORA], gq_ref[...])
    ckv = norm(z[:, Q_LORA:Q_LORA + KV_LORA], gkv_ref[...])
    q = _dot(cq, wq_ref[...]) * (QK_DIM ** -0.5 * LOG2E)
    kv = _dot(ckv, wkv_ref[...])

    ang = pos_ref[...].astype(F32) * invf_ref[...]
    lane = _iota((tm, LANES), 1)
    half = ROPE_DIM // 2
    cos, sin = jnp.cos(ang), jnp.sin(ang)
    c_tab = jnp.where(lane < ROPE_DIM, cos, 0.0)
    s_up = jnp.where((lane >= half) & (lane < ROPE_DIM), sin, 0.0)
    s_dn = jnp.where(lane < half, -sin, 0.0)

    def rope(x):
        return (x * c_tab + pltpu.roll(x, half, 1) * s_up
                + pltpu.roll(x, LANES - half, 1) * s_dn)

    k_r = rope(z[:, Q_LORA + KV_LORA:]).astype(BF16)
    for h in range(MLA_HEADS):
        lo = h * QK_PAD
        q_ref[:, lo:lo + NOPE_DIM] = q[:, lo:lo + NOPE_DIM].astype(BF16)
        q_ref[:, lo + NOPE_DIM:lo + QK_PAD] = rope(q[:, lo + NOPE_DIM:lo + QK_PAD]).astype(BF16)
        k_ref[:, lo:lo + NOPE_DIM] = kv[:, h * NOPE_DIM:(h + 1) * NOPE_DIM].astype(BF16)
        k_ref[:, lo + NOPE_DIM:lo + QK_PAD] = k_r
    v_ref[...] = kv[:, MLA_HEADS * NOPE_DIM:].astype(BF16)


def _mla_proj(z_mla, pos, invf, g_q, g_kv, wq, wkv, tm=256):
    m, w = z_mla.shape
    full = lambda shape: pl.BlockSpec(shape, lambda i: (0, 0))
    row = lambda n: pl.BlockSpec((tm, n), lambda i: (i, 0))
    return pl.pallas_call(
        _mla_proj_kernel,
        grid=(m // tm,),
        in_specs=[row(w), row(1), full((1, LANES)), full((1, Q_LORA)), full((1, KV_LORA)),
                  full(wq.shape), full(wkv.shape)],
        out_specs=[row(MLA_HEADS * QK_PAD), row(MLA_HEADS * QK_PAD), row(MLA_HEADS * V_DIM)],
        out_shape=[jax.ShapeDtypeStruct((m, MLA_HEADS * QK_PAD), BF16),
                   jax.ShapeDtypeStruct((m, MLA_HEADS * QK_PAD), BF16),
                   jax.ShapeDtypeStruct((m, MLA_HEADS * V_DIM), BF16)],
        compiler_params=_cparams(("parallel",)),
        name="mla_proj",
    )(z_mla, pos, invf, g_q.reshape(1, -1), g_kv.reshape(1, -1), wq, wkv)


def _mla_attn_kernel(q_ref, k_ref, v_ref, o_ref):
    tq = q_ref.shape[0]
    nk = k_ref.shape[0] // ATTN_TK
    q = q_ref[...]

    def scores(j):
        return _dot_nt(q, k_ref[j * ATTN_TK:(j + 1) * ATTN_TK, :])

    m = jnp.full((tq, 1), -jnp.inf, F32)
    l = jnp.zeros((tq, 1), F32)
    acc = jnp.zeros((tq, V_DIM), F32)
    s_next = scores(0)
    for j in range(nk):
        s = s_next
        if j + 1 < nk:
            s_next = scores(j + 1)
        m_new = jnp.maximum(m, jnp.max(s, axis=-1, keepdims=True))
        alpha = jnp.exp2(m - m_new)
        p = jnp.exp2(s - m_new)
        l = alpha * l + jnp.sum(p, axis=-1, keepdims=True)
        acc = alpha * acc + _dot(p.astype(BF16), v_ref[j * ATTN_TK:(j + 1) * ATTN_TK, :])
        m = m_new
    o_ref[...] = (acc / l).astype(o_ref.dtype)


def _mla_attn(q, k, v, tq=1024):
    b, s, _ = q.shape
    return pl.pallas_call(
        _mla_attn_kernel,
        grid=(b, MLA_HEADS, s // tq),
        in_specs=[pl.BlockSpec((None, tq, QK_PAD), lambda b_, h, i: (b_, i, h)),
                  pl.BlockSpec((None, s, QK_PAD), lambda b_, h, i: (b_, 0, h)),
                  pl.BlockSpec((None, s, V_DIM), lambda b_, h, i: (b_, 0, h))],
        out_specs=pl.BlockSpec((None, tq, V_DIM), lambda b_, h, i: (b_, i, h)),
        out_shape=jax.ShapeDtypeStruct((b, s, MLA_HEADS * V_DIM), BF16),
        compiler_params=_cparams(("parallel", "parallel", "arbitrary")),
        name="mla_attn",
    )(q, k, v)


def _rwkv_prep_kernel(seq_len,
                      r_ref, rp_ref, rn_ref, k_ref, kp_ref, kn_ref, v_ref, vp_ref, vn_ref,
                      l_ref, lp_ref, ln_ref, scr_ref, sck_ref, scv_ref, scl_ref,
                      w2f_ref, w2b_ref, a2f_ref, a2b_ref, g2_ref,
                      w0f_ref, w0b_ref, a0f_ref, a0b_ref, kk_w_ref, ka_ref,
                      ro_ref, vo_ref, kko_ref, lwf_ref, lwb_ref, alf_ref, alb_ref,
                      kdf_ref, kdb_ref, g_ref):
    tm = r_ref.shape[0]
    i = pl.program_id(0)
    first = lax.rem(i * tm, seq_len) == 0
    last = lax.rem((i + 1) * tm, seq_len) == 0

    def shift(z_ref, zp_ref, zn_ref, w_ref):
        z = z_ref[...]
        rows = _iota(z.shape, 0)
        prev_row = jnp.where(first, 0.0, zp_ref[SUBLANES - 1:SUBLANES, :])
        next_row = jnp.where(last, 0.0, zn_ref[0:1, :])
        z_prev = jnp.where(rows == 0, prev_row, pltpu.roll(z, 1, 0))
        z_next = jnp.where(rows == tm - 1, next_row, pltpu.roll(z, tm - 1, 0))
        return w_ref[0:1, :] * z_prev + w_ref[1:2, :] * z + w_ref[2:3, :] * z_next

    r = shift(r_ref, rp_ref, rn_ref, scr_ref)
    k = shift(k_ref, kp_ref, kn_ref, sck_ref)
    v = shift(v_ref, vp_ref, vn_ref, scv_ref)
    lo = shift(l_ref, lp_ref, ln_ref, scl_ref)
    xw_f = lo[:, 0 * LORA_PAD:1 * LORA_PAD]
    xw_b = lo[:, 1 * LORA_PAD:2 * LORA_PAD]
    xa_f = lo[:, 2 * LORA_PAD:3 * LORA_PAD]
    xa_b = lo[:, 3 * LORA_PAD:4 * LORA_PAD]
    xg = lo[:, 4 * LORA_PAD:]

    def log_decay(xw, w0_ref, w2_ref):
        y = -(w0_ref[...] + _dot(jnp.tanh(xw).astype(BF16), w2_ref[...]))
        softplus = jnp.maximum(y, 0.0) + jnp.log(1.0 + jnp.exp(-jnp.abs(y)))
        return -jnp.exp(-softplus - 0.5)

    def rate(xa, a0_ref, a2_ref):
        return _sigmoid(a0_ref[...] + _dot(xa.astype(BF16), a2_ref[...]))

    al_f = rate(xa_f, a0f_ref, a2f_ref)
    al_b = rate(xa_b, a0b_ref, a2b_ref)
    kk = k * kk_w_ref[...]
    kk = kk * lax.rsqrt(_head_sum(kk * kk, _head_ones()) + 1e-12)
    ka = ka_ref[...]

    ro_ref[...] = r
    vo_ref[...] = v
    kko_ref[...] = kk
    lwf_ref[...] = log_decay(xw_f, w0f_ref, w2f_ref)
    lwb_ref[...] = log_decay(xw_b, w0b_ref, w2b_ref)
    alf_ref[...] = al_f
    alb_ref[...] = al_b
    kdf_ref[...] = k * (1.0 + (al_f - 1.0) * ka)
    kdb_ref[...] = k * (1.0 + (al_b - 1.0) * ka)
    g_ref[...] = _dot(_sigmoid(xg).astype(BF16), g2_ref[...])


def _rwkv_prep(z_rkv, z_lora, seq_len, sc_rkv, sc_lora, w2f, w2b, a2f, a2b, g2,
               w0f, w0b, a0f, a0b, k_k, k_a, tm=512, tw=512):
    m = z_rkv.shape[0]
    wl = z_lora.shape[1]
    nb = RWKV_WIDTH // tw
    rb = tm // SUBLANES
    nrb = m // SUBLANES

    def main(seg):
        return pl.BlockSpec((tm, tw), lambda i, p: (i, seg * nb + p))

    def prev(seg):
        return pl.BlockSpec((SUBLANES, tw),
                            lambda i, p: (jnp.maximum(i * rb - 1, 0), seg * nb + p))

    def nxt(seg):
        return pl.BlockSpec((SUBLANES, tw),
                            lambda i, p: (jnp.minimum((i + 1) * rb, nrb - 1), seg * nb + p))

    def sc(seg):
        return pl.BlockSpec((3, tw), lambda i, p: (0, seg * nb + p))

    colblk = lambda rows: pl.BlockSpec((rows, tw), lambda i, p: (0, p))
    in_specs = []
    for seg in range(3):
        in_specs += [main(seg), prev(seg), nxt(seg)]
    in_specs += [pl.BlockSpec((tm, wl), lambda i, p: (i, 0)),
                 pl.BlockSpec((SUBLANES, wl), lambda i, p: (jnp.maximum(i * rb - 1, 0), 0)),
                 pl.BlockSpec((SUBLANES, wl), lambda i, p: (jnp.minimum((i + 1) * rb, nrb - 1), 0))]
    in_specs += [sc(0), sc(1), sc(2), pl.BlockSpec((3, wl), lambda i, p: (0, 0))]
    in_specs += [colblk(LORA_PAD)] * 4 + [colblk(GATE_LORA)] + [colblk(1)] * 6
    out_spec = pl.BlockSpec((tm, tw), lambda i, p: (i, p))
    n_out = 10
    row = lambda a: a.reshape(1, -1)
    return pl.pallas_call(
        functools.partial(_rwkv_prep_kernel, seq_len),
        grid=(m // tm, nb),
        in_specs=in_specs,
        out_specs=[out_spec] * n_out,
        out_shape=[jax.ShapeDtypeStruct((m, RWKV_WIDTH), F32)] * n_out,
        compiler_params=_cparams(("parallel", "arbitrary")),
        name="rwkv_prep",
    )(z_rkv, z_rkv, z_rkv, z_rkv, z_rkv, z_rkv, z_rkv, z_rkv, z_rkv,
      z_lora, z_lora, z_lora, sc_rkv, sc_rkv, sc_rkv, sc_lora,
      w2f, w2b, a2f, a2b, g2, row(w0f), row(w0b), row(a0f), row(a0b), row(k_k), row(k_a))


def _scan_chunks(chains):
    c = CHUNK
    nc = range(len(chains))
    r, v, kk, lw, al, kd, rk, ht, rev = [list(t) for t in zip(*chains)]
    t_i = _iota((c, c), 0)
    s_i = _iota((c, c), 1)
    tri = {False: jnp.where(s_i <= t_i, 1.0, 0.0).astype(BF16),
           True: jnp.where(s_i >= t_i, 1.0, 0.0).astype(BF16)}
    row = _iota((c, QW), 0)
    col = jnp.bitwise_and(_iota((c, QW), 1), RWKV_HEAD - 1)
    strict = {False: col < row, True: col > row}
    incl = {False: col <= row, True: col >= row}
    eye = jnp.where(row == col, 1.0, 0.0)
    lane_head = lax.shift_right_logical(_iota((c, QW), 1), 6)
    head_mask = [lane_head == h for h in range(QUAD)]
    on_diag = (lax.shift_right_logical(_iota((QW, QW), 0), 6)
               == lax.shift_right_logical(_iota((QW, QW), 1), 6))

    def blockdiag(x):
        xb = x.astype(BF16)
        zero = jnp.zeros_like(xb)
        return jnp.concatenate([jnp.where(head_mask[h], xb, zero) for h in range(QUAD)], axis=0)

    def cat(x, y, axis=0):
        return jnp.concatenate([x, y], axis=axis)

    a = [-kk[i] for i in nc]
    b = [kk[i] * al[i] for i in nc]
    lw_hi = [lw[i].astype(BF16) for i in nc]
    lw_lo = [(lw[i] - lw_hi[i].astype(F32)).astype(BF16) for i in nc]
    cum = [_dot(tri[rev[i]], lw_hi[i]) + _dot(tri[rev[i]], lw_lo[i]) for i in nc]
    cum_prev = [cum[i] - lw[i] for i in nc]
    ref_row = [c // 2 if rev[i] else c // 2 - 1 for i in nc]
    tot_row = [0 if rev[i] else c - 1 for i in nc]
    c_ref = [cum[i][ref_row[i]:ref_row[i] + 1, :] for i in nc]
    c_tot = [cum[i][tot_row[i]:tot_row[i] + 1, :] for i in nc]
    e_inv = [jnp.exp(c_ref[i] - cum[i]) for i in nc]
    e_out = [jnp.exp(c_tot[i] - cum[i]) for i in nc]
    e_cur = [jnp.exp(cum[i] - c_ref[i]) for i in nc]
    lhs = [jnp.concatenate([a[i] * jnp.exp(cum_prev[i] - c_ref[i]), r[i] * e_cur[i], r[i] * rk[i] * e_cur[i]],
                           axis=0).astype(BF16) for i in nc]
    rhs = [cat(blockdiag(b[i] * e_inv[i]), blockdiag(kd[i] * e_inv[i])) for i in nc]
    nmat = [_dot_nt(lhs[i], rhs[i]) for i in nc]
    n_ab = [jnp.where(strict[rev[i]], nmat[i][:c, :QW], 0.0) for i in nc]
    n_ak = [jnp.where(strict[rev[i]], nmat[i][:c, QW:], 0.0) for i in nc]
    n_rb = [jnp.where(incl[rev[i]], nmat[i][c:2 * c, :QW], 0.0).astype(BF16) for i in nc]
    n_rk = [jnp.where(incl[rev[i]], nmat[i][c:2 * c, QW:], 0.0) + jnp.where(row == col, nmat[i][2 * c:, QW:], 0.0)
            for i in nc]

    n_sq = int(np.log2(c)) - 1
    pw = n_ab
    tinv = [eye + pw[i] for i in nc]
    pw = [_dot(pw[i].astype(BF16), blockdiag(pw[i])) for i in nc]
    for j in range(n_sq):
        if j + 1 < n_sq:
            both = [_dot(cat(tinv[i], pw[i]).astype(BF16), blockdiag(pw[i])) for i in nc]
            tinv = [tinv[i] + both[i][:c] for i in nc]
            pw = [both[i][c:] for i in nc]
        else:
            tinv = [tinv[i] + _dot(tinv[i].astype(BF16), blockdiag(pw[i])) for i in nc]

    v_bd = [blockdiag(v[i]) for i in nc]
    state_lhs = [cat(a[i] * jnp.exp(cum_prev[i]), r[i] * jnp.exp(cum[i])).astype(BF16) for i in nc]
    from_state = [_dot_nt(state_lhs[i], ht[i].astype(BF16)) for i in nc]
    from_v = [_dot(cat(n_ak[i], n_rk[i]).astype(BF16), v_bd[i]) for i in nc]
    u = [_dot(tinv[i].astype(BF16), blockdiag(from_state[i][:c] + from_v[i][:c])) for i in nc]
    o = [from_state[i][c:] + from_v[i][c:] + _dot(n_rb[i], blockdiag(u[i])) for i in nc]
    u_b = [u[i].astype(BF16) for i in nc]
    v_b = [v[i].astype(BF16) for i in nc]
    upd = [_dot_tn(u_b[i], (b[i] * e_out[i]).astype(BF16)) + _dot_tn(v_b[i], (kd[i] * e_out[i]).astype(BF16))
           for i in nc]
    ht_new = [ht[i] * jnp.exp(c_tot[i]) + jnp.where(on_diag, upd[i], 0.0) for i in nc]
    return [(o[i], ht_new[i]) for i in nc]


def _rwkv_scan_kernel(rf, vf, kkf, lwf, alf, kdf, rb, vb, kkb, lwb, alb, kdb, rk_ref,
                      yf_ref, yb_ref, h_ref):
    @pl.when(pl.program_id(2) == 0)
    def _():
        h_ref[...] = jnp.zeros_like(h_ref)

    n_chunks = rf.shape[0] // CHUNK
    n_groups = rf.shape[1] // QW

    def body(ci, carry):
        chains = []
        for d, (refs, y_ref) in enumerate((((rf, vf, kkf, lwf, alf, kdf), yf_ref),
                                           ((rb, vb, kkb, lwb, alb, kdb), yb_ref))):
            cidx = ci if d == 0 else n_chunks - 1 - ci
            rows = pl.ds(pl.multiple_of(cidx * CHUNK, CHUNK), CHUNK)
            for g in range(n_groups):
                lanes = slice(g * QW, (g + 1) * QW)
                vals = [x[rows, lanes] for x in refs] + [rk_ref[:, lanes], h_ref[d, g]]
                chains.append((d, g, y_ref, rows, lanes, vals))
        results = _scan_chunks([tuple(vals) + (d == 1,) for d, _, _, _, _, vals in chains])
        for (d, g, y_ref, rows, lanes, _), (y, ht) in zip(chains, results):
            y_ref[rows, lanes] = y
            h_ref[d, g] = ht
        return carry

    lax.fori_loop(0, n_chunks, body, 0)


def _rwkv_scan(r, v, kk, lw_f, lw_b, al_f, al_b, kd_f, kd_b, r_k, tb=128, groups=8):
    b, s, w = r.shape
    nblk = s // tb
    wb = groups * QW
    fwd = pl.BlockSpec((None, tb, wb), lambda b_, p, j: (b_, j, p))
    bwd = pl.BlockSpec((None, tb, wb), lambda b_, p, j: (b_, nblk - 1 - j, p))
    return pl.pallas_call(
        _rwkv_scan_kernel,
        grid=(b, w // wb, nblk),
        in_specs=[fwd] * 6 + [bwd] * 6 + [pl.BlockSpec((1, wb), lambda b_, p, j: (0, p))],
        out_specs=[fwd, bwd],
        out_shape=[jax.ShapeDtypeStruct((b, s, w), F32)] * 2,
        scratch_shapes=[pltpu.VMEM((2, groups, QW, QW), F32)],
        compiler_params=_cparams(("parallel", "parallel", "arbitrary")),
        name="rwkv_scan",
    )(r, v, kk, lw_f, al_f, kd_f, r, v, kk, lw_b, al_b, kd_b, r_k.reshape(1, -1))


def _rwkv_post_kernel(yf_ref, yb_ref, g_ref, lg_ref, lb_ref, o_ref):
    ones_bd = _head_ones()
    y = yf_ref[...] + yb_ref[...]
    mu = _head_sum(y, ones_bd) * (1.0 / RWKV_HEAD)
    d = y - mu
    var = _head_sum(d * d, ones_bd) * (1.0 / RWKV_HEAD)
    yn = d * lax.rsqrt(var + LNX_EPS) * lg_ref[...] + lb_ref[...]
    o_ref[...] = (yn * g_ref[...]).astype(o_ref.dtype)


def _rwkv_post(y_f, y_b, g, lnx_g, lnx_b, tm=512, tw=512):
    m, w = y_f.shape
    blk = pl.BlockSpec((tm, tw), lambda i, p: (i, p))
    vec = pl.BlockSpec((1, tw), lambda i, p: (0, p))
    return pl.pallas_call(
        _rwkv_post_kernel,
        grid=(m // tm, w // tw),
        in_specs=[blk, blk, blk, vec, vec],
        out_specs=blk,
        out_shape=jax.ShapeDtypeStruct((m, w), BF16),
        compiler_params=_cparams(("parallel", "parallel")),
        name="rwkv_post",
    )(y_f, y_b, g, lnx_g.reshape(1, -1), lnx_b.reshape(1, -1))


def _merge_kernel(a1_ref, w1_ref, a2_ref, w2_ref, g1_ref, g2_ref, o_ref):
    m1 = _dot(a1_ref[...], w1_ref[...])
    m2 = _dot(a2_ref[...], w2_ref[...])
    o_ref[...] = (_sigmoid(g1_ref[...]) * m1 + _sigmoid(g2_ref[...]) * m2).astype(o_ref.dtype)


def _merge(o_mla, w_up_mla, o_rwkv, w_up_rwkv, z_gate, tm=1024, tn=512):
    m, k = o_mla.shape
    n = w_up_mla.shape[1]
    nj = n // tn
    a_spec = pl.BlockSpec((tm, k), lambda i, j: (i, 0))
    w_spec = pl.BlockSpec((k, tn), lambda i, j: (0, j))
    return pl.pallas_call(
        _merge_kernel,
        grid=(m // tm, nj),
        in_specs=[a_spec, w_spec, a_spec, w_spec,
                  pl.BlockSpec((tm, tn), lambda i, j: (i, j)),
                  pl.BlockSpec((tm, tn), lambda i, j: (i, nj + j))],
        out_specs=pl.BlockSpec((tm, tn), lambda i, j: (i, j)),
        out_shape=jax.ShapeDtypeStruct((m, n), BF16),
        compiler_params=_cparams(("parallel", "arbitrary")),
        name="merge",
    )(o_mla, w_up_mla, o_rwkv, w_up_rwkv, z_gate, z_gate)


def _cross_router_kernel(h_ref, gc_ref, wq_ref, kv_ref, wo_ref, gf_ref, wr_ref, br_ref,
                         h2_ref, n3_ref, info_ref, cnt_ref, carry_ref):
    @pl.when(pl.program_id(0) == 0)
    def _():
        carry_ref[...] = jnp.zeros_like(carry_ref)

    h = h_ref[...]
    hn = (h * lax.rsqrt(jnp.mean(h * h, axis=-1, keepdims=True) + RMS_EPS) * gc_ref[...]).astype(BF16)
    q = (_dot(hn, wq_ref[...]) * (CROSS_HEAD_DIM ** -0.5)).astype(BF16)
    kv = kv_ref[...]
    outs = []
    for hd in range(CROSS_HEADS):
        lo = hd * CROSS_HEAD_DIM
        s = _dot_nt(q[:, lo:lo + CROSS_HEAD_DIM], kv[:, lo:lo + CROSS_HEAD_DIM])
        p = jnp.exp(s - jnp.max(s, axis=-1, keepdims=True))
        p = p / jnp.sum(p, axis=-1, keepdims=True)
        outs.append(_dot(p.astype(BF16), kv[:, CROSS_WIDTH + lo:CROSS_WIDTH + lo + CROSS_HEAD_DIM]))
    o = jnp.concatenate(outs, axis=-1).astype(BF16)
    h2 = h + _dot(o, wo_ref[...])
    h2_ref[...] = h2

    n3 = h2 * lax.rsqrt(jnp.mean(h2 * h2, axis=-1, keepdims=True) + RMS_EPS) * gf_ref[...]
    n3_ref[...] = n3

    wr = wr_ref[...]
    n_hi, w_hi = n3.astype(BF16), wr.astype(BF16)
    n_lo = (n3 - n_hi.astype(F32)).astype(BF16)
    w_lo = (wr - w_hi.astype(F32)).astype(BF16)
    logits = _dot(n_hi, w_hi) + (_dot(n_hi, w_lo) + _dot(n_lo, w_hi)) + br_ref[...]
    lane = _iota(logits.shape, 1)
    lane_f = lane.astype(F32)
    neg = jnp.float32(-jnp.inf)
    big = jnp.float32(1e9)

    def masked_softmax(mask):
        x = jnp.where(mask, logits, neg)
        e = jnp.exp(x - jnp.max(x, axis=-1, keepdims=True))
        return e / jnp.sum(e, axis=-1, keepdims=True)

    def top1(prob, mask):
        pmax = jnp.max(jnp.where(mask, prob, -1.0), axis=-1, keepdims=True)
        idx = jnp.min(jnp.where(mask & (prob == pmax), lane_f, big), axis=-1, keepdims=True)
        return pmax, idx

    g_mask = (lane >= N_EXPERTS) & (lane < N_EXPERTS + N_GROUPS)
    p_group, g_idx = top1(masked_softmax(g_mask), g_mask)
    g_sel = g_idx - float(N_EXPERTS)
    e_mask = (lane < N_EXPERTS) & (lax.shift_right_logical(lane, 3).astype(F32) == g_sel)
    e_prob = masked_softmax(e_mask)
    p1, i1 = top1(e_prob, e_mask)
    rest = e_mask & (lane_f != i1)
    p2, i2 = top1(e_prob, rest)
    denom = p1 + p2
    w1 = p_group * (p1 / denom)
    w2 = p_group * (p2 / denom)

    tm = logits.shape[0]
    oh1 = jnp.where(lane_f == i1, 1.0, 0.0)
    oh2 = jnp.where(lane_f == i2, 1.0, 0.0)
    before = jnp.where(_iota((tm, tm), 1) < _iota((tm, tm), 0), 1.0, 0.0).astype(BF16)
    carry = carry_ref[...]
    cnt1 = jnp.sum(oh1, axis=0, keepdims=True)
    cnt2 = jnp.sum(oh2, axis=0, keepdims=True)
    rank1 = jnp.sum(oh1 * (carry + _dot(before, oh1.astype(BF16))), axis=-1, keepdims=True)
    rank2 = jnp.sum(oh2 * (carry + cnt1 + _dot(before, oh2.astype(BF16))), axis=-1, keepdims=True)
    carry = carry + cnt1 + cnt2
    carry_ref[...] = carry
    cnt_ref[...] = carry
    info = jnp.zeros_like(logits)
    for k, val in enumerate((i1, i2, w1, w2, rank1, rank2)):
        info = jnp.where(lane == k, val, info)
    info_ref[...] = info


def _cross_router(h1, seq_len, g_cross, wq, kvm, wo, g_ffn, w_r, b_r, tm=512):
    m, d = h1.shape
    full = lambda a: pl.BlockSpec(a.shape, lambda i: (0,) * a.ndim)
    row = lambda n: pl.BlockSpec((tm, n), lambda i: (i, 0))
    gc, gf = g_cross.reshape(1, d), g_ffn.reshape(1, d)
    per_seq = seq_len // tm
    return pl.pallas_call(
        _cross_router_kernel,
        grid=(m // tm,),
        in_specs=[row(d), full(gc), full(wq),
                  pl.BlockSpec((None,) + kvm.shape[1:], lambda i: (i // per_seq, 0, 0)),
                  full(wo), full(gf), full(w_r), full(b_r)],
        out_specs=[row(d), row(d), row(LANES), pl.BlockSpec((1, LANES), lambda i: (0, 0))],
        out_shape=[jax.ShapeDtypeStruct((m, d), F32), jax.ShapeDtypeStruct((m, d), F32),
                   jax.ShapeDtypeStruct((m, LANES), F32), jax.ShapeDtypeStruct((1, LANES), F32)],
        scratch_shapes=[pltpu.VMEM((1, LANES), F32)],
        compiler_params=_cparams(("arbitrary",)),
        name="cross_router",
    )(h1, gc, wq, kvm, wo, gf, w_r, b_r)


def _gather_rows(src_hbm, idx_ref, base, dst, sem, n_rows):
    def issue(g, carry):
        for u in range(GATHER_UNROLL):
            r = g * GATHER_UNROLL + u
            src_row = idx_ref[base + r]
            pltpu.make_async_copy(src_hbm.at[pl.ds(src_row, 1), :], dst.at[pl.ds(r, 1), :],
                                  sem).start(priority=u % 2)
        return carry
    lax.fori_loop(0, n_rows // GATHER_UNROLL, issue, 0)


def _wait_rows(src_hbm, dst, sem, n_rows):
    pltpu.make_async_copy(src_hbm.at[pl.ds(0, n_rows), :], dst, sem).wait()


def _moe_dispatch_kernel(pos1, pos2, x_ref, xs_zero, xs_ref, sem):
    del xs_zero
    i = pl.program_id(0)
    tm = x_ref.shape[0]

    def issue(g, carry):
        for u in range(GATHER_UNROLL):
            r = g * GATHER_UNROLL + u
            src = x_ref.at[pl.ds(r, 1), :]
            pltpu.make_async_copy(src, xs_ref.at[pl.ds(pos1[i * tm + r], 1), :], sem.at[0]).start(priority=0)
            pltpu.make_async_copy(src, xs_ref.at[pl.ds(pos2[i * tm + r], 1), :], sem.at[1]).start(priority=1)
        return carry

    lax.fori_loop(0, tm // GATHER_UNROLL, issue, 0)
    for k in range(2):
        pltpu.make_async_copy(x_ref, xs_ref.at[pl.ds(0, tm), :], sem.at[k]).wait()


def _moe_dispatch(n3p, pos1, pos2, p_max, tm=512):
    m, w = n3p.shape
    grid_spec = pltpu.PrefetchScalarGridSpec(
        num_scalar_prefetch=2,
        grid=(m // tm,),
        in_specs=[pl.BlockSpec((tm, w), lambda i, p1, p2: (i, 0)),
                  pl.BlockSpec(memory_space=pl.ANY)],
        out_specs=pl.BlockSpec(memory_space=pl.ANY),
        scratch_shapes=[pltpu.SemaphoreType.DMA((2,))],
    )
    return pl.pallas_call(
        _moe_dispatch_kernel,
        grid_spec=grid_spec,
        out_shape=jax.ShapeDtypeStruct((p_max, w), n3p.dtype),
        input_output_aliases={3: 0},
        compiler_params=_cparams(("arbitrary",)),
        name="moe_dispatch",
    )(pos1, pos2, n3p, jnp.zeros((p_max, w), n3p.dtype))


def _tile_expert(t, tile_end):
    t = jnp.minimum(t, tile_end[N_EXPERTS - 1] - 1)
    e = jnp.int32(0)
    for k in range(N_EXPERTS - 1):
        e = e + (tile_end[k] <= t).astype(jnp.int32)
    return e


def _moe_expert_kernel(tile_end, x_ref, wg_ref, wu_ref, wd_ref, y_ref, wg_b, wu_b, wd_b):
    t = pl.program_id(0)
    n_valid = tile_end[N_EXPERTS - 1]
    changed = jnp.logical_or(t == 0, _tile_expert(t, tile_end) != _tile_expert(jnp.maximum(t - 1, 0), tile_end))

    @pl.when(changed)
    def _():
        wg_b[...] = wg_ref[...].astype(BF16)
        wu_b[...] = wu_ref[...].astype(BF16)
        wd_b[...] = wd_ref[...].astype(BF16)

    @pl.when(t < n_valid)
    def _():
        x = x_ref[...].astype(BF16)
        hg = _dot(x, wg_b[...])
        hu = _dot(x, wu_b[...])
        hid = (hg * _sigmoid(hg) * hu).astype(BF16)
        y_ref[...] = _dot(hid, wd_b[...])

    @pl.when(t >= n_valid)
    def _():
        y_ref[...] = jnp.zeros_like(y_ref)


def _moe_experts(xs, tile_end, w_eg, w_eu, w_ed):
    p_max, w = xs.shape
    ne, d, f = w_eg.shape
    last = lambda t, te: jnp.minimum(t, te[N_EXPERTS - 1] - 1)
    grid_spec = pltpu.PrefetchScalarGridSpec(
        num_scalar_prefetch=1,
        grid=(p_max // MOE_TILE,),
        in_specs=[pl.BlockSpec((MOE_TILE, w), lambda t, te: (last(t, te), 0)),
                  pl.BlockSpec((None, d, f), lambda t, te: (_tile_expert(t, te), 0, 0)),
                  pl.BlockSpec((None, d, f), lambda t, te: (_tile_expert(t, te), 0, 0)),
                  pl.BlockSpec((None, f, d), lambda t, te: (_tile_expert(t, te), 0, 0))],
        out_specs=pl.BlockSpec((MOE_TILE, d), lambda t, te: (t, 0)),
        scratch_shapes=[pltpu.VMEM((d, f), BF16), pltpu.VMEM((d, f), BF16), pltpu.VMEM((f, d), BF16)],
    )
    return pl.pallas_call(
        _moe_expert_kernel,
        grid_spec=grid_spec,
        out_shape=jax.ShapeDtypeStruct((p_max, d), F32),
        compiler_params=_cparams(("arbitrary",)),
        name="moe_experts",
    )(tile_end, xs, w_eg, w_eu, w_ed)


def _moe_combine_kernel(pos1, pos2, y_hbm, h_ref, info_ref, g_ref, o_ref, buf1, buf2, sem):
    i = pl.program_id(0)
    ni = pl.num_programs(0)
    tm = h_ref.shape[0]
    slot = lax.rem(i, 2)

    def start(step, s):
        _gather_rows(y_hbm, pos1, step * tm, buf1.at[s], sem.at[0, s], tm)
        _gather_rows(y_hbm, pos2, step * tm, buf2.at[s], sem.at[1, s], tm)

    @pl.when(i == 0)
    def _():
        start(0, 0)

    @pl.when(i + 1 < ni)
    def _():
        start(i + 1, 1 - slot)

    _wait_rows(y_hbm, buf1.at[slot], sem.at[0, slot], tm)
    _wait_rows(y_hbm, buf2.at[slot], sem.at[1, slot], tm)
    info = info_ref[...]
    y = h_ref[...] + info[:, 2:3] * buf1[slot] + info[:, 3:4] * buf2[slot]
    o_ref[...] = y * lax.rsqrt(jnp.mean(y * y, axis=-1, keepdims=True) + RMS_EPS) * g_ref[...]


def _moe_combine(ys, pos1, pos2, h2, info, g_final, tm=256):
    m, d = h2.shape
    grid_spec = pltpu.PrefetchScalarGridSpec(
        num_scalar_prefetch=2,
        grid=(m // tm,),
        in_specs=[pl.BlockSpec(memory_space=pl.ANY),
                  pl.BlockSpec((tm, d), lambda i, p1, p2: (i, 0)),
                  pl.BlockSpec((tm, LANES), lambda i, p1, p2: (i, 0)),
                  pl.BlockSpec((1, d), lambda i, p1, p2: (0, 0))],
        out_specs=pl.BlockSpec((tm, d), lambda i, p1, p2: (i, 0)),
        scratch_shapes=[pltpu.VMEM((2, tm, d), F32), pltpu.VMEM((2, tm, d), F32),
                        pltpu.SemaphoreType.DMA((2, 2))],
    )
    return pl.pallas_call(
        _moe_combine_kernel,
        grid_spec=grid_spec,
        out_shape=jax.ShapeDtypeStruct((m, d), F32),
        compiler_params=_cparams(("arbitrary",)),
        name="moe_combine",
    )(pos1, pos2, ys, h2, info, g_final.reshape(1, d))


def _moe_plan_kernel(info_ref, cnt_ref, pos_ref, end_ref):
    cnt = jnp.broadcast_to(cnt_ref[...], (SUBLANES, LANES))
    tiles = jnp.floor((cnt + (MOE_TILE - 1)) * (1.0 / MOE_TILE))
    upto = jnp.where(_iota((LANES, LANES), 0) <= _iota((LANES, LANES), 1), 1.0, 0.0).astype(BF16)
    tile_end = _dot(tiles.astype(BF16), upto)
    first_row = ((tile_end - tiles) * MOE_TILE)[0:1, :]
    info = info_ref[...]
    lane_f = _iota(info.shape, 1).astype(F32)

    def position(e, rank):
        return jnp.sum(jnp.where(lane_f == e, first_row, 0.0), axis=-1, keepdims=True) + rank

    pos1 = position(info[:, 0:1], info[:, 4:5])
    pos2 = position(info[:, 1:2], info[:, 5:6])
    lane = _iota(info.shape, 1)
    pos_ref[...] = jnp.where(lane == 0, pos1, jnp.where(lane == 1, pos2, 0.0)).astype(jnp.int32)
    end_ref[...] = tile_end.astype(jnp.int32)


def _moe_plan(info, counts, tm=1024):
    m = info.shape[0]
    p_max = 2 * m + N_EXPERTS * MOE_TILE
    pos, tile_end = pl.pallas_call(
        _moe_plan_kernel,
        grid=(m // tm,),
        in_specs=[pl.BlockSpec((tm, LANES), lambda i: (i, 0)), pl.BlockSpec((1, LANES), lambda i: (0, 0))],
        out_specs=[pl.BlockSpec((tm, LANES), lambda i: (i, 0)), pl.BlockSpec((SUBLANES, LANES), lambda i: (0, 0))],
        out_shape=[jax.ShapeDtypeStruct((m, LANES), jnp.int32), jax.ShapeDtypeStruct((SUBLANES, LANES), jnp.int32)],
        compiler_params=_cparams(("arbitrary",)),
        name="moe_plan",
    )(info, counts)
    return pos[:, 0], pos[:, 1], tile_end[0, :N_EXPERTS], p_max


def _pad_cols(w, n):
    return jnp.pad(w, ((0, 0), (0, n - w.shape[1])))


def _pad_rows(w, n):
    return jnp.pad(w, ((0, n - w.shape[0]), (0, 0)))


def _split_lora(w):
    o = 0
    parts = []
    for width in (DECAY_LORA, DECAY_LORA, ICLR_LORA, ICLR_LORA):
        parts.append(_pad_cols(w[:, o:o + width], LORA_PAD))
        o += width
    parts.append(w[:, o:o + GATE_LORA])
    return jnp.concatenate(parts, axis=1)


def kernel(x, mem, positions, g_mix, w_in, g_q, w_uq, g_kv, w_ukv, shift_conv, w0_f, w2_f, w0_b, w2_b, a0_f, a2_f, a0_b, a2_b, g2, k_k, k_a, r_k, lnx_g, lnx_b, w_up_mla, w_up_rwkv, w_out, g_cross, g_mem, wq_c, wkv_c, wo_c, g_ffn, w_rg, b_rg, w_re, b_re, w_eg, w_eu, w_ed, g_final):
    bsz, seq, d = x.shape
    m = bsz * seq
    depth = w_in.shape[0]
    h = x.reshape(m, d)
    pos = positions.reshape(m, 1)
    lane = np.arange(LANES)
    invf = jnp.asarray(np.where(lane < ROPE_DIM, 1.0, 0.0), F32) * (
        ROPE_BASE ** (-jnp.asarray(lane % (ROPE_DIM // 2), F32) * (2.0 / ROPE_DIM)))
    invf = invf.reshape(1, LANES)
    assert depth == 1, "the MoE kernel applies the final norm, so it must be the last layer"
    for l in range(depth):
        wi = w_in[l]
        w_mla = _pad_cols(wi[:, :MLA_IN], MLA_IN + (LANES - ROPE_DIM)).astype(BF16)
        rw = wi[:, MLA_IN:MLA_IN + RWKV_IN]
        w_rkv = rw[:, :3 * RWKV_WIDTH].astype(BF16)
        w_lora = _split_lora(rw[:, 3 * RWKV_WIDTH:]).astype(BF16)
        w_gate = wi[:, MLA_IN + RWKV_IN:].astype(BF16)
        sc = shift_conv[l]
        sc_rkv = sc[:, :3 * RWKV_WIDTH]
        sc_lora = _split_lora(sc[:, 3 * RWKV_WIDTH:])
        wq = w_uq[l].reshape(Q_LORA, MLA_HEADS, QK_DIM)
        wq = jnp.pad(wq, ((0, 0), (0, 0), (0, QK_PAD - QK_DIM))).reshape(Q_LORA, MLA_HEADS * QK_PAD)
        wkv = w_ukv[l].reshape(KV_LORA, MLA_HEADS, NOPE_DIM + V_DIM)
        wkv = jnp.concatenate([wkv[:, :, :NOPE_DIM].reshape(KV_LORA, -1),
                               wkv[:, :, NOPE_DIM:].reshape(KV_LORA, -1)], axis=1)
        lora_rows = lambda w: _pad_rows(w, LORA_PAD).astype(BF16)
        w_router = _pad_cols(jnp.concatenate(
            [jnp.moveaxis(w_re[l], 0, 1).reshape(d, N_EXPERTS), w_rg[l]], axis=1), LANES)
        b_router = _pad_cols(jnp.concatenate([b_re[l].reshape(1, N_EXPERTS), b_rg[l].reshape(1, N_GROUPS)],
                                             axis=1), LANES)

        n1 = _rmsnorm(h, g_mix[l])
        z_mla = _matmul(n1, w_mla, F32, 1024, w_mla.shape[1], name="in_proj_mla")
        z_rkv = _matmul(n1, w_rkv, F32, 1024, 512, name="in_proj_rkv")
        z_lora = _matmul(n1, w_lora, F32, 1024, w_lora.shape[1], name="in_proj_lora")
        z_gate = _matmul(n1, w_gate, F32, 1024, 512, name="in_proj_gate")

        q_cat, k_cat, v_mla = _mla_proj(z_mla, pos, invf, g_q[l], g_kv[l], wq.astype(BF16), wkv.astype(BF16))
        o_mla = _mla_attn(q_cat.reshape(bsz, seq, -1), k_cat.reshape(bsz, seq, -1),
                          v_mla.reshape(bsz, seq, -1)).reshape(m, -1)

        (r, v, kk, lw_f, lw_b, al_f, al_b, kd_f, kd_b, gate) = _rwkv_prep(
            z_rkv, z_lora, seq, sc_rkv, sc_lora, lora_rows(w2_f[l]), lora_rows(w2_b[l]),
            lora_rows(a2_f[l]), lora_rows(a2_b[l]), g2[l].astype(BF16),
            w0_f[l], w0_b[l], a0_f[l], a0_b[l], k_k[l], k_a[l])
        sh = lambda t: t.reshape(bsz, seq, RWKV_WIDTH)
        y_f, y_b = _rwkv_scan(sh(r), sh(v), sh(kk), sh(lw_f), sh(lw_b), sh(al_f), sh(al_b),
                              sh(kd_f), sh(kd_b), r_k[l])
        o_rwkv = _rwkv_post(y_f.reshape(m, -1), y_b.reshape(m, -1), gate, lnx_g[l], lnx_b[l])

        merged = _merge(o_mla, w_up_mla[l].astype(BF16), o_rwkv, w_up_rwkv[l].astype(BF16), z_gate)
        h1 = _matmul(merged, w_out[l].astype(BF16), F32, 1024, 512, res=h, name="out_proj")

        mem_n = _rmsnorm(mem.reshape(bsz * MEM_LEN, d), g_mem[l], tm=MEM_LEN)
        kvm = _matmul(mem_n, wkv_c[l].astype(BF16), BF16, bsz * MEM_LEN, 512, name="mem_kv")
        h2, n3, info, counts = _cross_router(h1, seq, g_cross[l], wq_c[l].astype(BF16),
                                             kvm.reshape(bsz, MEM_LEN, 2 * CROSS_WIDTH),
                                             wo_c[l].astype(BF16), g_ffn[l], w_router, b_router)

        pos1, pos2, tile_end, p_max = _moe_plan(info, counts)
        flat = lambda w: w.reshape((N_EXPERTS,) + w.shape[2:])
        xs = _moe_dispatch(n3, pos1, pos2, p_max)
        ys = _moe_experts(xs, tile_end, flat(w_eg[l]), flat(w_eu[l]), flat(w_ed[l]))
        out = _moe_combine(ys, pos1, pos2, h2, info, g_final)
    return out.reshape(bsz, seq, d)
```

```python
import functools

import jax
import jax.numpy as jnp
import numpy as np
from jax import lax
from jax.experimental import pallas as pl
from jax.experimental.pallas import tpu as pltpu

F32 = jnp.float32
BF16 = jnp.bfloat16

D_MODEL = 2048
MEM_LEN = 256
RMS_EPS = 1e-6
MLA_HEADS = 16
Q_LORA = 512
KV_LORA = 256
NOPE_DIM = 128
ROPE_DIM = 64
V_DIM = 128
QK_DIM = NOPE_DIM + ROPE_DIM
ROPE_BASE = 10000.0
RWKV_HEAD = 64
RWKV_HEADS = D_MODEL // RWKV_HEAD
RWKV_WIDTH = D_MODEL
DECAY_LORA = 96
ICLR_LORA = 96
GATE_LORA = 256
LNX_EPS = 64e-5
CROSS_HEADS = 4
CROSS_HEAD_DIM = 128
CROSS_WIDTH = CROSS_HEADS * CROSS_HEAD_DIM
N_GROUPS = 4
EXPERTS_PER_GROUP = 8
N_EXPERTS = N_GROUPS * EXPERTS_PER_GROUP
MLA_IN = Q_LORA + KV_LORA + ROPE_DIM
RWKV_IN = 3 * RWKV_WIDTH + 2 * DECAY_LORA + 2 * ICLR_LORA + GATE_LORA

LANES = 128
SUBLANES = 8
QK_PAD = 256
LORA_PAD = LANES
HEAD_SHIFT = RWKV_HEAD.bit_length() - 1
GROUP_SHIFT = EXPERTS_PER_GROUP.bit_length() - 1
QUAD = 4
QW = QUAD * RWKV_HEAD
CHUNK = 64
ATTN_TK = 1024
MOE_TILE = 512
GATHER_UNROLL = 8
LOG2E = 1.4426950408889634
VMEM_LIMIT = 56 * 1024 * 1024


def _cparams(sem, vmem=VMEM_LIMIT, flags=None):
    return pltpu.CompilerParams(dimension_semantics=sem, vmem_limit_bytes=vmem, flags=flags)


def _iota(shape, dim):
    return lax.broadcasted_iota(jnp.int32, shape, dim)


def _sigmoid(x):
    return 1.0 / (1.0 + jnp.exp(-x))


def _dot(a, b):
    return jnp.dot(a, b, preferred_element_type=F32)


def _dot_nt(a, b):
    return lax.dot_general(a, b, (((1,), (1,)), ((), ())), preferred_element_type=F32)


def _dot_tn(a, b):
    return lax.dot_general(a, b, (((0,), (0,)), ((), ())), preferred_element_type=F32)


def _head_ones():
    r = lax.shift_right_logical(_iota((LANES, LANES), 0), HEAD_SHIFT)
    c = lax.shift_right_logical(_iota((LANES, LANES), 1), HEAD_SHIFT)
    return jnp.where(r == c, 1.0, 0.0).astype(BF16)


def _head_sum(x, ones_bd):
    w = ones_bd.shape[0]
    hi = x.astype(BF16)
    lo = (x - hi.astype(F32)).astype(BF16)
    parts = [_dot(hi[:, j:j + w], ones_bd) + _dot(lo[:, j:j + w], ones_bd) for j in range(0, x.shape[1], w)]
    return parts[0] if len(parts) == 1 else jnp.concatenate(parts, axis=1)


def _rmsnorm_kernel(x_ref, g_ref, o_ref):
    x = x_ref[...]
    y = x * lax.rsqrt(jnp.mean(x * x, axis=-1, keepdims=True) + RMS_EPS)
    o_ref[...] = (y * g_ref[...]).astype(o_ref.dtype)


def _rmsnorm(x, g, tm=512):
    m, d = x.shape
    return pl.pallas_call(
        _rmsnorm_kernel,
        grid=(m // tm,),
        in_specs=[pl.BlockSpec((tm, d), lambda i: (i, 0)),
                  pl.BlockSpec((1, d), lambda i: (0, 0))],
        out_specs=pl.BlockSpec((tm, d), lambda i: (i, 0)),
        out_shape=jax.ShapeDtypeStruct((m, d), BF16),
        compiler_params=_cparams(("parallel",)),
        name="rmsnorm",
    )(x, g.reshape(1, d))


def _mm_kernel(a_ref, b_ref, o_ref):
    o_ref[...] = _dot(a_ref[...], b_ref[...]).astype(o_ref.dtype)


def _mm_res_kernel(a_ref, b_ref, r_ref, o_ref):
    o_ref[...] = (r_ref[...] + _dot(a_ref[...], b_ref[...])).astype(o_ref.dtype)


def _matmul(a, b, out_dtype, tm, tn, res=None, name="matmul"):
    m, k = a.shape
    n = b.shape[1]
    in_specs = [pl.BlockSpec((tm, k), lambda i, j: (i, 0)),
                pl.BlockSpec((k, tn), lambda i, j: (0, j))]
    args = [a, b]
    kern = _mm_kernel
    if res is not None:
        in_specs.append(pl.BlockSpec((tm, tn), lambda i, j: (i, j)))
        args.append(res)
        kern = _mm_res_kernel
    return pl.pallas_call(
        kern,
        grid=(m // tm, n // tn),
        in_specs=in_specs,
        out_specs=pl.BlockSpec((tm, tn), lambda i, j: (i, j)),
        out_shape=jax.ShapeDtypeStruct((m, n), out_dtype),
        compiler_params=_cparams(("parallel", "arbitrary")),
        name=name,
    )(*args)


def _mla_proj_kernel(z_ref, pos_ref, invf_ref, gq_ref, gkv_ref, wq_ref, wkv_ref,
                     q_ref, k_ref, v_ref):
    tm = z_ref.shape[0]
    z = z_ref[...]

    def norm(c, g):
        return (c * lax.rsqrt(jnp.mean(c * c, axis=-1, keepdims=True) + RMS_EPS) * g).astype(BF16)

    cq = norm(z[:, :Q_LORA], gq_ref[...])
    ckv = norm(z[:, Q_LORA:Q_LORA + KV_LORA], gkv_ref[...])
    q = _dot(cq, wq_ref[...]) * (QK_DIM ** -0.5 * LOG2E)
    kv = _dot(ckv, wkv_ref[...])

    ang = pos_ref[...].astype(F32) * invf_ref[...]
    lane = _iota((tm, LANES), 1)
    half = ROPE_DIM // 2
    cos, sin = jnp.cos(ang), jnp.sin(ang)
    c_tab = jnp.where(lane < ROPE_DIM, cos, 0.0)
    s_up = jnp.where((lane >= half) & (lane < ROPE_DIM), sin, 0.0)
    s_dn = jnp.where(lane < half, -sin, 0.0)

    def rope(x):
        return (x * c_tab + pltpu.roll(x, half, 1) * s_up
                + pltpu.roll(x, LANES - half, 1) * s_dn)

    k_r = rope(z[:, Q_LORA + KV_LORA:]).astype(BF16)
    for h in range(MLA_HEADS):
        lo = h * QK_PAD
        q_ref[:, lo:lo + NOPE_DIM] = q[:, lo:lo + NOPE_DIM].astype(BF16)
        q_ref[:, lo + NOPE_DIM:lo + QK_PAD] = rope(q[:, lo + NOPE_DIM:lo + QK_PAD]).astype(BF16)
        k_ref[:, lo:lo + NOPE_DIM] = kv[:, h * NOPE_DIM:(h + 1) * NOPE_DIM].astype(BF16)
        k_ref[:, lo + NOPE_DIM:lo + QK_PAD] = k_r
    v_ref[...] = kv[:, MLA_HEADS * NOPE_DIM:].astype(BF16)


def _mla_proj(z_mla, pos, invf, g_q, g_kv, wq, wkv, tm=256):
    m, w = z_mla.shape
    full = lambda shape: pl.BlockSpec(shape, lambda i: (0, 0))
    row = lambda n: pl.BlockSpec((tm, n), lambda i: (i, 0))
    return pl.pallas_call(
        _mla_proj_kernel,
        grid=(m // tm,),
        in_specs=[row(w), row(1), full((1, LANES)), full((1, Q_LORA)), full((1, KV_LORA)),
                  full(wq.shape), full(wkv.shape)],
        out_specs=[row(MLA_HEADS * QK_PAD), row(MLA_HEADS * QK_PAD), row(MLA_HEADS * V_DIM)],
        out_shape=[jax.ShapeDtypeStruct((m, MLA_HEADS * QK_PAD), BF16),
                   jax.ShapeDtypeStruct((m, MLA_HEADS * QK_PAD), BF16),
                   jax.ShapeDtypeStruct((m, MLA_HEADS * V_DIM), BF16)],
        compiler_params=_cparams(("parallel",)),
        name="mla_proj",
    )(z_mla, pos, invf, g_q.reshape(1, -1), g_kv.reshape(1, -1), wq, wkv)


def _mla_attn_kernel(q_ref, k_ref, v_ref, o_ref):
    tq = q_ref.shape[0]
    nk = k_ref.shape[0] // ATTN_TK
    q = q_ref[...]

    def scores(j):
        return _dot_nt(q, k_ref[j * ATTN_TK:(j + 1) * ATTN_TK, :])

    m = jnp.full((tq, 1), -jnp.inf, F32)
    l = jnp.zeros((tq, 1), F32)
    acc = jnp.zeros((tq, V_DIM), F32)
    s_next = scores(0)
    for j in range(nk):
        s = s_next
        if j + 1 < nk:
            s_next = scores(j + 1)
        m_new = jnp.maximum(m, jnp.max(s, axis=-1, keepdims=True))
        alpha = jnp.exp2(m - m_new)
        p = jnp.exp2(s - m_new)
        l = alpha * l + jnp.sum(p, axis=-1, keepdims=True)
        acc = alpha * acc + _dot(p.astype(BF16), v_ref[j * ATTN_TK:(j + 1) * ATTN_TK, :])
        m = m_new
    o_ref[...] = (acc / l).astype(o_ref.dtype)


def _mla_attn(q, k, v, tq=1024):
    b, s, _ = q.shape
    return pl.pallas_call(
        _mla_attn_kernel,
        grid=(b, MLA_HEADS, s // tq),
        in_specs=[pl.BlockSpec((None, tq, QK_PAD), lambda b_, h, i: (b_, i, h)),
                  pl.BlockSpec((None, s, QK_PAD), lambda b_, h, i: (b_, 0, h)),
                  pl.BlockSpec((None, s, V_DIM), lambda b_, h, i: (b_, 0, h))],
        out_specs=pl.BlockSpec((None, tq, V_DIM), lambda b_, h, i: (b_, i, h)),
        out_shape=jax.ShapeDtypeStruct((b, s, MLA_HEADS * V_DIM), BF16),
        compiler_params=_cparams(("parallel", "parallel", "arbitrary")),
        name="mla_attn",
    )(q, k, v)


def _rwkv_prep_kernel(seq_len,
                      r_ref, rp_ref, rn_ref, k_ref, kp_ref, kn_ref, v_ref, vp_ref, vn_ref,
                      l_ref, lp_ref, ln_ref, scr_ref, sck_ref, scv_ref, scl_ref,
                      w2f_ref, w2b_ref, a2f_ref, a2b_ref, g2_ref,
                      w0f_ref, w0b_ref, a0f_ref, a0b_ref, kk_w_ref,
                      ro_ref, ko_ref, vo_ref, kko_ref, lwf_ref, lwb_ref, alf_ref, alb_ref, g_ref):
    tm = r_ref.shape[0]
    i = pl.program_id(0)
    first = lax.rem(i * tm, seq_len) == 0
    last = lax.rem((i + 1) * tm, seq_len) == 0

    def shift(z_ref, zp_ref, zn_ref, w_ref):
        z = z_ref[...]
        rows = _iota(z.shape, 0)
        prev_row = jnp.where(first, 0.0, zp_ref[SUBLANES - 1:SUBLANES, :])
        next_row = jnp.where(last, 0.0, zn_ref[0:1, :])
        z_prev = jnp.where(rows == 0, prev_row, pltpu.roll(z, 1, 0))
        z_next = jnp.where(rows == tm - 1, next_row, pltpu.roll(z, tm - 1, 0))
        return w_ref[0:1, :] * z_prev + w_ref[1:2, :] * z + w_ref[2:3, :] * z_next

    r = shift(r_ref, rp_ref, rn_ref, scr_ref)
    k = shift(k_ref, kp_ref, kn_ref, sck_ref)
    v = shift(v_ref, vp_ref, vn_ref, scv_ref)
    lo = shift(l_ref, lp_ref, ln_ref, scl_ref)
    xw_f = lo[:, 0 * LORA_PAD:1 * LORA_PAD]
    xw_b = lo[:, 1 * LORA_PAD:2 * LORA_PAD]
    xa_f = lo[:, 2 * LORA_PAD:3 * LORA_PAD]
    xa_b = lo[:, 3 * LORA_PAD:4 * LORA_PAD]
    xg = lo[:, 4 * LORA_PAD:]

    def log_decay(xw, w0_ref, w2_ref):
        y = -(w0_ref[...] + _dot(jnp.tanh(xw).astype(BF16), w2_ref[...]))
        softplus = jnp.maximum(y, 0.0) + jnp.log(1.0 + jnp.exp(-jnp.abs(y)))
        return -jnp.exp(-softplus - 0.5)

    def rate(xa, a0_ref, a2_ref):
        return _sigmoid(a0_ref[...] + _dot(xa.astype(BF16), a2_ref[...]))

    al_f = rate(xa_f, a0f_ref, a2f_ref)
    al_b = rate(xa_b, a0b_ref, a2b_ref)
    kk = k * kk_w_ref[...]
    kk = kk * lax.rsqrt(_head_sum(kk * kk, _head_ones()) + 1e-12)

    ro_ref[...] = r
    ko_ref[...] = k
    vo_ref[...] = v
    kko_ref[...] = kk
    lwf_ref[...] = log_decay(xw_f, w0f_ref, w2f_ref)
    lwb_ref[...] = log_decay(xw_b, w0b_ref, w2b_ref)
    alf_ref[...] = al_f
    alb_ref[...] = al_b
    g_ref[...] = _dot(_sigmoid(xg).astype(BF16), g2_ref[...])


def _rwkv_prep(z_rkv, z_lora, seq_len, sc_rkv, sc_lora, w2f, w2b, a2f, a2b, g2,
               w0f, w0b, a0f, a0b, k_k, tm=512, tw=512):
    m = z_rkv.shape[0]
    wl = z_lora.shape[1]
    nb = RWKV_WIDTH // tw
    rb = tm // SUBLANES
    nrb = m // SUBLANES

    def main(seg):
        return pl.BlockSpec((tm, tw), lambda i, p: (i, seg * nb + p))

    def prev(seg):
        return pl.BlockSpec((SUBLANES, tw),
                            lambda i, p: (jnp.maximum(i * rb - 1, 0), seg * nb + p))

    def nxt(seg):
        return pl.BlockSpec((SUBLANES, tw),
                            lambda i, p: (jnp.minimum((i + 1) * rb, nrb - 1), seg * nb + p))

    def sc(seg):
        return pl.BlockSpec((3, tw), lambda i, p: (0, seg * nb + p))

    colblk = lambda rows: pl.BlockSpec((rows, tw), lambda i, p: (0, p))
    in_specs = []
    for seg in range(3):
        in_specs += [main(seg), prev(seg), nxt(seg)]
    in_specs += [pl.BlockSpec((tm, wl), lambda i, p: (i, 0)),
                 pl.BlockSpec((SUBLANES, wl), lambda i, p: (jnp.maximum(i * rb - 1, 0), 0)),
                 pl.BlockSpec((SUBLANES, wl), lambda i, p: (jnp.minimum((i + 1) * rb, nrb - 1), 0))]
    in_specs += [sc(0), sc(1), sc(2), pl.BlockSpec((3, wl), lambda i, p: (0, 0))]
    in_specs += [colblk(LORA_PAD)] * 4 + [colblk(GATE_LORA)] + [colblk(1)] * 5
    out_spec = pl.BlockSpec((tm, tw), lambda i, p: (i, p))
    n_out = 9
    row = lambda a: a.reshape(1, -1)
    return pl.pallas_call(
        functools.partial(_rwkv_prep_kernel, seq_len),
        grid=(m // tm, nb),
        in_specs=in_specs,
        out_specs=[out_spec] * n_out,
        out_shape=[jax.ShapeDtypeStruct((m, RWKV_WIDTH), F32)] * n_out,
        compiler_params=_cparams(("parallel", "arbitrary")),
        name="rwkv_prep",
    )(z_rkv, z_rkv, z_rkv, z_rkv, z_rkv, z_rkv, z_rkv, z_rkv, z_rkv,
      z_lora, z_lora, z_lora, sc_rkv, sc_rkv, sc_rkv, sc_lora,
      w2f, w2b, a2f, a2b, g2, row(w0f), row(w0b), row(a0f), row(a0b), row(k_k))


def _scan_chunks(chains):
    c = CHUNK
    nc = range(len(chains))
    r, v, kk, lw, al, k, rk, ka, ht, rev = [list(t) for t in zip(*chains)]
    t_i = _iota((c, c), 0)
    s_i = _iota((c, c), 1)
    tri = {False: jnp.where(s_i <= t_i, 1.0, 0.0).astype(BF16),
           True: jnp.where(s_i >= t_i, 1.0, 0.0).astype(BF16)}
    row = _iota((c, QW), 0)
    col = jnp.bitwise_and(_iota((c, QW), 1), RWKV_HEAD - 1)
    strict = {False: col < row, True: col > row}
    incl = {False: col <= row, True: col >= row}
    eye = jnp.where(row == col, 1.0, 0.0)
    lane_head = lax.shift_right_logical(_iota((c, QW), 1), HEAD_SHIFT)
    head_mask = [lane_head == h for h in range(QUAD)]
    on_diag = (lax.shift_right_logical(_iota((QW, QW), 0), HEAD_SHIFT)
               == lax.shift_right_logical(_iota((QW, QW), 1), HEAD_SHIFT))

    def blockdiag(x):
        xb = x.astype(BF16)
        zero = jnp.zeros_like(xb)
        return jnp.concatenate([jnp.where(head_mask[h], xb, zero) for h in range(QUAD)], axis=0)

    def cat(x, y, axis=0):
        return jnp.concatenate([x, y], axis=axis)

    a = [-kk[i] for i in nc]
    b = [kk[i] * al[i] for i in nc]
    kd = [k[i] * (1.0 + (al[i] - 1.0) * ka[i]) for i in nc]
    lw_hi = [lw[i].astype(BF16) for i in nc]
    lw_lo = [(lw[i] - lw_hi[i].astype(F32)).astype(BF16) for i in nc]
    cum = [_dot(tri[rev[i]], lw_hi[i]) + _dot(tri[rev[i]], lw_lo[i]) for i in nc]
    cum_prev = [cum[i] - lw[i] for i in nc]
    ref_row = [c // 2 if rev[i] else c // 2 - 1 for i in nc]
    tot_row = [0 if rev[i] else c - 1 for i in nc]
    c_ref = [cum[i][ref_row[i]:ref_row[i] + 1, :] for i in nc]
    c_tot = [cum[i][tot_row[i]:tot_row[i] + 1, :] for i in nc]
    e_inv = [jnp.exp(c_ref[i] - cum[i]) for i in nc]
    e_out = [jnp.exp(c_tot[i] - cum[i]) for i in nc]
    e_cur = [jnp.exp(cum[i] - c_ref[i]) for i in nc]
    lhs = [jnp.concatenate([a[i] * jnp.exp(cum_prev[i] - c_ref[i]), r[i] * e_cur[i], r[i] * rk[i] * e_cur[i]],
                           axis=0).astype(BF16) for i in nc]
    rhs = [cat(blockdiag(b[i] * e_inv[i]), blockdiag(kd[i] * e_inv[i])) for i in nc]
    nmat = [_dot_nt(lhs[i], rhs[i]) for i in nc]
    n_ab = [jnp.where(strict[rev[i]], nmat[i][:c, :QW], 0.0) for i in nc]
    n_ak = [jnp.where(strict[rev[i]], nmat[i][:c, QW:], 0.0) for i in nc]
    n_rb = [jnp.where(incl[rev[i]], nmat[i][c:2 * c, :QW], 0.0).astype(BF16) for i in nc]
    n_rk = [jnp.where(incl[rev[i]], nmat[i][c:2 * c, QW:], 0.0) + jnp.where(row == col, nmat[i][2 * c:, QW:], 0.0)
            for i in nc]

    n_sq = int(np.log2(c)) - 1
    pw = n_ab
    tinv = [eye + pw[i] for i in nc]
    pw = [_dot(pw[i].astype(BF16), blockdiag(pw[i])) for i in nc]
    for j in range(n_sq):
        if j + 1 < n_sq:
            both = [_dot(cat(tinv[i], pw[i]).astype(BF16), blockdiag(pw[i])) for i in nc]
            tinv = [tinv[i] + both[i][:c] for i in nc]
            pw = [both[i][c:] for i in nc]
        else:
            tinv = [tinv[i] + _dot(tinv[i].astype(BF16), blockdiag(pw[i])) for i in nc]

    v_bd = [blockdiag(v[i]) for i in nc]
    state_lhs = [cat(a[i] * jnp.exp(cum_prev[i]), r[i] * jnp.exp(cum[i])).astype(BF16) for i in nc]
    from_state = [_dot_nt(state_lhs[i], ht[i].astype(BF16)) for i in nc]
    from_v = [_dot(cat(n_ak[i], n_rk[i]).astype(BF16), v_bd[i]) for i in nc]
    u = [_dot(tinv[i].astype(BF16), blockdiag(from_state[i][:c] + from_v[i][:c])) for i in nc]
    o = [from_state[i][c:] + from_v[i][c:] + _dot(n_rb[i], blockdiag(u[i])) for i in nc]
    u_b = [u[i].astype(BF16) for i in nc]
    v_b = [v[i].astype(BF16) for i in nc]
    upd = [_dot_tn(u_b[i], (b[i] * e_out[i]).astype(BF16)) + _dot_tn(v_b[i], (kd[i] * e_out[i]).astype(BF16))
           for i in nc]
    ht_new = [ht[i] * jnp.exp(c_tot[i]) + jnp.where(on_diag, upd[i], 0.0) for i in nc]
    return [(o[i], ht_new[i]) for i in nc]


def _rwkv_scan_kernel(rf, vf, kkf, lwf, alf, kf, rb, vb, kkb, lwb, alb, kb, rk_ref, ka_ref,
                      yf_ref, yb_ref, h_ref):
    @pl.when(pl.program_id(2) == 0)
    def _():
        h_ref[...] = jnp.zeros_like(h_ref)

    n_chunks = rf.shape[0] // CHUNK
    n_groups = rf.shape[1] // QW

    def body(ci, carry):
        chains = []
        for d, (refs, y_ref) in enumerate((((rf, vf, kkf, lwf, alf, kf), yf_ref),
                                           ((rb, vb, kkb, lwb, alb, kb), yb_ref))):
            cidx = ci if d == 0 else n_chunks - 1 - ci
            rows = pl.ds(pl.multiple_of(cidx * CHUNK, CHUNK), CHUNK)
            for g in range(n_groups):
                lanes = slice(g * QW, (g + 1) * QW)
                vals = [x[rows, lanes] for x in refs] + [rk_ref[:, lanes], ka_ref[:, lanes], h_ref[d, g]]
                chains.append((d, g, y_ref, rows, lanes, vals))
        results = _scan_chunks([tuple(vals) + (d == 1,) for d, _, _, _, _, vals in chains])
        for (d, g, y_ref, rows, lanes, _), (y, ht) in zip(chains, results):
            y_ref[rows, lanes] = y
            h_ref[d, g] = ht
        return carry

    lax.fori_loop(0, n_chunks, body, 0)


def _rwkv_scan(r, k, v, kk, lw_f, lw_b, al_f, al_b, r_k, k_a, tb=128, groups=8):
    b, s, w = r.shape
    nblk = s // tb
    wb = groups * QW
    fwd = pl.BlockSpec((None, tb, wb), lambda b_, p, j: (b_, j, p))
    bwd = pl.BlockSpec((None, tb, wb), lambda b_, p, j: (b_, nblk - 1 - j, p))
    vec = pl.BlockSpec((1, wb), lambda b_, p, j: (0, p))
    return pl.pallas_call(
        _rwkv_scan_kernel,
        grid=(b, w // wb, nblk),
        in_specs=[fwd] * 6 + [bwd] * 6 + [vec, vec],
        out_specs=[fwd, bwd],
        out_shape=[jax.ShapeDtypeStruct((b, s, w), F32)] * 2,
        scratch_shapes=[pltpu.VMEM((2, groups, QW, QW), F32)],
        compiler_params=_cparams(("parallel", "parallel", "arbitrary")),
        name="rwkv_scan",
    )(r, v, kk, lw_f, al_f, k, r, v, kk, lw_b, al_b, k, r_k.reshape(1, -1), k_a.reshape(1, -1))


def _rwkv_post_kernel(yf_ref, yb_ref, g_ref, lg_ref, lb_ref, o_ref):
    ones_bd = _head_ones()
    y = yf_ref[...] + yb_ref[...]
    mu = _head_sum(y, ones_bd) * (1.0 / RWKV_HEAD)
    d = y - mu
    var = _head_sum(d * d, ones_bd) * (1.0 / RWKV_HEAD)
    yn = d * lax.rsqrt(var + LNX_EPS) * lg_ref[...] + lb_ref[...]
    o_ref[...] = (yn * g_ref[...]).astype(o_ref.dtype)


def _rwkv_post(y_f, y_b, g, lnx_g, lnx_b, tm=512, tw=512):
    m, w = y_f.shape
    blk = pl.BlockSpec((tm, tw), lambda i, p: (i, p))
    vec = pl.BlockSpec((1, tw), lambda i, p: (0, p))
    return pl.pallas_call(
        _rwkv_post_kernel,
        grid=(m // tm, w // tw),
        in_specs=[blk, blk, blk, vec, vec],
        out_specs=blk,
        out_shape=jax.ShapeDtypeStruct((m, w), BF16),
        compiler_params=_cparams(("parallel", "parallel")),
        name="rwkv_post",
    )(y_f, y_b, g, lnx_g.reshape(1, -1), lnx_b.reshape(1, -1))


def _merge_kernel(a1_ref, w1_ref, a2_ref, w2_ref, g1_ref, g2_ref, o_ref):
    m1 = _dot(a1_ref[...], w1_ref[...])
    m2 = _dot(a2_ref[...], w2_ref[...])
    o_ref[...] = (_sigmoid(g1_ref[...]) * m1 + _sigmoid(g2_ref[...]) * m2).astype(o_ref.dtype)


def _merge(o_mla, w_up_mla, o_rwkv, w_up_rwkv, z_gate, tm=1024, tn=512):
    m, k = o_mla.shape
    n = w_up_mla.shape[1]
    nj = n // tn
    a_spec = pl.BlockSpec((tm, k), lambda i, j: (i, 0))
    w_spec = pl.BlockSpec((k, tn), lambda i, j: (0, j))
    return pl.pallas_call(
        _merge_kernel,
        grid=(m // tm, nj),
        in_specs=[a_spec, w_spec, a_spec, w_spec,
                  pl.BlockSpec((tm, tn), lambda i, j: (i, j)),
                  pl.BlockSpec((tm, tn), lambda i, j: (i, nj + j))],
        out_specs=pl.BlockSpec((tm, tn), lambda i, j: (i, j)),
        out_shape=jax.ShapeDtypeStruct((m, n), BF16),
        compiler_params=_cparams(("parallel", "arbitrary")),
        name="merge",
    )(o_mla, w_up_mla, o_rwkv, w_up_rwkv, z_gate, z_gate)


def _cross_router_kernel(h_ref, gc_ref, wq_ref, kv_ref, wo_ref, gf_ref, wr_ref, br_ref,
                         h2_ref, n3_ref, info_ref, cnt_ref, carry_ref):
    @pl.when(pl.program_id(0) == 0)
    def _():
        carry_ref[...] = jnp.zeros_like(carry_ref)

    h = h_ref[...]
    hn = (h * lax.rsqrt(jnp.mean(h * h, axis=-1, keepdims=True) + RMS_EPS) * gc_ref[...]).astype(BF16)
    q = (_dot(hn, wq_ref[...]) * (CROSS_HEAD_DIM ** -0.5)).astype(BF16)
    kv = kv_ref[...]
    outs = []
    for hd in range(CROSS_HEADS):
        lo = hd * CROSS_HEAD_DIM
        s = _dot_nt(q[:, lo:lo + CROSS_HEAD_DIM], kv[:, lo:lo + CROSS_HEAD_DIM])
        p = jnp.exp(s - jnp.max(s, axis=-1, keepdims=True))
        p = p / jnp.sum(p, axis=-1, keepdims=True)
        outs.append(_dot(p.astype(BF16), kv[:, CROSS_WIDTH + lo:CROSS_WIDTH + lo + CROSS_HEAD_DIM]))
    o = jnp.concatenate(outs, axis=-1).astype(BF16)
    h2 = h + _dot(o, wo_ref[...])
    h2_ref[...] = h2

    n3 = h2 * lax.rsqrt(jnp.mean(h2 * h2, axis=-1, keepdims=True) + RMS_EPS) * gf_ref[...]
    n3_ref[...] = n3

    wr = wr_ref[...]
    n_hi, w_hi = n3.astype(BF16), wr.astype(BF16)
    n_lo = (n3 - n_hi.astype(F32)).astype(BF16)
    w_lo = (wr - w_hi.astype(F32)).astype(BF16)
    logits = _dot(n_hi, w_hi) + (_dot(n_hi, w_lo) + _dot(n_lo, w_hi)) + br_ref[...]
    lane = _iota(logits.shape, 1)
    lane_f = lane.astype(F32)
    neg = jnp.float32(-jnp.inf)
    big = jnp.float32(1e9)

    def masked_softmax(mask):
        x = jnp.where(mask, logits, neg)
        e = jnp.exp(x - jnp.max(x, axis=-1, keepdims=True))
        return e / jnp.sum(e, axis=-1, keepdims=True)

    def top1(prob, mask):
        pmax = jnp.max(jnp.where(mask, prob, -1.0), axis=-1, keepdims=True)
        idx = jnp.min(jnp.where(mask & (prob == pmax), lane_f, big), axis=-1, keepdims=True)
        return pmax, idx

    g_mask = (lane >= N_EXPERTS) & (lane < N_EXPERTS + N_GROUPS)
    p_group, g_idx = top1(masked_softmax(g_mask), g_mask)
    g_sel = g_idx - float(N_EXPERTS)
    e_mask = (lane < N_EXPERTS) & (lax.shift_right_logical(lane, GROUP_SHIFT).astype(F32) == g_sel)
    e_prob = masked_softmax(e_mask)
    p1, i1 = top1(e_prob, e_mask)
    rest = e_mask & (lane_f != i1)
    p2, i2 = top1(e_prob, rest)
    denom = p1 + p2
    w1 = p_group * (p1 / denom)
    w2 = p_group * (p2 / denom)

    tm = logits.shape[0]
    oh1 = jnp.where(lane_f == i1, 1.0, 0.0)
    oh2 = jnp.where(lane_f == i2, 1.0, 0.0)
    before = jnp.where(_iota((tm, tm), 1) < _iota((tm, tm), 0), 1.0, 0.0).astype(BF16)
    carry = carry_ref[...]
    cnt1 = jnp.sum(oh1, axis=0, keepdims=True)
    cnt2 = jnp.sum(oh2, axis=0, keepdims=True)
    rank1 = jnp.sum(oh1 * (carry + _dot(before, oh1.astype(BF16))), axis=-1, keepdims=True)
    rank2 = jnp.sum(oh2 * (carry + cnt1 + _dot(before, oh2.astype(BF16))), axis=-1, keepdims=True)
    carry = carry + cnt1 + cnt2
    carry_ref[...] = carry
    cnt_ref[...] = carry
    info = jnp.zeros_like(logits)
    for k, val in enumerate((i1, i2, w1, w2, rank1, rank2)):
        info = jnp.where(lane == k, val, info)
    info_ref[...] = info


def _cross_router(h1, seq_len, g_cross, wq, kvm, wo, g_ffn, w_r, b_r, tm=512):
    m, d = h1.shape
    full = lambda a: pl.BlockSpec(a.shape, lambda i: (0,) * a.ndim)
    row = lambda n: pl.BlockSpec((tm, n), lambda i: (i, 0))
    gc, gf = g_cross.reshape(1, d), g_ffn.reshape(1, d)
    per_seq = seq_len // tm
    return pl.pallas_call(
        _cross_router_kernel,
        grid=(m // tm,),
        in_specs=[row(d), full(gc), full(wq),
                  pl.BlockSpec((None,) + kvm.shape[1:], lambda i: (i // per_seq, 0, 0)),
                  full(wo), full(gf), full(w_r), full(b_r)],
        out_specs=[row(d), row(d), row(LANES), pl.BlockSpec((1, LANES), lambda i: (0, 0))],
        out_shape=[jax.ShapeDtypeStruct((m, d), F32), jax.ShapeDtypeStruct((m, d), F32),
                   jax.ShapeDtypeStruct((m, LANES), F32), jax.ShapeDtypeStruct((1, LANES), F32)],
        scratch_shapes=[pltpu.VMEM((1, LANES), F32)],
        compiler_params=_cparams(("arbitrary",)),
        name="cross_router",
    )(h1, gc, wq, kvm, wo, gf, w_r, b_r)


def _gather_rows(src_hbm, idx_ref, base, dst, sem, n_rows):
    def issue(g, carry):
        for u in range(GATHER_UNROLL):
            r = g * GATHER_UNROLL + u
            src_row = idx_ref[base + r]
            pltpu.make_async_copy(src_hbm.at[pl.ds(src_row, 1), :], dst.at[pl.ds(r, 1), :], sem).start()
        return carry
    lax.fori_loop(0, n_rows // GATHER_UNROLL, issue, 0)


def _wait_rows(src_hbm, dst, sem, n_rows):
    pltpu.make_async_copy(src_hbm.at[pl.ds(0, n_rows), :], dst, sem).wait()


def _moe_dispatch_kernel(pos1, pos2, x_ref, xs_zero, xs_ref, sem):
    del xs_zero
    i = pl.program_id(0)
    tm = x_ref.shape[0]

    def issue(g, carry):
        for u in range(GATHER_UNROLL):
            r = g * GATHER_UNROLL + u
            src = x_ref.at[pl.ds(r, 1), :]
            pltpu.make_async_copy(src, xs_ref.at[pl.ds(pos1[i * tm + r], 1), :], sem.at[0]).start()
            pltpu.make_async_copy(src, xs_ref.at[pl.ds(pos2[i * tm + r], 1), :], sem.at[1]).start()
        return carry

    lax.fori_loop(0, tm // GATHER_UNROLL, issue, 0)
    for k in range(2):
        pltpu.make_async_copy(x_ref, xs_ref.at[pl.ds(0, tm), :], sem.at[k]).wait()


def _moe_dispatch(n3p, pos1, pos2, p_max, tm=512):
    m, w = n3p.shape
    grid_spec = pltpu.PrefetchScalarGridSpec(
        num_scalar_prefetch=2,
        grid=(m // tm,),
        in_specs=[pl.BlockSpec((tm, w), lambda i, p1, p2: (i, 0)),
                  pl.BlockSpec(memory_space=pl.ANY)],
        out_specs=pl.BlockSpec(memory_space=pl.ANY),
        scratch_shapes=[pltpu.SemaphoreType.DMA((2,))],
    )
    return pl.pallas_call(
        _moe_dispatch_kernel,
        grid_spec=grid_spec,
        out_shape=jax.ShapeDtypeStruct((p_max, w), n3p.dtype),
        input_output_aliases={3: 0},
        compiler_params=_cparams(("arbitrary",)),
        name="moe_dispatch",
    )(pos1, pos2, n3p, jnp.zeros((p_max, w), n3p.dtype))


def _tile_expert(t, tile_end):
    t = jnp.minimum(t, tile_end[N_EXPERTS - 1] - 1)
    e = jnp.int32(0)
    for k in range(N_EXPERTS - 1):
        e = e + (tile_end[k] <= t).astype(jnp.int32)
    return e


def _moe_expert_kernel(tile_end, x_ref, wg_ref, wu_ref, wd_ref, y_ref, wg_b, wu_b, wd_b):
    t = pl.program_id(0)
    n_valid = tile_end[N_EXPERTS - 1]
    changed = jnp.logical_or(t == 0, _tile_expert(t, tile_end) != _tile_expert(jnp.maximum(t - 1, 0), tile_end))

    @pl.when(changed)
    def _():
        wg_b[...] = wg_ref[...].astype(BF16)
        wu_b[...] = wu_ref[...].astype(BF16)
        wd_b[...] = wd_ref[...].astype(BF16)

    @pl.when(t < n_valid)
    def _():
        x = x_ref[...].astype(BF16)
        hg = _dot(x, wg_b[...])
        hu = _dot(x, wu_b[...])
        hid = (hg * _sigmoid(hg) * hu).astype(BF16)
        y_ref[...] = _dot(hid, wd_b[...])

    @pl.when(t >= n_valid)
    def _():
        y_ref[...] = jnp.zeros_like(y_ref)


def _moe_experts(xs, tile_end, w_eg, w_eu, w_ed):
    p_max, w = xs.shape
    ne, d, f = w_eg.shape
    last = lambda t, te: jnp.minimum(t, te[N_EXPERTS - 1] - 1)
    grid_spec = pltpu.PrefetchScalarGridSpec(
        num_scalar_prefetch=1,
        grid=(p_max // MOE_TILE,),
        in_specs=[pl.BlockSpec((MOE_TILE, w), lambda t, te: (last(t, te), 0)),
                  pl.BlockSpec((None, d, f), lambda t, te: (_tile_expert(t, te), 0, 0)),
                  pl.BlockSpec((None, d, f), lambda t, te: (_tile_expert(t, te), 0, 0)),
                  pl.BlockSpec((None, f, d), lambda t, te: (_tile_expert(t, te), 0, 0))],
        out_specs=pl.BlockSpec((MOE_TILE, d), lambda t, te: (t, 0)),
        scratch_shapes=[pltpu.VMEM((d, f), BF16), pltpu.VMEM((d, f), BF16), pltpu.VMEM((f, d), BF16)],
    )
    return pl.pallas_call(
        _moe_expert_kernel,
        grid_spec=grid_spec,
        out_shape=jax.ShapeDtypeStruct((p_max, d), F32),
        compiler_params=_cparams(("arbitrary",)),
        name="moe_experts",
    )(tile_end, xs, w_eg, w_eu, w_ed)


def _moe_combine_kernel(pos1, pos2, y_hbm, h_ref, info_ref, g_ref, o_ref, buf1, buf2, sem):
    i = pl.program_id(0)
    ni = pl.num_programs(0)
    tm = h_ref.shape[0]
    slot = lax.rem(i, 2)

    def start(step, s):
        _gather_rows(y_hbm, pos1, step * tm, buf1.at[s], sem.at[0, s], tm)
        _gather_rows(y_hbm, pos2, step * tm, buf2.at[s], sem.at[1, s], tm)

    @pl.when(i == 0)
    def _():
        start(0, 0)

    @pl.when(i + 1 < ni)
    def _():
        start(i + 1, 1 - slot)

    _wait_rows(y_hbm, buf1.at[slot], sem.at[0, slot], tm)
    _wait_rows(y_hbm, buf2.at[slot], sem.at[1, slot], tm)
    info = info_ref[...]
    y = h_ref[...] + info[:, 2:3] * buf1[slot] + info[:, 3:4] * buf2[slot]
    o_ref[...] = y * lax.rsqrt(jnp.mean(y * y, axis=-1, keepdims=True) + RMS_EPS) * g_ref[...]


def _moe_combine(ys, pos1, pos2, h2, info, g_final, tm=256):
    m, d = h2.shape
    grid_spec = pltpu.PrefetchScalarGridSpec(
        num_scalar_prefetch=2,
        grid=(m // tm,),
        in_specs=[pl.BlockSpec(memory_space=pl.ANY),
                  pl.BlockSpec((tm, d), lambda i, p1, p2: (i, 0)),
                  pl.BlockSpec((tm, LANES), lambda i, p1, p2: (i, 0)),
                  pl.BlockSpec((1, d), lambda i, p1, p2: (0, 0))],
        out_specs=pl.BlockSpec((tm, d), lambda i, p1, p2: (i, 0)),
        scratch_shapes=[pltpu.VMEM((2, tm, d), F32), pltpu.VMEM((2, tm, d), F32),
                        pltpu.SemaphoreType.DMA((2, 2))],
    )
    return pl.pallas_call(
        _moe_combine_kernel,
        grid_spec=grid_spec,
        out_shape=jax.ShapeDtypeStruct((m, d), F32),
        compiler_params=_cparams(("arbitrary",)),
        name="moe_combine",
    )(pos1, pos2, ys, h2, info, g_final.reshape(1, d))


def _moe_plan_kernel(info_ref, cnt_ref, pos_ref, end_ref):
    cnt = jnp.broadcast_to(cnt_ref[...], (SUBLANES, LANES))
    tiles = jnp.floor((cnt + (MOE_TILE - 1)) * (1.0 / MOE_TILE))
    upto = jnp.where(_iota((LANES, LANES), 0) <= _iota((LANES, LANES), 1), 1.0, 0.0).astype(BF16)
    tile_end = _dot(tiles.astype(BF16), upto)
    first_row = ((tile_end - tiles) * MOE_TILE)[0:1, :]
    info = info_ref[...]
    lane_f = _iota(info.shape, 1).astype(F32)

    def position(e, rank):
        return jnp.sum(jnp.where(lane_f == e, first_row, 0.0), axis=-1, keepdims=True) + rank

    pos1 = position(info[:, 0:1], info[:, 4:5])
    pos2 = position(info[:, 1:2], info[:, 5:6])
    lane = _iota(info.shape, 1)
    pos_ref[...] = jnp.where(lane == 0, pos1, jnp.where(lane == 1, pos2, 0.0)).astype(jnp.int32)
    end_ref[...] = tile_end.astype(jnp.int32)


def _moe_plan(info, counts, tm=1024):
    m = info.shape[0]
    p_max = 2 * m + N_EXPERTS * MOE_TILE
    pos, tile_end = pl.pallas_call(
        _moe_plan_kernel,
        grid=(m // tm,),
        in_specs=[pl.BlockSpec((tm, LANES), lambda i: (i, 0)), pl.BlockSpec((1, LANES), lambda i: (0, 0))],
        out_specs=[pl.BlockSpec((tm, LANES), lambda i: (i, 0)), pl.BlockSpec((SUBLANES, LANES), lambda i: (0, 0))],
        out_shape=[jax.ShapeDtypeStruct((m, LANES), jnp.int32), jax.ShapeDtypeStruct((SUBLANES, LANES), jnp.int32)],
        compiler_params=_cparams(("arbitrary",)),
        name="moe_plan",
    )(info, counts)
    return pos[:, 0], pos[:, 1], tile_end[0, :N_EXPERTS], p_max


def _pad_cols(w, n):
    return jnp.pad(w, ((0, 0), (0, n - w.shape[1])))


def _pad_rows(w, n):
    return jnp.pad(w, ((0, n - w.shape[0]), (0, 0)))


def _split_lora(w):
    o = 0
    parts = []
    for width in (DECAY_LORA, DECAY_LORA, ICLR_LORA, ICLR_LORA):
        parts.append(_pad_cols(w[:, o:o + width], LORA_PAD))
        o += width
    parts.append(w[:, o:o + GATE_LORA])
    return jnp.concatenate(parts, axis=1)


def kernel(x, mem, positions, g_mix, w_in, g_q, w_uq, g_kv, w_ukv, shift_conv, w0_f, w2_f, w0_b, w2_b, a0_f, a2_f, a0_b, a2_b, g2, k_k, k_a, r_k, lnx_g, lnx_b, w_up_mla, w_up_rwkv, w_out, g_cross, g_mem, wq_c, wkv_c, wo_c, g_ffn, w_rg, b_rg, w_re, b_re, w_eg, w_eu, w_ed, g_final):
    bsz, seq, d = x.shape
    m = bsz * seq
    depth = w_in.shape[0]
    h = x.reshape(m, d)
    pos = positions.reshape(m, 1)
    lane = np.arange(LANES)
    invf = jnp.asarray(np.where(lane < ROPE_DIM, 1.0, 0.0), F32) * (
        ROPE_BASE ** (-jnp.asarray(lane % (ROPE_DIM // 2), F32) * (2.0 / ROPE_DIM)))
    invf = invf.reshape(1, LANES)
    assert depth == 1, "the MoE kernel applies the final norm, so it must be the last layer"
    for l in range(depth):
        wi = w_in[l]
        w_mla = _pad_cols(wi[:, :MLA_IN], MLA_IN + (LANES - ROPE_DIM)).astype(BF16)
        rw = wi[:, MLA_IN:MLA_IN + RWKV_IN]
        w_rkv = rw[:, :3 * RWKV_WIDTH].astype(BF16)
        w_lora = _split_lora(rw[:, 3 * RWKV_WIDTH:]).astype(BF16)
        w_gate = wi[:, MLA_IN + RWKV_IN:].astype(BF16)
        sc = shift_conv[l]
        sc_rkv = sc[:, :3 * RWKV_WIDTH]
        sc_lora = _split_lora(sc[:, 3 * RWKV_WIDTH:])
        wq = w_uq[l].reshape(Q_LORA, MLA_HEADS, QK_DIM)
        wq = jnp.pad(wq, ((0, 0), (0, 0), (0, QK_PAD - QK_DIM))).reshape(Q_LORA, MLA_HEADS * QK_PAD)
        wkv = w_ukv[l].reshape(KV_LORA, MLA_HEADS, NOPE_DIM + V_DIM)
        wkv = jnp.concatenate([wkv[:, :, :NOPE_DIM].reshape(KV_LORA, -1),
                               wkv[:, :, NOPE_DIM:].reshape(KV_LORA, -1)], axis=1)
        lora_rows = lambda w: _pad_rows(w, LORA_PAD).astype(BF16)
        w_router = _pad_cols(jnp.concatenate(
            [jnp.moveaxis(w_re[l], 0, 1).reshape(d, N_EXPERTS), w_rg[l]], axis=1), LANES)
        b_router = _pad_cols(jnp.concatenate([b_re[l].reshape(1, N_EXPERTS), b_rg[l].reshape(1, N_GROUPS)],
                                             axis=1), LANES)

        n1 = _rmsnorm(h, g_mix[l])
        z_mla = _matmul(n1, w_mla, F32, 1024, w_mla.shape[1], name="in_proj_mla")
        z_rkv = _matmul(n1, w_rkv, F32, 1024, 512, name="in_proj_rkv")
        z_lora = _matmul(n1, w_lora, F32, 1024, w_lora.shape[1], name="in_proj_lora")
        z_gate = _matmul(n1, w_gate, F32, 1024, 512, name="in_proj_gate")

        q_cat, k_cat, v_mla = _mla_proj(z_mla, pos, invf, g_q[l], g_kv[l], wq.astype(BF16), wkv.astype(BF16))
        o_mla = _mla_attn(q_cat.reshape(bsz, seq, -1), k_cat.reshape(bsz, seq, -1),
                          v_mla.reshape(bsz, seq, -1)).reshape(m, -1)

        (r, k, v, kk, lw_f, lw_b, al_f, al_b, gate) = _rwkv_prep(
            z_rkv, z_lora, seq, sc_rkv, sc_lora, lora_rows(w2_f[l]), lora_rows(w2_b[l]),
            lora_rows(a2_f[l]), lora_rows(a2_b[l]), g2[l].astype(BF16),
            w0_f[l], w0_b[l], a0_f[l], a0_b[l], k_k[l])
        sh = lambda t: t.reshape(bsz, seq, RWKV_WIDTH)
        y_f, y_b = _rwkv_scan(sh(r), sh(k), sh(v), sh(kk), sh(lw_f), sh(lw_b), sh(al_f), sh(al_b),
                              r_k[l], k_a[l])
        o_rwkv = _rwkv_post(y_f.reshape(m, -1), y_b.reshape(m, -1), gate, lnx_g[l], lnx_b[l])

        merged = _merge(o_mla, w_up_mla[l].astype(BF16), o_rwkv, w_up_rwkv[l].astype(BF16), z_gate)
        h1 = _matmul(merged, w_out[l].astype(BF16), F32, 1024, 512, res=h, name="out_proj")

        mem_n = _rmsnorm(mem.reshape(bsz * MEM_LEN, d), g_mem[l], tm=MEM_LEN)
        kvm = _matmul(mem_n, wkv_c[l].astype(BF16), BF16, bsz * MEM_LEN, 512, name="mem_kv")
        h2, n3, info, counts = _cross_router(h1, seq, g_cross[l], wq_c[l].astype(BF16),
                                             kvm.reshape(bsz, MEM_LEN, 2 * CROSS_WIDTH),
                                             wo_c[l].astype(BF16), g_ffn[l], w_router, b_router)

        pos1, pos2, tile_end, p_max = _moe_plan(info, counts)
        flat = lambda w: w.reshape((N_EXPERTS,) + w.shape[2:])
        xs = _moe_dispatch(n3, pos1, pos2, p_max)
        ys = _moe_experts(xs, tile_end, flat(w_eg[l]), flat(w_eu[l]), flat(w_ed[l]))
        out = _moe_combine(ys, pos1, pos2, h2, info, g_final)
    return out.reshape(bsz, seq, d)
```

```python
import functools

import jax
import jax.numpy as jnp
import numpy as np
from jax import lax
from jax.experimental import pallas as pl
from jax.experimental.pallas import tpu as pltpu

F32 = jnp.float32
BF16 = jnp.bfloat16

D_MODEL = 2048
MEM_LEN = 256
RMS_EPS = 1e-6
MLA_HEADS = 16
Q_LORA = 512
KV_LORA = 256
NOPE_DIM = 128
ROPE_DIM = 64
V_DIM = 128
QK_DIM = NOPE_DIM + ROPE_DIM
ROPE_BASE = 10000.0
RWKV_HEAD = 64
RWKV_HEADS = D_MODEL // RWKV_HEAD
RWKV_WIDTH = D_MODEL
DECAY_LORA = 96
ICLR_LORA = 96
GATE_LORA = 256
LNX_EPS = 64e-5
CROSS_HEADS = 4
CROSS_HEAD_DIM = 128
CROSS_WIDTH = CROSS_HEADS * CROSS_HEAD_DIM
N_GROUPS = 4
EXPERTS_PER_GROUP = 8
N_EXPERTS = N_GROUPS * EXPERTS_PER_GROUP
MLA_IN = Q_LORA + KV_LORA + ROPE_DIM
RWKV_IN = 3 * RWKV_WIDTH + 2 * DECAY_LORA + 2 * ICLR_LORA + GATE_LORA

LANES = 128
SUBLANES = 8
QK_PAD = 256
LORA_PAD = LANES
HEAD_SHIFT = RWKV_HEAD.bit_length() - 1
GROUP_SHIFT = EXPERTS_PER_GROUP.bit_length() - 1
QUAD = 4
QW = QUAD * RWKV_HEAD
CHUNK = 64
ATTN_TK = 1024
MOE_TILE = 512
GATHER_UNROLL = 8
LOG2E = 1.4426950408889634
IN_LORA_W = 4 * LORA_PAD + GATE_LORA
IN_OFF_RKV = 1024
IN_OFF_GATE = IN_OFF_RKV + 3 * RWKV_WIDTH
IN_OFF_MLA = IN_OFF_GATE + 2 * D_MODEL
IN_MLA_W = 1024
IN_TOTAL = IN_OFF_MLA + IN_MLA_W
VMEM_LIMIT = 56 * 1024 * 1024


def _cparams(sem, vmem=VMEM_LIMIT, flags=None):
    return pltpu.CompilerParams(dimension_semantics=sem, vmem_limit_bytes=vmem, flags=flags)


def _iota(shape, dim):
    return lax.broadcasted_iota(jnp.int32, shape, dim)


def _sigmoid(x):
    return 1.0 / (1.0 + jnp.exp(-x))


def _dot(a, b):
    return jnp.dot(a, b, preferred_element_type=F32)


def _dot_nt(a, b):
    return lax.dot_general(a, b, (((1,), (1,)), ((), ())), preferred_element_type=F32)


def _dot_tn(a, b):
    return lax.dot_general(a, b, (((0,), (0,)), ((), ())), preferred_element_type=F32)


def _head_ones():
    r = lax.shift_right_logical(_iota((LANES, LANES), 0), HEAD_SHIFT)
    c = lax.shift_right_logical(_iota((LANES, LANES), 1), HEAD_SHIFT)
    return jnp.where(r == c, 1.0, 0.0).astype(BF16)


def _head_sum(x, ones_bd):
    w = ones_bd.shape[0]
    hi = x.astype(BF16)
    lo = (x - hi.astype(F32)).astype(BF16)
    parts = [_dot(hi[:, j:j + w], ones_bd) + _dot(lo[:, j:j + w], ones_bd) for j in range(0, x.shape[1], w)]
    return parts[0] if len(parts) == 1 else jnp.concatenate(parts, axis=1)


def _rmsnorm_kernel(x_ref, g_ref, o_ref):
    x = x_ref[...]
    y = x * lax.rsqrt(jnp.mean(x * x, axis=-1, keepdims=True) + RMS_EPS)
    o_ref[...] = (y * g_ref[...]).astype(o_ref.dtype)


def _rmsnorm(x, g, tm=512):
    m, d = x.shape
    return pl.pallas_call(
        _rmsnorm_kernel,
        grid=(m // tm,),
        in_specs=[pl.BlockSpec((tm, d), lambda i: (i, 0)),
                  pl.BlockSpec((1, d), lambda i: (0, 0))],
        out_specs=pl.BlockSpec((tm, d), lambda i: (i, 0)),
        out_shape=jax.ShapeDtypeStruct((m, d), BF16),
        compiler_params=_cparams(("parallel",)),
        name="rmsnorm",
    )(x, g.reshape(1, d))


def _in_proj_kernel(x_ref, g_ref, w_ref, o_ref, n_ref):
    @pl.when(pl.program_id(1) == 0)
    def _():
        x = x_ref[...]
        y = x * lax.rsqrt(jnp.mean(x * x, axis=-1, keepdims=True) + RMS_EPS)
        n_ref[...] = (y * g_ref[...]).astype(BF16)

    o_ref[...] = _dot(n_ref[...], w_ref[...])


def _in_proj(x, g, w, tm=1024, tn=512):
    m, d = x.shape
    n = w.shape[1]
    return pl.pallas_call(
        _in_proj_kernel,
        grid=(m // tm, n // tn),
        in_specs=[pl.BlockSpec((tm, d), lambda i, j: (i, 0)),
                  pl.BlockSpec((1, d), lambda i, j: (0, 0)),
                  pl.BlockSpec((d, tn), lambda i, j: (0, j))],
        out_specs=pl.BlockSpec((tm, tn), lambda i, j: (i, j)),
        out_shape=jax.ShapeDtypeStruct((m, n), F32),
        scratch_shapes=[pltpu.VMEM((tm, d), BF16)],
        compiler_params=_cparams(("parallel", "arbitrary")),
        name="in_proj",
    )(x, g.reshape(1, d), w)


def _mm_kernel(a_ref, b_ref, o_ref):
    o_ref[...] = _dot(a_ref[...], b_ref[...]).astype(o_ref.dtype)


def _mm_res_kernel(a_ref, b_ref, r_ref, o_ref):
    o_ref[...] = (r_ref[...] + _dot(a_ref[...], b_ref[...])).astype(o_ref.dtype)


def _matmul(a, b, out_dtype, tm, tn, res=None, name="matmul"):
    m, k = a.shape
    n = b.shape[1]
    in_specs = [pl.BlockSpec((tm, k), lambda i, j: (i, 0)),
                pl.BlockSpec((k, tn), lambda i, j: (0, j))]
    args = [a, b]
    kern = _mm_kernel
    if res is not None:
        in_specs.append(pl.BlockSpec((tm, tn), lambda i, j: (i, j)))
        args.append(res)
        kern = _mm_res_kernel
    return pl.pallas_call(
        kern,
        grid=(m // tm, n // tn),
        in_specs=in_specs,
        out_specs=pl.BlockSpec((tm, tn), lambda i, j: (i, j)),
        out_shape=jax.ShapeDtypeStruct((m, n), out_dtype),
        compiler_params=_cparams(("parallel", "arbitrary")),
        name=name,
    )(*args)


def _mla_proj_kernel(z_ref, pos_ref, invf_ref, gq_ref, gkv_ref, wq_ref, wkv_ref,
                     q_ref, k_ref, v_ref):
    tm = z_ref.shape[0]
    z = z_ref[...]

    def norm(c, g):
        return (c * lax.rsqrt(jnp.mean(c * c, axis=-1, keepdims=True) + RMS_EPS) * g).astype(BF16)

    cq = norm(z[:, :Q_LORA], gq_ref[...])
    ckv = norm(z[:, Q_LORA:Q_LORA + KV_LORA], gkv_ref[...])
    q = _dot(cq, wq_ref[...]) * (QK_DIM ** -0.5 * LOG2E)
    kv = _dot(ckv, wkv_ref[...])

    ang = pos_ref[...].astype(F32) * invf_ref[...]
    lane = _iota((tm, LANES), 1)
    half = ROPE_DIM // 2
    cos, sin = jnp.cos(ang), jnp.sin(ang)
    c_tab = jnp.where(lane < ROPE_DIM, cos, 0.0)
    s_up = jnp.where((lane >= half) & (lane < ROPE_DIM), sin, 0.0)
    s_dn = jnp.where(lane < half, -sin, 0.0)

    def rope(x):
        return (x * c_tab + pltpu.roll(x, half, 1) * s_up
                + pltpu.roll(x, LANES - half, 1) * s_dn)

    k_r = rope(z[:, Q_LORA + KV_LORA:Q_LORA + KV_LORA + LANES]).astype(BF16)
    for h in range(MLA_HEADS):
        lo = h * QK_PAD
        q_ref[:, lo:lo + NOPE_DIM] = q[:, lo:lo + NOPE_DIM].astype(BF16)
        q_ref[:, lo + NOPE_DIM:lo + QK_PAD] = rope(q[:, lo + NOPE_DIM:lo + QK_PAD]).astype(BF16)
        k_ref[:, lo:lo + NOPE_DIM] = kv[:, h * NOPE_DIM:(h + 1) * NOPE_DIM].astype(BF16)
        k_ref[:, lo + NOPE_DIM:lo + QK_PAD] = k_r
    v_ref[...] = kv[:, MLA_HEADS * NOPE_DIM:].astype(BF16)


def _mla_proj(z_all, pos, invf, g_q, g_kv, wq, wkv, tm=256):
    m = z_all.shape[0]
    full = lambda shape: pl.BlockSpec(shape, lambda i: (0, 0))
    row = lambda n: pl.BlockSpec((tm, n), lambda i: (i, 0))
    z_spec = pl.BlockSpec((tm, IN_MLA_W), lambda i: (i, IN_OFF_MLA // IN_MLA_W))
    return pl.pallas_call(
        _mla_proj_kernel,
        grid=(m // tm,),
        in_specs=[z_spec, row(1), full((1, LANES)), full((1, Q_LORA)), full((1, KV_LORA)),
                  full(wq.shape), full(wkv.shape)],
        out_specs=[row(MLA_HEADS * QK_PAD), row(MLA_HEADS * QK_PAD), row(MLA_HEADS * V_DIM)],
        out_shape=[jax.ShapeDtypeStruct((m, MLA_HEADS * QK_PAD), BF16),
                   jax.ShapeDtypeStruct((m, MLA_HEADS * QK_PAD), BF16),
                   jax.ShapeDtypeStruct((m, MLA_HEADS * V_DIM), BF16)],
        compiler_params=_cparams(("parallel",)),
        name="mla_proj",
    )(z_all, pos, invf, g_q.reshape(1, -1), g_kv.reshape(1, -1), wq, wkv)


def _mla_attn_kernel(q_ref, k_ref, v_ref, o_ref):
    tq = q_ref.shape[0]
    nk = k_ref.shape[0] // ATTN_TK
    q = q_ref[...]

    def scores(j):
        return _dot_nt(q, k_ref[j * ATTN_TK:(j + 1) * ATTN_TK, :])

    m = jnp.full((tq, 1), -jnp.inf, F32)
    l = jnp.zeros((tq, 1), F32)
    acc = jnp.zeros((tq, V_DIM), F32)
    s_next = scores(0)
    for j in range(nk):
        s = s_next
        if j + 1 < nk:
            s_next = scores(j + 1)
        m_new = jnp.maximum(m, jnp.max(s, axis=-1, keepdims=True))
        alpha = jnp.exp2(m - m_new)
        p = jnp.exp2(s - m_new)
        l = alpha * l + jnp.sum(p, axis=-1, keepdims=True)
        acc = alpha * acc + _dot(p.astype(BF16), v_ref[j * ATTN_TK:(j + 1) * ATTN_TK, :])
        m = m_new
    o_ref[...] = (acc / l).astype(o_ref.dtype)


def _mla_attn(q, k, v, tq=1024):
    b, s, _ = q.shape
    return pl.pallas_call(
        _mla_attn_kernel,
        grid=(b, MLA_HEADS, s // tq),
        in_specs=[pl.BlockSpec((None, tq, QK_PAD), lambda b_, h, i: (b_, i, h)),
                  pl.BlockSpec((None, s, QK_PAD), lambda b_, h, i: (b_, 0, h)),
                  pl.BlockSpec((None, s, V_DIM), lambda b_, h, i: (b_, 0, h))],
        out_specs=pl.BlockSpec((None, tq, V_DIM), lambda b_, h, i: (b_, i, h)),
        out_shape=jax.ShapeDtypeStruct((b, s, MLA_HEADS * V_DIM), BF16),
        compiler_params=_cparams(("parallel", "parallel", "arbitrary")),
        name="mla_attn",
    )(q, k, v)


def _rwkv_prep_kernel(seq_len,
                      r_ref, rp_ref, rn_ref, k_ref, kp_ref, kn_ref, v_ref, vp_ref, vn_ref,
                      l_ref, lp_ref, ln_ref, scr_ref, sck_ref, scv_ref, scl_ref,
                      w2f_ref, w2b_ref, a2f_ref, a2b_ref, g2_ref,
                      w0f_ref, w0b_ref, a0f_ref, a0b_ref, kk_w_ref,
                      ro_ref, ko_ref, vo_ref, kko_ref, lwf_ref, lwb_ref, alf_ref, alb_ref, g_ref):
    tm = r_ref.shape[0]
    i = pl.program_id(0)
    first = lax.rem(i * tm, seq_len) == 0
    last = lax.rem((i + 1) * tm, seq_len) == 0

    def shift(z_ref, zp_ref, zn_ref, w_ref):
        z = z_ref[...]
        rows = _iota(z.shape, 0)
        prev_row = jnp.where(first, 0.0, zp_ref[SUBLANES - 1:SUBLANES, :])
        next_row = jnp.where(last, 0.0, zn_ref[0:1, :])
        z_prev = jnp.where(rows == 0, prev_row, pltpu.roll(z, 1, 0))
        z_next = jnp.where(rows == tm - 1, next_row, pltpu.roll(z, tm - 1, 0))
        return w_ref[0:1, :] * z_prev + w_ref[1:2, :] * z + w_ref[2:3, :] * z_next

    r = shift(r_ref, rp_ref, rn_ref, scr_ref)
    k = shift(k_ref, kp_ref, kn_ref, sck_ref)
    v = shift(v_ref, vp_ref, vn_ref, scv_ref)
    lo = shift(l_ref, lp_ref, ln_ref, scl_ref)
    xw_f = lo[:, 0 * LORA_PAD:1 * LORA_PAD]
    xw_b = lo[:, 1 * LORA_PAD:2 * LORA_PAD]
    xa_f = lo[:, 2 * LORA_PAD:3 * LORA_PAD]
    xa_b = lo[:, 3 * LORA_PAD:4 * LORA_PAD]
    xg = lo[:, 4 * LORA_PAD:]

    def log_decay(xw, w0_ref, w2_ref):
        y = -(w0_ref[...] + _dot(jnp.tanh(xw).astype(BF16), w2_ref[...]))
        softplus = jnp.maximum(y, 0.0) + jnp.log(1.0 + jnp.exp(-jnp.abs(y)))
        return -jnp.exp(-softplus - 0.5)

    def rate(xa, a0_ref, a2_ref):
        return _sigmoid(a0_ref[...] + _dot(xa.astype(BF16), a2_ref[...]))

    al_f = rate(xa_f, a0f_ref, a2f_ref)
    al_b = rate(xa_b, a0b_ref, a2b_ref)
    kk = k * kk_w_ref[...]
    kk = kk * lax.rsqrt(_head_sum(kk * kk, _head_ones()) + 1e-12)

    ro_ref[...] = r
    ko_ref[...] = k
    vo_ref[...] = v
    kko_ref[...] = kk
    lwf_ref[...] = log_decay(xw_f, w0f_ref, w2f_ref)
    lwb_ref[...] = log_decay(xw_b, w0b_ref, w2b_ref)
    alf_ref[...] = al_f
    alb_ref[...] = al_b
    g_ref[...] = _dot(_sigmoid(xg).astype(BF16), g2_ref[...])


def _rwkv_prep(z_all, seq_len, sc_rkv, sc_lora, w2f, w2b, a2f, a2b, g2,
               w0f, w0b, a0f, a0b, k_k, tm=512, tw=512):
    m = z_all.shape[0]
    wl = IN_LORA_W
    nb = RWKV_WIDTH // tw
    z0 = IN_OFF_RKV // tw
    rb = tm // SUBLANES
    nrb = m // SUBLANES

    def main(seg):
        return pl.BlockSpec((tm, tw), lambda i, p: (i, z0 + seg * nb + p))

    def prev(seg):
        return pl.BlockSpec((SUBLANES, tw),
                            lambda i, p: (jnp.maximum(i * rb - 1, 0), z0 + seg * nb + p))

    def nxt(seg):
        return pl.BlockSpec((SUBLANES, tw),
                            lambda i, p: (jnp.minimum((i + 1) * rb, nrb - 1), z0 + seg * nb + p))

    def sc(seg):
        return pl.BlockSpec((3, tw), lambda i, p: (0, seg * nb + p))

    colblk = lambda rows: pl.BlockSpec((rows, tw), lambda i, p: (0, p))
    in_specs = []
    for seg in range(3):
        in_specs += [main(seg), prev(seg), nxt(seg)]
    in_specs += [pl.BlockSpec((tm, wl), lambda i, p: (i, 0)),
                 pl.BlockSpec((SUBLANES, wl), lambda i, p: (jnp.maximum(i * rb - 1, 0), 0)),
                 pl.BlockSpec((SUBLANES, wl), lambda i, p: (jnp.minimum((i + 1) * rb, nrb - 1), 0))]
    in_specs += [sc(0), sc(1), sc(2), pl.BlockSpec((3, wl), lambda i, p: (0, 0))]
    in_specs += [colblk(LORA_PAD)] * 4 + [colblk(GATE_LORA)] + [colblk(1)] * 5
    out_spec = pl.BlockSpec((tm, tw), lambda i, p: (i, p))
    n_out = 9
    row = lambda a: a.reshape(1, -1)
    return pl.pallas_call(
        functools.partial(_rwkv_prep_kernel, seq_len),
        grid=(m // tm, nb),
        in_specs=in_specs,
        out_specs=[out_spec] * n_out,
        out_shape=[jax.ShapeDtypeStruct((m, RWKV_WIDTH), F32)] * n_out,
        compiler_params=_cparams(("parallel", "arbitrary")),
        name="rwkv_prep",
    )(*([z_all] * 12), sc_rkv, sc_rkv, sc_rkv, sc_lora,
      w2f, w2b, a2f, a2b, g2, row(w0f), row(w0b), row(a0f), row(a0b), row(k_k))


def _scan_chunks(chains):
    c = CHUNK
    nc = range(len(chains))
    r, v, kk, lw, al, k, rk, ka, ht, rev = [list(t) for t in zip(*chains)]
    t_i = _iota((c, c), 0)
    s_i = _iota((c, c), 1)
    tri = {False: jnp.where(s_i <= t_i, 1.0, 0.0).astype(BF16),
           True: jnp.where(s_i >= t_i, 1.0, 0.0).astype(BF16)}
    row = _iota((c, QW), 0)
    col = jnp.bitwise_and(_iota((c, QW), 1), RWKV_HEAD - 1)
    strict = {False: col < row, True: col > row}
    incl = {False: col <= row, True: col >= row}
    eye = jnp.where(row == col, 1.0, 0.0)
    lane_head = lax.shift_right_logical(_iota((c, QW), 1), HEAD_SHIFT)
    head_mask = [lane_head == h for h in range(QUAD)]
    on_diag = (lax.shift_right_logical(_iota((QW, QW), 0), HEAD_SHIFT)
               == lax.shift_right_logical(_iota((QW, QW), 1), HEAD_SHIFT))

    def blockdiag(x):
        xb = x.astype(BF16)
        zero = jnp.zeros_like(xb)
        return jnp.concatenate([jnp.where(head_mask[h], xb, zero) for h in range(QUAD)], axis=0)

    def cat(x, y, axis=0):
        return jnp.concatenate([x, y], axis=axis)

    a = [-kk[i] for i in nc]
    b = [kk[i] * al[i] for i in nc]
    kd = [k[i] * (1.0 + (al[i] - 1.0) * ka[i]) for i in nc]
    lw_hi = [lw[i].astype(BF16) for i in nc]
    lw_lo = [(lw[i] - lw_hi[i].astype(F32)).astype(BF16) for i in nc]
    cum = [_dot(tri[rev[i]], lw_hi[i]) + _dot(tri[rev[i]], lw_lo[i]) for i in nc]
    cum_prev = [cum[i] - lw[i] for i in nc]
    ref_row = [c // 2 if rev[i] else c // 2 - 1 for i in nc]
    tot_row = [0 if rev[i] else c - 1 for i in nc]
    c_ref = [cum[i][ref_row[i]:ref_row[i] + 1, :] for i in nc]
    c_tot = [cum[i][tot_row[i]:tot_row[i] + 1, :] for i in nc]
    e_inv = [jnp.exp(c_ref[i] - cum[i]) for i in nc]
    e_out = [jnp.exp(c_tot[i] - cum[i]) for i in nc]
    e_cur = [jnp.exp(cum[i] - c_ref[i]) for i in nc]
    lhs = [jnp.concatenate([a[i] * jnp.exp(cum_prev[i] - c_ref[i]), r[i] * e_cur[i], r[i] * rk[i] * e_cur[i]],
                           axis=0).astype(BF16) for i in nc]
    rhs = [cat(blockdiag(b[i] * e_inv[i]), blockdiag(kd[i] * e_inv[i])) for i in nc]
    nmat = [_dot_nt(lhs[i], rhs[i]) for i in nc]
    n_ab = [jnp.where(strict[rev[i]], nmat[i][:c, :QW], 0.0) for i in nc]
    n_ak = [jnp.where(strict[rev[i]], nmat[i][:c, QW:], 0.0) for i in nc]
    n_rb = [jnp.where(incl[rev[i]], nmat[i][c:2 * c, :QW], 0.0).astype(BF16) for i in nc]
    n_rk = [jnp.where(incl[rev[i]], nmat[i][c:2 * c, QW:], 0.0) + jnp.where(row == col, nmat[i][2 * c:, QW:], 0.0)
            for i in nc]

    n_sq = int(np.log2(c)) - 1
    pw = n_ab
    tinv = [eye + pw[i] for i in nc]
    pw = [_dot(pw[i].astype(BF16), blockdiag(pw[i])) for i in nc]
    for j in range(n_sq):
        if j + 1 < n_sq:
            both = [_dot(cat(tinv[i], pw[i]).astype(BF16), blockdiag(pw[i])) for i in nc]
            tinv = [tinv[i] + both[i][:c] for i in nc]
            pw = [both[i][c:] for i in nc]
        else:
            tinv = [tinv[i] + _dot(tinv[i].astype(BF16), blockdiag(pw[i])) for i in nc]

    v_bd = [blockdiag(v[i]) for i in nc]
    state_lhs = [cat(a[i] * jnp.exp(cum_prev[i]), r[i] * jnp.exp(cum[i])).astype(BF16) for i in nc]
    from_state = [_dot_nt(state_lhs[i], ht[i].astype(BF16)) for i in nc]
    from_v = [_dot(cat(n_ak[i], n_rk[i]).astype(BF16), v_bd[i]) for i in nc]
    u = [_dot(tinv[i].astype(BF16), blockdiag(from_state[i][:c] + from_v[i][:c])) for i in nc]
    o = [from_state[i][c:] + from_v[i][c:] + _dot(n_rb[i], blockdiag(u[i])) for i in nc]
    u_b = [u[i].astype(BF16) for i in nc]
    v_b = [v[i].astype(BF16) for i in nc]
    upd = [_dot_tn(u_b[i], (b[i] * e_out[i]).astype(BF16)) + _dot_tn(v_b[i], (kd[i] * e_out[i]).astype(BF16))
           for i in nc]
    ht_new = [ht[i] * jnp.exp(c_tot[i]) + jnp.where(on_diag, upd[i], 0.0) for i in nc]
    return [(o[i], ht_new[i]) for i in nc]


def _rwkv_scan_kernel(rf, vf, kkf, lwf, alf, kf, rb, vb, kkb, lwb, alb, kb, rk_ref, ka_ref,
                      yf_ref, yb_ref, h_ref):
    @pl.when(pl.program_id(2) == 0)
    def _():
        h_ref[...] = jnp.zeros_like(h_ref)

    n_chunks = rf.shape[0] // CHUNK
    n_groups = rf.shape[1] // QW

    def body(ci, carry):
        chains = []
        for d, (refs, y_ref) in enumerate((((rf, vf, kkf, lwf, alf, kf), yf_ref),
                                           ((rb, vb, kkb, lwb, alb, kb), yb_ref))):
            cidx = ci if d == 0 else n_chunks - 1 - ci
            rows = pl.ds(pl.multiple_of(cidx * CHUNK, CHUNK), CHUNK)
            for g in range(n_groups):
                lanes = slice(g * QW, (g + 1) * QW)
                vals = [x[rows, lanes] for x in refs] + [rk_ref[:, lanes], ka_ref[:, lanes], h_ref[d, g]]
                chains.append((d, g, y_ref, rows, lanes, vals))
        results = _scan_chunks([tuple(vals) + (d == 1,) for d, _, _, _, _, vals in chains])
        for (d, g, y_ref, rows, lanes, _), (y, ht) in zip(chains, results):
            y_ref[rows, lanes] = y
            h_ref[d, g] = ht
        return carry

    lax.fori_loop(0, n_chunks, body, 0)


def _rwkv_scan(r, k, v, kk, lw_f, lw_b, al_f, al_b, r_k, k_a, tb=128, groups=8):
    b, s, w = r.shape
    nblk = s // tb
    wb = groups * QW
    fwd = pl.BlockSpec((None, tb, wb), lambda b_, p, j: (b_, j, p))
    bwd = pl.BlockSpec((None, tb, wb), lambda b_, p, j: (b_, nblk - 1 - j, p))
    vec = pl.BlockSpec((1, wb), lambda b_, p, j: (0, p))
    return pl.pallas_call(
        _rwkv_scan_kernel,
        grid=(b, w // wb, nblk),
        in_specs=[fwd] * 6 + [bwd] * 6 + [vec, vec],
        out_specs=[fwd, bwd],
        out_shape=[jax.ShapeDtypeStruct((b, s, w), F32)] * 2,
        scratch_shapes=[pltpu.VMEM((2, groups, QW, QW), F32)],
        compiler_params=_cparams(("parallel", "parallel", "arbitrary")),
        name="rwkv_scan",
    )(r, v, kk, lw_f, al_f, k, r, v, kk, lw_b, al_b, k, r_k.reshape(1, -1), k_a.reshape(1, -1))


def _rwkv_post_kernel(yf_ref, yb_ref, g_ref, lg_ref, lb_ref, o_ref):
    ones_bd = _head_ones()
    y = yf_ref[...] + yb_ref[...]
    mu = _head_sum(y, ones_bd) * (1.0 / RWKV_HEAD)
    d = y - mu
    var = _head_sum(d * d, ones_bd) * (1.0 / RWKV_HEAD)
    yn = d * lax.rsqrt(var + LNX_EPS) * lg_ref[...] + lb_ref[...]
    o_ref[...] = (yn * g_ref[...]).astype(o_ref.dtype)


def _rwkv_post(y_f, y_b, g, lnx_g, lnx_b, tm=512, tw=512):
    m, w = y_f.shape
    blk = pl.BlockSpec((tm, tw), lambda i, p: (i, p))
    vec = pl.BlockSpec((1, tw), lambda i, p: (0, p))
    return pl.pallas_call(
        _rwkv_post_kernel,
        grid=(m // tm, w // tw),
        in_specs=[blk, blk, blk, vec, vec],
        out_specs=blk,
        out_shape=jax.ShapeDtypeStruct((m, w), BF16),
        compiler_params=_cparams(("parallel", "parallel")),
        name="rwkv_post",
    )(y_f, y_b, g, lnx_g.reshape(1, -1), lnx_b.reshape(1, -1))


def _merge_kernel(a1_ref, w1_ref, a2_ref, w2_ref, g1_ref, g2_ref, o_ref):
    m1 = _dot(a1_ref[...], w1_ref[...])
    m2 = _dot(a2_ref[...], w2_ref[...])
    o_ref[...] = (_sigmoid(g1_ref[...]) * m1 + _sigmoid(g2_ref[...]) * m2).astype(o_ref.dtype)


def _merge(o_mla, w_up_mla, o_rwkv, w_up_rwkv, z_all, tm=1024, tn=512):
    m, k = o_mla.shape
    n = w_up_mla.shape[1]
    nj = n // tn
    g0 = IN_OFF_GATE // tn
    a_spec = pl.BlockSpec((tm, k), lambda i, j: (i, 0))
    w_spec = pl.BlockSpec((k, tn), lambda i, j: (0, j))
    return pl.pallas_call(
        _merge_kernel,
        grid=(m // tm, nj),
        in_specs=[a_spec, w_spec, a_spec, w_spec,
                  pl.BlockSpec((tm, tn), lambda i, j: (i, g0 + j)),
                  pl.BlockSpec((tm, tn), lambda i, j: (i, g0 + nj + j))],
        out_specs=pl.BlockSpec((tm, tn), lambda i, j: (i, j)),
        out_shape=jax.ShapeDtypeStruct((m, n), BF16),
        compiler_params=_cparams(("parallel", "arbitrary")),
        name="merge",
    )(o_mla, w_up_mla, o_rwkv, w_up_rwkv, z_all, z_all)


def _cross_router_kernel(h_ref, gc_ref, wq_ref, kv_ref, wo_ref, gf_ref, wr_ref, br_ref,
                         h2_ref, n3_ref, info_ref, cnt_ref, carry_ref):
    @pl.when(pl.program_id(0) == 0)
    def _():
        carry_ref[...] = jnp.zeros_like(carry_ref)

    h = h_ref[...]
    hn = (h * lax.rsqrt(jnp.mean(h * h, axis=-1, keepdims=True) + RMS_EPS) * gc_ref[...]).astype(BF16)
    q = (_dot(hn, wq_ref[...]) * (CROSS_HEAD_DIM ** -0.5)).astype(BF16)
    kv = kv_ref[...]
    outs = []
    for hd in range(CROSS_HEADS):
        lo = hd * CROSS_HEAD_DIM
        s = _dot_nt(q[:, lo:lo + CROSS_HEAD_DIM], kv[:, lo:lo + CROSS_HEAD_DIM])
        p = jnp.exp(s - jnp.max(s, axis=-1, keepdims=True))
        p = p / jnp.sum(p, axis=-1, keepdims=True)
        outs.append(_dot(p.astype(BF16), kv[:, CROSS_WIDTH + lo:CROSS_WIDTH + lo + CROSS_HEAD_DIM]))
    o = jnp.concatenate(outs, axis=-1).astype(BF16)
    h2 = h + _dot(o, wo_ref[...])
    h2_ref[...] = h2

    n3 = h2 * lax.rsqrt(jnp.mean(h2 * h2, axis=-1, keepdims=True) + RMS_EPS) * gf_ref[...]
    n3_ref[...] = n3

    wr = wr_ref[...]
    n_hi, w_hi = n3.astype(BF16), wr.astype(BF16)
    n_lo = (n3 - n_hi.astype(F32)).astype(BF16)
    w_lo = (wr - w_hi.astype(F32)).astype(BF16)
    logits = _dot(n_hi, w_hi) + (_dot(n_hi, w_lo) + _dot(n_lo, w_hi)) + br_ref[...]
    lane = _iota(logits.shape, 1)
    lane_f = lane.astype(F32)
    neg = jnp.float32(-jnp.inf)
    big = jnp.float32(1e9)

    def masked_softmax(mask):
        x = jnp.where(mask, logits, neg)
        e = jnp.exp(x - jnp.max(x, axis=-1, keepdims=True))
        return e / jnp.sum(e, axis=-1, keepdims=True)

    def top1(prob, mask):
        pmax = jnp.max(jnp.where(mask, prob, -1.0), axis=-1, keepdims=True)
        idx = jnp.min(jnp.where(mask & (prob == pmax), lane_f, big), axis=-1, keepdims=True)
        return pmax, idx

    g_mask = (lane >= N_EXPERTS) & (lane < N_EXPERTS + N_GROUPS)
    p_group, g_idx = top1(masked_softmax(g_mask), g_mask)
    g_sel = g_idx - float(N_EXPERTS)
    e_mask = (lane < N_EXPERTS) & (lax.shift_right_logical(lane, GROUP_SHIFT).astype(F32) == g_sel)
    e_prob = masked_softmax(e_mask)
    p1, i1 = top1(e_prob, e_mask)
    rest = e_mask & (lane_f != i1)
    p2, i2 = top1(e_prob, rest)
    denom = p1 + p2
    w1 = p_group * (p1 / denom)
    w2 = p_group * (p2 / denom)

    tm = logits.shape[0]
    oh1 = jnp.where(lane_f == i1, 1.0, 0.0)
    oh2 = jnp.where(lane_f == i2, 1.0, 0.0)
    before = jnp.where(_iota((tm, tm), 1) < _iota((tm, tm), 0), 1.0, 0.0).astype(BF16)
    carry = carry_ref[...]
    cnt1 = jnp.sum(oh1, axis=0, keepdims=True)
    cnt2 = jnp.sum(oh2, axis=0, keepdims=True)
    rank1 = jnp.sum(oh1 * (carry + _dot(before, oh1.astype(BF16))), axis=-1, keepdims=True)
    rank2 = jnp.sum(oh2 * (carry + cnt1 + _dot(before, oh2.astype(BF16))), axis=-1, keepdims=True)
    carry = carry + cnt1 + cnt2
    carry_ref[...] = carry
    cnt_ref[...] = carry
    info = jnp.zeros_like(logits)
    for k, val in enumerate((i1, i2, w1, w2, rank1, rank2)):
        info = jnp.where(lane == k, val, info)
    info_ref[...] = info


def _cross_router(h1, seq_len, g_cross, wq, kvm, wo, g_ffn, w_r, b_r, tm=512):
    m, d = h1.shape
    full = lambda a: pl.BlockSpec(a.shape, lambda i: (0,) * a.ndim)
    row = lambda n: pl.BlockSpec((tm, n), lambda i: (i, 0))
    gc, gf = g_cross.reshape(1, d), g_ffn.reshape(1, d)
    per_seq = seq_len // tm
    return pl.pallas_call(
        _cross_router_kernel,
        grid=(m // tm,),
        in_specs=[row(d), full(gc), full(wq),
                  pl.BlockSpec((None,) + kvm.shape[1:], lambda i: (i // per_seq, 0, 0)),
                  full(wo), full(gf), full(w_r), full(b_r)],
        out_specs=[row(d), row(d), row(LANES), pl.BlockSpec((1, LANES), lambda i: (0, 0))],
        out_shape=[jax.ShapeDtypeStruct((m, d), F32), jax.ShapeDtypeStruct((m, d), F32),
                   jax.ShapeDtypeStruct((m, LANES), F32), jax.ShapeDtypeStruct((1, LANES), F32)],
        scratch_shapes=[pltpu.VMEM((1, LANES), F32)],
        compiler_params=_cparams(("arbitrary",)),
        name="cross_router",
    )(h1, gc, wq, kvm, wo, gf, w_r, b_r)


def _gather_rows(src_hbm, idx_ref, base, dst, sem, n_rows):
    def issue(g, carry):
        for u in range(GATHER_UNROLL):
            r = g * GATHER_UNROLL + u
            src_row = idx_ref[base + r]
            pltpu.make_async_copy(src_hbm.at[pl.ds(src_row, 1), :], dst.at[pl.ds(r, 1), :], sem).start()
        return carry
    lax.fori_loop(0, n_rows // GATHER_UNROLL, issue, 0)


def _wait_rows(src_hbm, dst, sem, n_rows):
    pltpu.make_async_copy(src_hbm.at[pl.ds(0, n_rows), :], dst, sem).wait()


def _moe_dispatch_kernel(pos1, pos2, x_ref, xs_zero, xs_ref, sem):
    del xs_zero
    i = pl.program_id(0)
    tm = x_ref.shape[0]

    def issue(g, carry):
        for u in range(GATHER_UNROLL):
            r = g * GATHER_UNROLL + u
            src = x_ref.at[pl.ds(r, 1), :]
            pltpu.make_async_copy(src, xs_ref.at[pl.ds(pos1[i * tm + r], 1), :], sem.at[0]).start()
            pltpu.make_async_copy(src, xs_ref.at[pl.ds(pos2[i * tm + r], 1), :], sem.at[1]).start()
        return carry

    lax.fori_loop(0, tm // GATHER_UNROLL, issue, 0)
    for k in range(2):
        pltpu.make_async_copy(x_ref, xs_ref.at[pl.ds(0, tm), :], sem.at[k]).wait()


def _moe_dispatch(n3p, pos1, pos2, p_max, tm=512):
    m, w = n3p.shape
    grid_spec = pltpu.PrefetchScalarGridSpec(
        num_scalar_prefetch=2,
        grid=(m // tm,),
        in_specs=[pl.BlockSpec((tm, w), lambda i, p1, p2: (i, 0)),
                  pl.BlockSpec(memory_space=pl.ANY)],
        out_specs=pl.BlockSpec(memory_space=pl.ANY),
        scratch_shapes=[pltpu.SemaphoreType.DMA((2,))],
    )
    return pl.pallas_call(
        _moe_dispatch_kernel,
        grid_spec=grid_spec,
        out_shape=jax.ShapeDtypeStruct((p_max, w), n3p.dtype),
        input_output_aliases={3: 0},
        compiler_params=_cparams(("arbitrary",)),
        name="moe_dispatch",
    )(pos1, pos2, n3p, jnp.zeros((p_max, w), n3p.dtype))


def _tile_expert(t, tile_end):
    t = jnp.minimum(t, tile_end[N_EXPERTS - 1] - 1)
    e = jnp.int32(0)
    for k in range(N_EXPERTS - 1):
        e = e + (tile_end[k] <= t).astype(jnp.int32)
    return e


def _moe_expert_kernel(tile_end, x_ref, wg_ref, wu_ref, wd_ref, y_ref, wg_b, wu_b, wd_b):
    t = pl.program_id(0)
    n_valid = tile_end[N_EXPERTS - 1]
    changed = jnp.logical_or(t == 0, _tile_expert(t, tile_end) != _tile_expert(jnp.maximum(t - 1, 0), tile_end))

    @pl.when(changed)
    def _():
        wg_b[...] = wg_ref[...].astype(BF16)
        wu_b[...] = wu_ref[...].astype(BF16)
        wd_b[...] = wd_ref[...].astype(BF16)

    @pl.when(t < n_valid)
    def _():
        x = x_ref[...].astype(BF16)
        hg = _dot(x, wg_b[...])
        hu = _dot(x, wu_b[...])
        hid = (hg * _sigmoid(hg) * hu).astype(BF16)
        y_ref[...] = _dot(hid, wd_b[...])

    @pl.when(t >= n_valid)
    def _():
        y_ref[...] = jnp.zeros_like(y_ref)


def _moe_experts(xs, tile_end, w_eg, w_eu, w_ed):
    p_max, w = xs.shape
    ne, d, f = w_eg.shape
    last = lambda t, te: jnp.minimum(t, te[N_EXPERTS - 1] - 1)
    grid_spec = pltpu.PrefetchScalarGridSpec(
        num_scalar_prefetch=1,
        grid=(p_max // MOE_TILE,),
        in_specs=[pl.BlockSpec((MOE_TILE, w), lambda t, te: (last(t, te), 0)),
                  pl.BlockSpec((None, d, f), lambda t, te: (_tile_expert(t, te), 0, 0)),
                  pl.BlockSpec((None, d, f), lambda t, te: (_tile_expert(t, te), 0, 0)),
                  pl.BlockSpec((None, f, d), lambda t, te: (_tile_expert(t, te), 0, 0))],
        out_specs=pl.BlockSpec((MOE_TILE, d), lambda t, te: (t, 0)),
        scratch_shapes=[pltpu.VMEM((d, f), BF16), pltpu.VMEM((d, f), BF16), pltpu.VMEM((f, d), BF16)],
    )
    return pl.pallas_call(
        _moe_expert_kernel,
        grid_spec=grid_spec,
        out_shape=jax.ShapeDtypeStruct((p_max, d), F32),
        compiler_params=_cparams(("arbitrary",)),
        name="moe_experts",
    )(tile_end, xs, w_eg, w_eu, w_ed)


def _moe_combine_kernel(pos1, pos2, y_hbm, h_ref, info_ref, g_ref, o_ref, buf1, buf2, sem):
    i = pl.program_id(0)
    ni = pl.num_programs(0)
    tm = h_ref.shape[0]
    slot = lax.rem(i, 2)

    def start(step, s):
        _gather_rows(y_hbm, pos1, step * tm, buf1.at[s], sem.at[0, s], tm)
        _gather_rows(y_hbm, pos2, step * tm, buf2.at[s], sem.at[1, s], tm)

    @pl.when(i == 0)
    def _():
        start(0, 0)

    @pl.when(i + 1 < ni)
    def _():
        start(i + 1, 1 - slot)

    _wait_rows(y_hbm, buf1.at[slot], sem.at[0, slot], tm)
    _wait_rows(y_hbm, buf2.at[slot], sem.at[1, slot], tm)
    info = info_ref[...]
    y = h_ref[...] + info[:, 2:3] * buf1[slot] + info[:, 3:4] * buf2[slot]
    o_ref[...] = y * lax.rsqrt(jnp.mean(y * y, axis=-1, keepdims=True) + RMS_EPS) * g_ref[...]


def _moe_combine(ys, pos1, pos2, h2, info, g_final, tm=256):
    m, d = h2.shape
    grid_spec = pltpu.PrefetchScalarGridSpec(
        num_scalar_prefetch=2,
        grid=(m // tm,),
        in_specs=[pl.BlockSpec(memory_space=pl.ANY),
                  pl.BlockSpec((tm, d), lambda i, p1, p2: (i, 0)),
                  pl.BlockSpec((tm, LANES), lambda i, p1, p2: (i, 0)),
                  pl.BlockSpec((1, d), lambda i, p1, p2: (0, 0))],
        out_specs=pl.BlockSpec((tm, d), lambda i, p1, p2: (i, 0)),
        scratch_shapes=[pltpu.VMEM((2, tm, d), F32), pltpu.VMEM((2, tm, d), F32),
                        pltpu.SemaphoreType.DMA((2, 2))],
    )
    return pl.pallas_call(
        _moe_combine_kernel,
        grid_spec=grid_spec,
        out_shape=jax.ShapeDtypeStruct((m, d), F32),
        compiler_params=_cparams(("arbitrary",)),
        name="moe_combine",
    )(pos1, pos2, ys, h2, info, g_final.reshape(1, d))


def _moe_plan_kernel(info_ref, cnt_ref, pos_ref, end_ref):
    cnt = jnp.broadcast_to(cnt_ref[...], (SUBLANES, LANES))
    tiles = jnp.floor((cnt + (MOE_TILE - 1)) * (1.0 / MOE_TILE))
    upto = jnp.where(_iota((LANES, LANES), 0) <= _iota((LANES, LANES), 1), 1.0, 0.0).astype(BF16)
    tile_end = _dot(tiles.astype(BF16), upto)
    first_row = ((tile_end - tiles) * MOE_TILE)[0:1, :]
    info = info_ref[...]
    lane_f = _iota(info.shape, 1).astype(F32)

    def position(e, rank):
        return jnp.sum(jnp.where(lane_f == e, first_row, 0.0), axis=-1, keepdims=True) + rank

    pos1 = position(info[:, 0:1], info[:, 4:5])
    pos2 = position(info[:, 1:2], info[:, 5:6])
    lane = _iota(info.shape, 1)
    pos_ref[...] = jnp.where(lane == 0, pos1, jnp.where(lane == 1, pos2, 0.0)).astype(jnp.int32)
    end_ref[...] = tile_end.astype(jnp.int32)


def _moe_plan(info, counts, tm=1024):
    m = info.shape[0]
    p_max = 2 * m + N_EXPERTS * MOE_TILE
    pos, tile_end = pl.pallas_call(
        _moe_plan_kernel,
        grid=(m // tm,),
        in_specs=[pl.BlockSpec((tm, LANES), lambda i: (i, 0)), pl.BlockSpec((1, LANES), lambda i: (0, 0))],
        out_specs=[pl.BlockSpec((tm, LANES), lambda i: (i, 0)), pl.BlockSpec((SUBLANES, LANES), lambda i: (0, 0))],
        out_shape=[jax.ShapeDtypeStruct((m, LANES), jnp.int32), jax.ShapeDtypeStruct((SUBLANES, LANES), jnp.int32)],
        compiler_params=_cparams(("arbitrary",)),
        name="moe_plan",
    )(info, counts)
    return pos[:, 0], pos[:, 1], tile_end[0, :N_EXPERTS], p_max


def _pad_cols(w, n):
    return jnp.pad(w, ((0, 0), (0, n - w.shape[1])))


def _pad_rows(w, n):
    return jnp.pad(w, ((0, n - w.shape[0]), (0, 0)))


def _split_lora(w):
    o = 0
    parts = []
    for width in (DECAY_LORA, DECAY_LORA, ICLR_LORA, ICLR_LORA):
        parts.append(_pad_cols(w[:, o:o + width], LORA_PAD))
        o += width
    parts.append(w[:, o:o + GATE_LORA])
    return jnp.concatenate(parts, axis=1)


def kernel(x, mem, positions, g_mix, w_in, g_q, w_uq, g_kv, w_ukv, shift_conv, w0_f, w2_f, w0_b, w2_b, a0_f, a2_f, a0_b, a2_b, g2, k_k, k_a, r_k, lnx_g, lnx_b, w_up_mla, w_up_rwkv, w_out, g_cross, g_mem, wq_c, wkv_c, wo_c, g_ffn, w_rg, b_rg, w_re, b_re, w_eg, w_eu, w_ed, g_final):
    bsz, seq, d = x.shape
    m = bsz * seq
    depth = w_in.shape[0]
    h = x.reshape(m, d)
    pos = positions.reshape(m, 1)
    lane = np.arange(LANES)
    invf = jnp.asarray(np.where(lane < ROPE_DIM, 1.0, 0.0), F32) * (
        ROPE_BASE ** (-jnp.asarray(lane % (ROPE_DIM // 2), F32) * (2.0 / ROPE_DIM)))
    invf = invf.reshape(1, LANES)
    assert depth == 1, "the MoE kernel applies the final norm, so it must be the last layer"
    for l in range(depth):
        wi = w_in[l]
        rw = wi[:, MLA_IN:MLA_IN + RWKV_IN]
        w_all = jnp.concatenate(
            [_pad_cols(_split_lora(rw[:, 3 * RWKV_WIDTH:]), IN_OFF_RKV), rw[:, :3 * RWKV_WIDTH],
             wi[:, MLA_IN + RWKV_IN:], _pad_cols(wi[:, :MLA_IN], IN_MLA_W)], axis=1).astype(BF16)
        assert w_all.shape[1] == IN_TOTAL
        sc = shift_conv[l]
        sc_rkv = sc[:, :3 * RWKV_WIDTH]
        sc_lora = _split_lora(sc[:, 3 * RWKV_WIDTH:])
        wq = w_uq[l].reshape(Q_LORA, MLA_HEADS, QK_DIM)
        wq = jnp.pad(wq, ((0, 0), (0, 0), (0, QK_PAD - QK_DIM))).reshape(Q_LORA, MLA_HEADS * QK_PAD)
        wkv = w_ukv[l].reshape(KV_LORA, MLA_HEADS, NOPE_DIM + V_DIM)
        wkv = jnp.concatenate([wkv[:, :, :NOPE_DIM].reshape(KV_LORA, -1),
                               wkv[:, :, NOPE_DIM:].reshape(KV_LORA, -1)], axis=1)
        lora_rows = lambda w: _pad_rows(w, LORA_PAD).astype(BF16)
        w_router = _pad_cols(jnp.concatenate(
            [jnp.moveaxis(w_re[l], 0, 1).reshape(d, N_EXPERTS), w_rg[l]], axis=1), LANES)
        b_router = _pad_cols(jnp.concatenate([b_re[l].reshape(1, N_EXPERTS), b_rg[l].reshape(1, N_GROUPS)],
                                             axis=1), LANES)

        z_all = _in_proj(h, g_mix[l], w_all)

        q_cat, k_cat, v_mla = _mla_proj(z_all, pos, invf, g_q[l], g_kv[l], wq.astype(BF16), wkv.astype(BF16))
        o_mla = _mla_attn(q_cat.reshape(bsz, seq, -1), k_cat.reshape(bsz, seq, -1),
                          v_mla.reshape(bsz, seq, -1)).reshape(m, -1)

        (r, k, v, kk, lw_f, lw_b, al_f, al_b, gate) = _rwkv_prep(
            z_all, seq, sc_rkv, sc_lora, lora_rows(w2_f[l]), lora_rows(w2_b[l]),
            lora_rows(a2_f[l]), lora_rows(a2_b[l]), g2[l].astype(BF16),
            w0_f[l], w0_b[l], a0_f[l], a0_b[l], k_k[l])
        sh = lambda t: t.reshape(bsz, seq, RWKV_WIDTH)
        y_f, y_b = _rwkv_scan(sh(r), sh(k), sh(v), sh(kk), sh(lw_f), sh(lw_b), sh(al_f), sh(al_b),
                              r_k[l], k_a[l])
        o_rwkv = _rwkv_post(y_f.reshape(m, -1), y_b.reshape(m, -1), gate, lnx_g[l], lnx_b[l])

        merged = _merge(o_mla, w_up_mla[l].astype(BF16), o_rwkv, w_up_rwkv[l].astype(BF16), z_all)
        h1 = _matmul(merged, w_out[l].astype(BF16), F32, 1024, 512, res=h, name="out_proj")

        mem_n = _rmsnorm(mem.reshape(bsz * MEM_LEN, d), g_mem[l], tm=MEM_LEN)
        kvm = _matmul(mem_n, wkv_c[l].astype(BF16), BF16, bsz * MEM_LEN, 512, name="mem_kv")
        h2, n3, info, counts = _cross_router(h1, seq, g_cross[l], wq_c[l].astype(BF16),
                                             kvm.reshape(bsz, MEM_LEN, 2 * CROSS_WIDTH),
                                             wo_c[l].astype(BF16), g_ffn[l], w_router, b_router)

        pos1, pos2, tile_end, p_max = _moe_plan(info, counts)
        flat = lambda w: w.reshape((N_EXPERTS,) + w.shape[2:])
        xs = _moe_dispatch(n3, pos1, pos2, p_max)
        ys = _moe_experts(xs, tile_end, flat(w_eg[l]), flat(w_eu[l]), flat(w_ed[l]))
        out = _moe_combine(ys, pos1, pos2, h2, info, g_final)
    return out.reshape(bsz, seq, d)
```

```python
import functools

import jax
import jax.numpy as jnp
import numpy as np
from jax import lax
from jax.experimental import pallas as pl
from jax.experimental.pallas import tpu as pltpu

F32 = jnp.float32
BF16 = jnp.bfloat16

D_MODEL = 2048
MEM_LEN = 256
RMS_EPS = 1e-6
MLA_HEADS = 16
Q_LORA = 512
KV_LORA = 256
NOPE_DIM = 128
ROPE_DIM = 64
V_DIM = 128
QK_DIM = NOPE_DIM + ROPE_DIM
ROPE_BASE = 10000.0
RWKV_HEAD = 64
RWKV_HEADS = D_MODEL // RWKV_HEAD
RWKV_WIDTH = D_MODEL
DECAY_LORA = 96
ICLR_LORA = 96
GATE_LORA = 256
LNX_EPS = 64e-5
CROSS_HEADS = 4
CROSS_HEAD_DIM = 128
CROSS_WIDTH = CROSS_HEADS * CROSS_HEAD_DIM
N_GROUPS = 4
EXPERTS_PER_GROUP = 8
N_EXPERTS = N_GROUPS * EXPERTS_PER_GROUP
MLA_IN = Q_LORA + KV_LORA + ROPE_DIM
RWKV_IN = 3 * RWKV_WIDTH + 2 * DECAY_LORA + 2 * ICLR_LORA + GATE_LORA

LANES = 128
SUBLANES = 8
QK_PAD = 256
LORA_PAD = LANES
HEAD_SHIFT = RWKV_HEAD.bit_length() - 1
GROUP_SHIFT = EXPERTS_PER_GROUP.bit_length() - 1
QUAD = 4
QW = QUAD * RWKV_HEAD
CHUNK = 64
ATTN_TK = 1024
MOE_TILE = 512
GATHER_UNROLL = 32
LOG2E = 1.4426950408889634
VMEM_LIMIT = 56 * 1024 * 1024


def _cparams(sem, vmem=VMEM_LIMIT, flags=None):
    return pltpu.CompilerParams(dimension_semantics=sem, vmem_limit_bytes=vmem, flags=flags)


def _iota(shape, dim):
    return lax.broadcasted_iota(jnp.int32, shape, dim)


def _sigmoid(x):
    return 1.0 / (1.0 + jnp.exp(-x))


def _dot(a, b):
    return jnp.dot(a, b, preferred_element_type=F32)


def _dot_nt(a, b):
    return lax.dot_general(a, b, (((1,), (1,)), ((), ())), preferred_element_type=F32)


def _dot_tn(a, b):
    return lax.dot_general(a, b, (((0,), (0,)), ((), ())), preferred_element_type=F32)


def _head_ones():
    r = lax.shift_right_logical(_iota((LANES, LANES), 0), HEAD_SHIFT)
    c = lax.shift_right_logical(_iota((LANES, LANES), 1), HEAD_SHIFT)
    return jnp.where(r == c, 1.0, 0.0).astype(BF16)


def _head_sum(x, ones_bd):
    w = ones_bd.shape[0]
    hi = x.astype(BF16)
    lo = (x - hi.astype(F32)).astype(BF16)
    parts = [_dot(hi[:, j:j + w], ones_bd) + _dot(lo[:, j:j + w], ones_bd) for j in range(0, x.shape[1], w)]
    return parts[0] if len(parts) == 1 else jnp.concatenate(parts, axis=1)


def _rmsnorm_kernel(x_ref, g_ref, o_ref):
    x = x_ref[...]
    y = x * lax.rsqrt(jnp.mean(x * x, axis=-1, keepdims=True) + RMS_EPS)
    o_ref[...] = (y * g_ref[...]).astype(o_ref.dtype)


def _rmsnorm(x, g, tm=512):
    m, d = x.shape
    return pl.pallas_call(
        _rmsnorm_kernel,
        grid=(m // tm,),
        in_specs=[pl.BlockSpec((tm, d), lambda i: (i, 0)),
                  pl.BlockSpec((1, d), lambda i: (0, 0))],
        out_specs=pl.BlockSpec((tm, d), lambda i: (i, 0)),
        out_shape=jax.ShapeDtypeStruct((m, d), BF16),
        compiler_params=_cparams(("parallel",)),
        name="rmsnorm",
    )(x, g.reshape(1, d))


def _mm_kernel(a_ref, b_ref, o_ref):
    o_ref[...] = _dot(a_ref[...], b_ref[...]).astype(o_ref.dtype)


def _mm_res_kernel(a_ref, b_ref, r_ref, o_ref):
    o_ref[...] = (r_ref[...] + _dot(a_ref[...], b_ref[...])).astype(o_ref.dtype)


def _matmul(a, b, out_dtype, tm, tn, res=None, name="matmul"):
    m, k = a.shape
    n = b.shape[1]
    in_specs = [pl.BlockSpec((tm, k), lambda i, j: (i, 0)),
                pl.BlockSpec((k, tn), lambda i, j: (0, j))]
    args = [a, b]
    kern = _mm_kernel
    if res is not None:
        in_specs.append(pl.BlockSpec((tm, tn), lambda i, j: (i, j)))
        args.append(res)
        kern = _mm_res_kernel
    return pl.pallas_call(
        kern,
        grid=(m // tm, n // tn),
        in_specs=in_specs,
        out_specs=pl.BlockSpec((tm, tn), lambda i, j: (i, j)),
        out_shape=jax.ShapeDtypeStruct((m, n), out_dtype),
        compiler_params=_cparams(("parallel", "arbitrary")),
        name=name,
    )(*args)


def _mla_proj_kernel(z_ref, pos_ref, invf_ref, gq_ref, gkv_ref, wq_ref, wkv_ref,
                     q_ref, k_ref, v_ref):
    tm = z_ref.shape[0]
    z = z_ref[...]

    def norm(c, g):
        return (c * lax.rsqrt(jnp.mean(c * c, axis=-1, keepdims=True) + RMS_EPS) * g).astype(BF16)

    cq = norm(z[:, :Q_LORA], gq_ref[...])
    ckv = norm(z[:, Q_LORA:Q_LORA + KV_LORA], gkv_ref[...])
    q = _dot(cq, wq_ref[...]) * (QK_DIM ** -0.5 * LOG2E)
    kv = _dot(ckv, wkv_ref[...])

    ang = pos_ref[...].astype(F32) * invf_ref[...]
    lane = _iota((tm, LANES), 1)
    half = ROPE_DIM // 2
    cos, sin = jnp.cos(ang), jnp.sin(ang)
    c_tab = jnp.where(lane < ROPE_DIM, cos, 0.0)
    s_up = jnp.where((lane >= half) & (lane < ROPE_DIM), sin, 0.0)
    s_dn = jnp.where(lane < half, -sin, 0.0)

    def rope(x):
        return (x * c_tab + pltpu.roll(x, half, 1) * s_up
                + pltpu.roll(x, LANES - half, 1) * s_dn)

    k_r = rope(z[:, Q_LORA + KV_LORA:]).astype(BF16)
    for h in range(MLA_HEADS):
        lo = h * QK_PAD
        q_ref[:, lo:lo + NOPE_DIM] = q[:, lo:lo + NOPE_DIM].astype(BF16)
        q_ref[:, lo + NOPE_DIM:lo + QK_PAD] = rope(q[:, lo + NOPE_DIM:lo + QK_PAD]).astype(BF16)
        k_ref[:, lo:lo + NOPE_DIM] = kv[:, h * NOPE_DIM:(h + 1) * NOPE_DIM].astype(BF16)
        k_ref[:, lo + NOPE_DIM:lo + QK_PAD] = k_r
    v_ref[...] = kv[:, MLA_HEADS * NOPE_DIM:].astype(BF16)


def _mla_proj(z_mla, pos, invf, g_q, g_kv, wq, wkv, tm=256):
    m, w = z_mla.shape
    full = lambda shape: pl.BlockSpec(shape, lambda i: (0, 0))
    row = lambda n: pl.BlockSpec((tm, n), lambda i: (i, 0))
    return pl.pallas_call(
        _mla_proj_kernel,
        grid=(m // tm,),
        in_specs=[row(w), row(1), full((1, LANES)), full((1, Q_LORA)), full((1, KV_LORA)),
                  full(wq.shape), full(wkv.shape)],
        out_specs=[row(MLA_HEADS * QK_PAD), row(MLA_HEADS * QK_PAD), row(MLA_HEADS * V_DIM)],
        out_shape=[jax.ShapeDtypeStruct((m, MLA_HEADS * QK_PAD), BF16),
                   jax.ShapeDtypeStruct((m, MLA_HEADS * QK_PAD), BF16),
                   jax.ShapeDtypeStruct((m, MLA_HEADS * V_DIM), BF16)],
        compiler_params=_cparams(("parallel",)),
        name="mla_proj",
    )(z_mla, pos, invf, g_q.reshape(1, -1), g_kv.reshape(1, -1), wq, wkv)


def _mla_attn_kernel(q_ref, k_ref, v_ref, o_ref):
    tq = q_ref.shape[0]
    nk = k_ref.shape[0] // ATTN_TK
    q = q_ref[...]

    def scores(j):
        return _dot_nt(q, k_ref[j * ATTN_TK:(j + 1) * ATTN_TK, :])

    m = jnp.full((tq, 1), -jnp.inf, F32)
    l = jnp.zeros((tq, 1), F32)
    acc = jnp.zeros((tq, V_DIM), F32)
    s_next = scores(0)
    for j in range(nk):
        s = s_next
        if j + 1 < nk:
            s_next = scores(j + 1)
        m_new = jnp.maximum(m, jnp.max(s, axis=-1, keepdims=True))
        alpha = jnp.exp2(m - m_new)
        p = jnp.exp2(s - m_new)
        l = alpha * l + jnp.sum(p, axis=-1, keepdims=True)
        acc = alpha * acc + _dot(p.astype(BF16), v_ref[j * ATTN_TK:(j + 1) * ATTN_TK, :])
        m = m_new
    o_ref[...] = (acc / l).astype(o_ref.dtype)


def _mla_attn(q, k, v, tq=1024):
    b, s, _ = q.shape
    return pl.pallas_call(
        _mla_attn_kernel,
        grid=(b, MLA_HEADS, s // tq),
        in_specs=[pl.BlockSpec((None, tq, QK_PAD), lambda b_, h, i: (b_, i, h)),
                  pl.BlockSpec((None, s, QK_PAD), lambda b_, h, i: (b_, 0, h)),
                  pl.BlockSpec((None, s, V_DIM), lambda b_, h, i: (b_, 0, h))],
        out_specs=pl.BlockSpec((None, tq, V_DIM), lambda b_, h, i: (b_, i, h)),
        out_shape=jax.ShapeDtypeStruct((b, s, MLA_HEADS * V_DIM), BF16),
        compiler_params=_cparams(("parallel", "parallel", "arbitrary")),
        name="mla_attn",
    )(q, k, v)


def _rwkv_prep_kernel(seq_len,
                      r_ref, rp_ref, rn_ref, k_ref, kp_ref, kn_ref, v_ref, vp_ref, vn_ref,
                      l_ref, lp_ref, ln_ref, scr_ref, sck_ref, scv_ref, scl_ref,
                      w2f_ref, w2b_ref, a2f_ref, a2b_ref, g2_ref,
                      w0f_ref, w0b_ref, a0f_ref, a0b_ref, kk_w_ref,
                      ro_ref, ko_ref, vo_ref, kko_ref, lwf_ref, lwb_ref, alf_ref, alb_ref, g_ref):
    tm = r_ref.shape[0]
    i = pl.program_id(0)
    first = lax.rem(i * tm, seq_len) == 0
    last = lax.rem((i + 1) * tm, seq_len) == 0

    def shift(z_ref, zp_ref, zn_ref, w_ref):
        z = z_ref[...]
        rows = _iota(z.shape, 0)
        prev_row = jnp.where(first, 0.0, zp_ref[SUBLANES - 1:SUBLANES, :])
        next_row = jnp.where(last, 0.0, zn_ref[0:1, :])
        z_prev = jnp.where(rows == 0, prev_row, pltpu.roll(z, 1, 0))
        z_next = jnp.where(rows == tm - 1, next_row, pltpu.roll(z, tm - 1, 0))
        return w_ref[0:1, :] * z_prev + w_ref[1:2, :] * z + w_ref[2:3, :] * z_next

    r = shift(r_ref, rp_ref, rn_ref, scr_ref)
    k = shift(k_ref, kp_ref, kn_ref, sck_ref)
    v = shift(v_ref, vp_ref, vn_ref, scv_ref)
    lo = shift(l_ref, lp_ref, ln_ref, scl_ref)
    xw_f = lo[:, 0 * LORA_PAD:1 * LORA_PAD]
    xw_b = lo[:, 1 * LORA_PAD:2 * LORA_PAD]
    xa_f = lo[:, 2 * LORA_PAD:3 * LORA_PAD]
    xa_b = lo[:, 3 * LORA_PAD:4 * LORA_PAD]
    xg = lo[:, 4 * LORA_PAD:]

    def log_decay(xw, w0_ref, w2_ref):
        y = -(w0_ref[...] + _dot(jnp.tanh(xw).astype(BF16), w2_ref[...]))
        softplus = jnp.maximum(y, 0.0) + jnp.log(1.0 + jnp.exp(-jnp.abs(y)))
        return -jnp.exp(-softplus - 0.5)

    def rate(xa, a0_ref, a2_ref):
        return _sigmoid(a0_ref[...] + _dot(xa.astype(BF16), a2_ref[...]))

    al_f = rate(xa_f, a0f_ref, a2f_ref)
    al_b = rate(xa_b, a0b_ref, a2b_ref)
    kk = k * kk_w_ref[...]
    kk = kk * lax.rsqrt(_head_sum(kk * kk, _head_ones()) + 1e-12)

    ro_ref[...] = r
    ko_ref[...] = k
    vo_ref[...] = v
    kko_ref[...] = kk
    lwf_ref[...] = log_decay(xw_f, w0f_ref, w2f_ref)
    lwb_ref[...] = log_decay(xw_b, w0b_ref, w2b_ref)
    alf_ref[...] = al_f
    alb_ref[...] = al_b
    g_ref[...] = _dot(_sigmoid(xg).astype(BF16), g2_ref[...])


def _rwkv_prep(z_rkv, z_lora, seq_len, sc_rkv, sc_lora, w2f, w2b, a2f, a2b, g2,
               w0f, w0b, a0f, a0b, k_k, tm=512, tw=512):
    m = z_rkv.shape[0]
    wl = z_lora.shape[1]
    nb = RWKV_WIDTH // tw
    rb = tm // SUBLANES
    nrb = m // SUBLANES

    def main(seg):
        return pl.BlockSpec((tm, tw), lambda i, p: (i, seg * nb + p))

    def prev(seg):
        return pl.BlockSpec((SUBLANES, tw),
                            lambda i, p: (jnp.maximum(i * rb - 1, 0), seg * nb + p))

    def nxt(seg):
        return pl.BlockSpec((SUBLANES, tw),
                            lambda i, p: (jnp.minimum((i + 1) * rb, nrb - 1), seg * nb + p))

    def sc(seg):
        return pl.BlockSpec((3, tw), lambda i, p: (0, seg * nb + p))

    colblk = lambda rows: pl.BlockSpec((rows, tw), lambda i, p: (0, p))
    in_specs = []
    for seg in range(3):
        in_specs += [main(seg), prev(seg), nxt(seg)]
    in_specs += [pl.BlockSpec((tm, wl), lambda i, p: (i, 0)),
                 pl.BlockSpec((SUBLANES, wl), lambda i, p: (jnp.maximum(i * rb - 1, 0), 0)),
                 pl.BlockSpec((SUBLANES, wl), lambda i, p: (jnp.minimum((i + 1) * rb, nrb - 1), 0))]
    in_specs += [sc(0), sc(1), sc(2), pl.BlockSpec((3, wl), lambda i, p: (0, 0))]
    in_specs += [colblk(LORA_PAD)] * 4 + [colblk(GATE_LORA)] + [colblk(1)] * 5
    out_spec = pl.BlockSpec((tm, tw), lambda i, p: (i, p))
    n_out = 9
    row = lambda a: a.reshape(1, -1)
    return pl.pallas_call(
        functools.partial(_rwkv_prep_kernel, seq_len),
        grid=(m // tm, nb),
        in_specs=in_specs,
        out_specs=[out_spec] * n_out,
        out_shape=[jax.ShapeDtypeStruct((m, RWKV_WIDTH), F32)] * n_out,
        compiler_params=_cparams(("parallel", "arbitrary")),
        name="rwkv_prep",
    )(z_rkv, z_rkv, z_rkv, z_rkv, z_rkv, z_rkv, z_rkv, z_rkv, z_rkv,
      z_lora, z_lora, z_lora, sc_rkv, sc_rkv, sc_rkv, sc_lora,
      w2f, w2b, a2f, a2b, g2, row(w0f), row(w0b), row(a0f), row(a0b), row(k_k))


def _scan_chunks(chains):
    c = CHUNK
    nc = range(len(chains))
    r, v, kk, lw, al, k, rk, ka, ht, rev = [list(t) for t in zip(*chains)]
    t_i = _iota((c, c), 0)
    s_i = _iota((c, c), 1)
    tri = {False: jnp.where(s_i <= t_i, 1.0, 0.0).astype(BF16),
           True: jnp.where(s_i >= t_i, 1.0, 0.0).astype(BF16)}
    row = _iota((c, QW), 0)
    col = jnp.bitwise_and(_iota((c, QW), 1), RWKV_HEAD - 1)
    strict = {False: col < row, True: col > row}
    incl = {False: col <= row, True: col >= row}
    eye = jnp.where(row == col, 1.0, 0.0)
    lane_head = lax.shift_right_logical(_iota((c, QW), 1), HEAD_SHIFT)
    head_mask = [lane_head == h for h in range(QUAD)]
    on_diag = (lax.shift_right_logical(_iota((QW, QW), 0), HEAD_SHIFT)
               == lax.shift_right_logical(_iota((QW, QW), 1), HEAD_SHIFT))

    def blockdiag(x):
        xb = x.astype(BF16)
        zero = jnp.zeros_like(xb)
        return jnp.concatenate([jnp.where(head_mask[h], xb, zero) for h in range(QUAD)], axis=0)

    def cat(x, y, axis=0):
        return jnp.concatenate([x, y], axis=axis)

    a = [-kk[i] for i in nc]
    b = [kk[i] * al[i] for i in nc]
    kd = [k[i] * (1.0 + (al[i] - 1.0) * ka[i]) for i in nc]
    lw_hi = [lw[i].astype(BF16) for i in nc]
    lw_lo = [(lw[i] - lw_hi[i].astype(F32)).astype(BF16) for i in nc]
    cum = [_dot(tri[rev[i]], lw_hi[i]) + _dot(tri[rev[i]], lw_lo[i]) for i in nc]
    cum_prev = [cum[i] - lw[i] for i in nc]
    ref_row = [c // 2 if rev[i] else c // 2 - 1 for i in nc]
    tot_row = [0 if rev[i] else c - 1 for i in nc]
    c_ref = [cum[i][ref_row[i]:ref_row[i] + 1, :] for i in nc]
    c_tot = [cum[i][tot_row[i]:tot_row[i] + 1, :] for i in nc]
    e_inv = [jnp.exp(c_ref[i] - cum[i]) for i in nc]
    e_out = [jnp.exp(c_tot[i] - cum[i]) for i in nc]
    e_cur = [jnp.exp(cum[i] - c_ref[i]) for i in nc]
    lhs = [jnp.concatenate([a[i] * jnp.exp(cum_prev[i] - c_ref[i]), r[i] * e_cur[i], r[i] * rk[i] * e_cur[i]],
                           axis=0).astype(BF16) for i in nc]
    rhs = [cat(blockdiag(b[i] * e_inv[i]), blockdiag(kd[i] * e_inv[i])) for i in nc]
    nmat = [_dot_nt(lhs[i], rhs[i]) for i in nc]
    n_ab = [jnp.where(strict[rev[i]], nmat[i][:c, :QW], 0.0) for i in nc]
    n_ak = [jnp.where(strict[rev[i]], nmat[i][:c, QW:], 0.0) for i in nc]
    n_rb = [jnp.where(incl[rev[i]], nmat[i][c:2 * c, :QW], 0.0).astype(BF16) for i in nc]
    n_rk = [jnp.where(incl[rev[i]], nmat[i][c:2 * c, QW:], 0.0) + jnp.where(row == col, nmat[i][2 * c:, QW:], 0.0)
            for i in nc]

    n_sq = int(np.log2(c)) - 1
    pw = n_ab
    tinv = [eye + pw[i] for i in nc]
    pw = [_dot(pw[i].astype(BF16), blockdiag(pw[i])) for i in nc]
    for j in range(n_sq):
        if j + 1 < n_sq:
            both = [_dot(cat(tinv[i], pw[i]).astype(BF16), blockdiag(pw[i])) for i in nc]
            tinv = [tinv[i] + both[i][:c] for i in nc]
            pw = [both[i][c:] for i in nc]
        else:
            tinv = [tinv[i] + _dot(tinv[i].astype(BF16), blockdiag(pw[i])) for i in nc]

    v_bd = [blockdiag(v[i]) for i in nc]
    state_lhs = [cat(a[i] * jnp.exp(cum_prev[i]), r[i] * jnp.exp(cum[i])).astype(BF16) for i in nc]
    from_state = [_dot_nt(state_lhs[i], ht[i].astype(BF16)) for i in nc]
    from_v = [_dot(cat(n_ak[i], n_rk[i]).astype(BF16), v_bd[i]) for i in nc]
    u = [_dot(tinv[i].astype(BF16), blockdiag(from_state[i][:c] + from_v[i][:c])) for i in nc]
    o = [from_state[i][c:] + from_v[i][c:] + _dot(n_rb[i], blockdiag(u[i])) for i in nc]
    u_b = [u[i].astype(BF16) for i in nc]
    v_b = [v[i].astype(BF16) for i in nc]
    upd = [_dot_tn(u_b[i], (b[i] * e_out[i]).astype(BF16)) + _dot_tn(v_b[i], (kd[i] * e_out[i]).astype(BF16))
           for i in nc]
    ht_new = [ht[i] * jnp.exp(c_tot[i]) + jnp.where(on_diag, upd[i], 0.0) for i in nc]
    return [(o[i], ht_new[i]) for i in nc]


def _rwkv_scan_kernel(rf, vf, kkf, lwf, alf, kf, rb, vb, kkb, lwb, alb, kb, rk_ref, ka_ref,
                      yf_ref, yb_ref, h_ref):
    @pl.when(pl.program_id(2) == 0)
    def _():
        h_ref[...] = jnp.zeros_like(h_ref)

    n_chunks = rf.shape[0] // CHUNK
    n_groups = rf.shape[1] // QW

    def body(ci, carry):
        chains = []
        for d, (refs, y_ref) in enumerate((((rf, vf, kkf, lwf, alf, kf), yf_ref),
                                           ((rb, vb, kkb, lwb, alb, kb), yb_ref))):
            cidx = ci if d == 0 else n_chunks - 1 - ci
            rows = pl.ds(pl.multiple_of(cidx * CHUNK, CHUNK), CHUNK)
            for g in range(n_groups):
                lanes = slice(g * QW, (g + 1) * QW)
                vals = [x[rows, lanes] for x in refs] + [rk_ref[:, lanes], ka_ref[:, lanes], h_ref[d, g]]
                chains.append((d, g, y_ref, rows, lanes, vals))
        results = _scan_chunks([tuple(vals) + (d == 1,) for d, _, _, _, _, vals in chains])
        for (d, g, y_ref, rows, lanes, _), (y, ht) in zip(chains, results):
            y_ref[rows, lanes] = y
            h_ref[d, g] = ht
        return carry

    lax.fori_loop(0, n_chunks, body, 0)


def _rwkv_scan(r, k, v, kk, lw_f, lw_b, al_f, al_b, r_k, k_a, tb=128, groups=8):
    b, s, w = r.shape
    nblk = s // tb
    wb = groups * QW
    fwd = pl.BlockSpec((None, tb, wb), lambda b_, p, j: (b_, j, p))
    bwd = pl.BlockSpec((None, tb, wb), lambda b_, p, j: (b_, nblk - 1 - j, p))
    vec = pl.BlockSpec((1, wb), lambda b_, p, j: (0, p))
    return pl.pallas_call(
        _rwkv_scan_kernel,
        grid=(b, w // wb, nblk),
        in_specs=[fwd] * 6 + [bwd] * 6 + [vec, vec],
        out_specs=[fwd, bwd],
        out_shape=[jax.ShapeDtypeStruct((b, s, w), F32)] * 2,
        scratch_shapes=[pltpu.VMEM((2, groups, QW, QW), F32)],
        compiler_params=_cparams(("parallel", "parallel", "arbitrary")),
        name="rwkv_scan",
    )(r, v, kk, lw_f, al_f, k, r, v, kk, lw_b, al_b, k, r_k.reshape(1, -1), k_a.reshape(1, -1))


def _rwkv_post_kernel(yf_ref, yb_ref, g_ref, lg_ref, lb_ref, o_ref):
    ones_bd = _head_ones()
    y = yf_ref[...] + yb_ref[...]
    mu = _head_sum(y, ones_bd) * (1.0 / RWKV_HEAD)
    d = y - mu
    var = _head_sum(d * d, ones_bd) * (1.0 / RWKV_HEAD)
    yn = d * lax.rsqrt(var + LNX_EPS) * lg_ref[...] + lb_ref[...]
    o_ref[...] = (yn * g_ref[...]).astype(o_ref.dtype)


def _rwkv_post(y_f, y_b, g, lnx_g, lnx_b, tm=512, tw=512):
    m, w = y_f.shape
    blk = pl.BlockSpec((tm, tw), lambda i, p: (i, p))
    vec = pl.BlockSpec((1, tw), lambda i, p: (0, p))
    return pl.pallas_call(
        _rwkv_post_kernel,
        grid=(m // tm, w // tw),
        in_specs=[blk, blk, blk, vec, vec],
        out_specs=blk,
        out_shape=jax.ShapeDtypeStruct((m, w), BF16),
        compiler_params=_cparams(("parallel", "parallel")),
        name="rwkv_post",
    )(y_f, y_b, g, lnx_g.reshape(1, -1), lnx_b.reshape(1, -1))


def _merge_kernel(a1_ref, w1_ref, a2_ref, w2_ref, g1_ref, g2_ref, o_ref):
    m1 = _dot(a1_ref[...], w1_ref[...])
    m2 = _dot(a2_ref[...], w2_ref[...])
    o_ref[...] = (_sigmoid(g1_ref[...]) * m1 + _sigmoid(g2_ref[...]) * m2).astype(o_ref.dtype)


def _merge(o_mla, w_up_mla, o_rwkv, w_up_rwkv, z_gate, tm=1024, tn=512):
    m, k = o_mla.shape
    n = w_up_mla.shape[1]
    nj = n // tn
    a_spec = pl.BlockSpec((tm, k), lambda i, j: (i, 0))
    w_spec = pl.BlockSpec((k, tn), lambda i, j: (0, j))
    return pl.pallas_call(
        _merge_kernel,
        grid=(m // tm, nj),
        in_specs=[a_spec, w_spec, a_spec, w_spec,
                  pl.BlockSpec((tm, tn), lambda i, j: (i, j)),
                  pl.BlockSpec((tm, tn), lambda i, j: (i, nj + j))],
        out_specs=pl.BlockSpec((tm, tn), lambda i, j: (i, j)),
        out_shape=jax.ShapeDtypeStruct((m, n), BF16),
        compiler_params=_cparams(("parallel", "arbitrary")),
        name="merge",
    )(o_mla, w_up_mla, o_rwkv, w_up_rwkv, z_gate, z_gate)


def _cross_router_kernel(h_ref, gc_ref, wq_ref, kv_ref, wo_ref, gf_ref, wr_ref, br_ref,
                         h2_ref, n3_ref, info_ref, cnt_ref, carry_ref):
    @pl.when(pl.program_id(0) == 0)
    def _():
        carry_ref[...] = jnp.zeros_like(carry_ref)

    h = h_ref[...]
    hn = (h * lax.rsqrt(jnp.mean(h * h, axis=-1, keepdims=True) + RMS_EPS) * gc_ref[...]).astype(BF16)
    q = (_dot(hn, wq_ref[...]) * (CROSS_HEAD_DIM ** -0.5)).astype(BF16)
    kv = kv_ref[...]
    outs = []
    for hd in range(CROSS_HEADS):
        lo = hd * CROSS_HEAD_DIM
        s = _dot_nt(q[:, lo:lo + CROSS_HEAD_DIM], kv[:, lo:lo + CROSS_HEAD_DIM])
        p = jnp.exp(s - jnp.max(s, axis=-1, keepdims=True))
        p = p / jnp.sum(p, axis=-1, keepdims=True)
        outs.append(_dot(p.astype(BF16), kv[:, CROSS_WIDTH + lo:CROSS_WIDTH + lo + CROSS_HEAD_DIM]))
    o = jnp.concatenate(outs, axis=-1).astype(BF16)
    h2 = h + _dot(o, wo_ref[...])
    h2_ref[...] = h2

    n3 = h2 * lax.rsqrt(jnp.mean(h2 * h2, axis=-1, keepdims=True) + RMS_EPS) * gf_ref[...]
    n3_ref[...] = n3

    wr = wr_ref[...]
    n_hi, w_hi = n3.astype(BF16), wr.astype(BF16)
    n_lo = (n3 - n_hi.astype(F32)).astype(BF16)
    w_lo = (wr - w_hi.astype(F32)).astype(BF16)
    logits = _dot(n_hi, w_hi) + (_dot(n_hi, w_lo) + _dot(n_lo, w_hi)) + br_ref[...]
    lane = _iota(logits.shape, 1)
    lane_f = lane.astype(F32)
    neg = jnp.float32(-jnp.inf)
    big = jnp.float32(1e9)

    def masked_softmax(mask):
        x = jnp.where(mask, logits, neg)
        e = jnp.exp(x - jnp.max(x, axis=-1, keepdims=True))
        return e / jnp.sum(e, axis=-1, keepdims=True)

    def top1(prob, mask):
        pmax = jnp.max(jnp.where(mask, prob, -1.0), axis=-1, keepdims=True)
        idx = jnp.min(jnp.where(mask & (prob == pmax), lane_f, big), axis=-1, keepdims=True)
        return pmax, idx

    g_mask = (lane >= N_EXPERTS) & (lane < N_EXPERTS + N_GROUPS)
    p_group, g_idx = top1(masked_softmax(g_mask), g_mask)
    g_sel = g_idx - float(N_EXPERTS)
    e_mask = (lane < N_EXPERTS) & (lax.shift_right_logical(lane, GROUP_SHIFT).astype(F32) == g_sel)
    e_prob = masked_softmax(e_mask)
    p1, i1 = top1(e_prob, e_mask)
    rest = e_mask & (lane_f != i1)
    p2, i2 = top1(e_prob, rest)
    denom = p1 + p2
    w1 = p_group * (p1 / denom)
    w2 = p_group * (p2 / denom)

    tm = logits.shape[0]
    oh1 = jnp.where(lane_f == i1, 1.0, 0.0)
    oh2 = jnp.where(lane_f == i2, 1.0, 0.0)
    before = jnp.where(_iota((tm, tm), 1) < _iota((tm, tm), 0), 1.0, 0.0).astype(BF16)
    carry = carry_ref[...]
    cnt1 = jnp.sum(oh1, axis=0, keepdims=True)
    cnt2 = jnp.sum(oh2, axis=0, keepdims=True)
    rank1 = jnp.sum(oh1 * (carry + _dot(before, oh1.astype(BF16))), axis=-1, keepdims=True)
    rank2 = jnp.sum(oh2 * (carry + cnt1 + _dot(before, oh2.astype(BF16))), axis=-1, keepdims=True)
    carry = carry + cnt1 + cnt2
    carry_ref[...] = carry
    cnt_ref[...] = carry
    info = jnp.zeros_like(logits)
    for k, val in enumerate((i1, i2, w1, w2, rank1, rank2)):
        info = jnp.where(lane == k, val, info)
    info_ref[...] = info


def _cross_router(h1, seq_len, g_cross, wq, kvm, wo, g_ffn, w_r, b_r, tm=512):
    m, d = h1.shape
    full = lambda a: pl.BlockSpec(a.shape, lambda i: (0,) * a.ndim)
    row = lambda n: pl.BlockSpec((tm, n), lambda i: (i, 0))
    gc, gf = g_cross.reshape(1, d), g_ffn.reshape(1, d)
    per_seq = seq_len // tm
    return pl.pallas_call(
        _cross_router_kernel,
        grid=(m // tm,),
        in_specs=[row(d), full(gc), full(wq),
                  pl.BlockSpec((None,) + kvm.shape[1:], lambda i: (i // per_seq, 0, 0)),
                  full(wo), full(gf), full(w_r), full(b_r)],
        out_specs=[row(d), row(d), row(LANES), pl.BlockSpec((1, LANES), lambda i: (0, 0))],
        out_shape=[jax.ShapeDtypeStruct((m, d), F32), jax.ShapeDtypeStruct((m, d), F32),
                   jax.ShapeDtypeStruct((m, LANES), F32), jax.ShapeDtypeStruct((1, LANES), F32)],
        scratch_shapes=[pltpu.VMEM((1, LANES), F32)],
        compiler_params=_cparams(("arbitrary",)),
        name="cross_router",
    )(h1, gc, wq, kvm, wo, gf, w_r, b_r)


def _gather_rows(src_hbm, idx_ref, base, dst, sem, n_rows):
    def issue(g, carry):
        for u in range(GATHER_UNROLL):
            r = g * GATHER_UNROLL + u
            src_row = idx_ref[base + r]
            pltpu.make_async_copy(src_hbm.at[pl.ds(src_row, 1), :], dst.at[pl.ds(r, 1), :], sem).start()
        return carry
    lax.fori_loop(0, n_rows // GATHER_UNROLL, issue, 0)


def _wait_rows(src_hbm, dst, sem, n_rows):
    pltpu.make_async_copy(src_hbm.at[pl.ds(0, n_rows), :], dst, sem).wait()


def _moe_dispatch_kernel(pos1, pos2, x_ref, xs_zero, xs_ref, sem):
    del xs_zero
    i = pl.program_id(0)
    tm = x_ref.shape[0]

    def issue(g, carry):
        for u in range(GATHER_UNROLL):
            r = g * GATHER_UNROLL + u
            src = x_ref.at[pl.ds(r, 1), :]
            pltpu.make_async_copy(src, xs_ref.at[pl.ds(pos1[i * tm + r], 1), :], sem.at[0]).start()
            pltpu.make_async_copy(src, xs_ref.at[pl.ds(pos2[i * tm + r], 1), :], sem.at[1]).start()
        return carry

    lax.fori_loop(0, tm // GATHER_UNROLL, issue, 0)
    for k in range(2):
        pltpu.make_async_copy(x_ref, xs_ref.at[pl.ds(0, tm), :], sem.at[k]).wait()


def _moe_dispatch(n3p, pos1, pos2, p_max, tm=512):
    m, w = n3p.shape
    grid_spec = pltpu.PrefetchScalarGridSpec(
        num_scalar_prefetch=2,
        grid=(m // tm,),
        in_specs=[pl.BlockSpec((tm, w), lambda i, p1, p2: (i, 0)),
                  pl.BlockSpec(memory_space=pl.ANY)],
        out_specs=pl.BlockSpec(memory_space=pl.ANY),
        scratch_shapes=[pltpu.SemaphoreType.DMA((2,))],
    )
    return pl.pallas_call(
        _moe_dispatch_kernel,
        grid_spec=grid_spec,
        out_shape=jax.ShapeDtypeStruct((p_max, w), n3p.dtype),
        input_output_aliases={3: 0},
        compiler_params=_cparams(("arbitrary",)),
        name="moe_dispatch",
    )(pos1, pos2, n3p, jnp.zeros((p_max, w), n3p.dtype))


def _tile_expert(t, tile_end):
    t = jnp.minimum(t, tile_end[N_EXPERTS - 1] - 1)
    e = jnp.int32(0)
    for k in range(N_EXPERTS - 1):
        e = e + (tile_end[k] <= t).astype(jnp.int32)
    return e


def _moe_expert_kernel(tile_end, x_ref, wg_hbm, wu_hbm, wd_hbm, y_ref,
                       wg_s, wu_s, wd_s, wg_b, wu_b, wd_b, slot_ref, sem):
    t = pl.program_id(0)
    n_valid = tile_end[N_EXPERTS - 1]
    e_cur = _tile_expert(t, tile_end)
    changed = jnp.logical_or(t == 0, e_cur != _tile_expert(jnp.maximum(t - 1, 0), tile_end))

    def copies(e, s):
        return [pltpu.make_async_copy(hbm.at[e], stage.at[s], sem.at[k, s])
                for k, (hbm, stage) in enumerate(((wg_hbm, wg_s), (wu_hbm, wu_s), (wd_hbm, wd_s)))]

    @pl.when(t == 0)
    def _():
        slot_ref[0] = 0
        for c in copies(e_cur, 0):
            c.start()

    @pl.when(changed)
    def _():
        s = slot_ref[0]
        for c in copies(e_cur, s):
            c.wait()
        t_next = tile_end[e_cur]

        @pl.when(t_next < n_valid)
        def _():
            for c in copies(_tile_expert(t_next, tile_end), 1 - s):
                c.start()

        wg_b[...] = wg_s[s].astype(BF16)
        wu_b[...] = wu_s[s].astype(BF16)
        wd_b[...] = wd_s[s].astype(BF16)
        slot_ref[0] = 1 - s

    @pl.when(t < n_valid)
    def _():
        x = x_ref[...].astype(BF16)
        hg = _dot(x, wg_b[...])
        hu = _dot(x, wu_b[...])
        hid = (hg * _sigmoid(hg) * hu).astype(BF16)
        y_ref[...] = _dot(hid, wd_b[...])

    @pl.when(t >= n_valid)
    def _():
        y_ref[...] = jnp.zeros_like(y_ref)


def _moe_experts(xs, tile_end, w_eg, w_eu, w_ed):
    p_max, w = xs.shape
    ne, d, f = w_eg.shape
    last = lambda t, te: jnp.minimum(t, te[N_EXPERTS - 1] - 1)
    grid_spec = pltpu.PrefetchScalarGridSpec(
        num_scalar_prefetch=1,
        grid=(p_max // MOE_TILE,),
        in_specs=[pl.BlockSpec((MOE_TILE, w), lambda t, te: (last(t, te), 0)),
                  pl.BlockSpec(memory_space=pl.ANY), pl.BlockSpec(memory_space=pl.ANY),
                  pl.BlockSpec(memory_space=pl.ANY)],
        out_specs=pl.BlockSpec((MOE_TILE, d), lambda t, te: (t, 0)),
        scratch_shapes=[pltpu.VMEM((2, d, f), F32), pltpu.VMEM((2, d, f), F32), pltpu.VMEM((2, f, d), F32),
                        pltpu.VMEM((d, f), BF16), pltpu.VMEM((d, f), BF16), pltpu.VMEM((f, d), BF16),
                        pltpu.SMEM((1,), jnp.int32), pltpu.SemaphoreType.DMA((3, 2))],
    )
    return pl.pallas_call(
        _moe_expert_kernel,
        grid_spec=grid_spec,
        out_shape=jax.ShapeDtypeStruct((p_max, d), F32),
        compiler_params=_cparams(("arbitrary",)),
        name="moe_experts",
    )(tile_end, xs, w_eg, w_eu, w_ed)


def _moe_combine_kernel(pos1, pos2, y_hbm, h_ref, info_ref, g_ref, o_ref, buf1, buf2, sem):
    i = pl.program_id(0)
    ni = pl.num_programs(0)
    tm = h_ref.shape[0]
    slot = lax.rem(i, 2)

    def start(step, s):
        _gather_rows(y_hbm, pos1, step * tm, buf1.at[s], sem.at[0, s], tm)
        _gather_rows(y_hbm, pos2, step * tm, buf2.at[s], sem.at[1, s], tm)

    @pl.when(i == 0)
    def _():
        start(0, 0)

    @pl.when(i + 1 < ni)
    def _():
        start(i + 1, 1 - slot)

    _wait_rows(y_hbm, buf1.at[slot], sem.at[0, slot], tm)
    _wait_rows(y_hbm, buf2.at[slot], sem.at[1, slot], tm)
    info = info_ref[...]
    y = h_ref[...] + info[:, 2:3] * buf1[slot] + info[:, 3:4] * buf2[slot]
    o_ref[...] = y * lax.rsqrt(jnp.mean(y * y, axis=-1, keepdims=True) + RMS_EPS) * g_ref[...]


def _moe_combine(ys, pos1, pos2, h2, info, g_final, tm=256):
    m, d = h2.shape
    grid_spec = pltpu.PrefetchScalarGridSpec(
        num_scalar_prefetch=2,
        grid=(m // tm,),
        in_specs=[pl.BlockSpec(memory_space=pl.ANY),
                  pl.BlockSpec((tm, d), lambda i, p1, p2: (i, 0)),
                  pl.BlockSpec((tm, LANES), lambda i, p1, p2: (i, 0)),
                  pl.BlockSpec((1, d), lambda i, p1, p2: (0, 0))],
        out_specs=pl.BlockSpec((tm, d), lambda i, p1, p2: (i, 0)),
        scratch_shapes=[pltpu.VMEM((2, tm, d), F32), pltpu.VMEM((2, tm, d), F32),
                        pltpu.SemaphoreType.DMA((2, 2))],
    )
    return pl.pallas_call(
        _moe_combine_kernel,
        grid_spec=grid_spec,
        out_shape=jax.ShapeDtypeStruct((m, d), F32),
        compiler_params=_cparams(("arbitrary",)),
        name="moe_combine",
    )(pos1, pos2, ys, h2, info, g_final.reshape(1, d))


def _moe_plan_kernel(info_ref, cnt_ref, pos_ref, end_ref):
    cnt = jnp.broadcast_to(cnt_ref[...], (SUBLANES, LANES))
    tiles = jnp.floor((cnt + (MOE_TILE - 1)) * (1.0 / MOE_TILE))
    upto = jnp.where(_iota((LANES, LANES), 0) <= _iota((LANES, LANES), 1), 1.0, 0.0).astype(BF16)
    tile_end = _dot(tiles.astype(BF16), upto)
    first_row = ((tile_end - tiles) * MOE_TILE)[0:1, :]
    info = info_ref[...]
    lane_f = _iota(info.shape, 1).astype(F32)

    def position(e, rank):
        return jnp.sum(jnp.where(lane_f == e, first_row, 0.0), axis=-1, keepdims=True) + rank

    pos1 = position(info[:, 0:1], info[:, 4:5])
    pos2 = position(info[:, 1:2], info[:, 5:6])
    lane = _iota(info.shape, 1)
    pos_ref[...] = jnp.where(lane == 0, pos1, jnp.where(lane == 1, pos2, 0.0)).astype(jnp.int32)
    end_ref[...] = tile_end.astype(jnp.int32)


def _moe_plan(info, counts, tm=1024):
    m = info.shape[0]
    p_max = 2 * m + N_EXPERTS * MOE_TILE
    pos, tile_end = pl.pallas_call(
        _moe_plan_kernel,
        grid=(m // tm,),
        in_specs=[pl.BlockSpec((tm, LANES), lambda i: (i, 0)), pl.BlockSpec((1, LANES), lambda i: (0, 0))],
        out_specs=[pl.BlockSpec((tm, LANES), lambda i: (i, 0)), pl.BlockSpec((SUBLANES, LANES), lambda i: (0, 0))],
        out_shape=[jax.ShapeDtypeStruct((m, LANES), jnp.int32), jax.ShapeDtypeStruct((SUBLANES, LANES), jnp.int32)],
        compiler_params=_cparams(("arbitrary",)),
        name="moe_plan",
    )(info, counts)
    return pos[:, 0], pos[:, 1], tile_end[0, :N_EXPERTS], p_max


def _pad_cols(w, n):
    return jnp.pad(w, ((0, 0), (0, n - w.shape[1])))


def _pad_rows(w, n):
    return jnp.pad(w, ((0, n - w.shape[0]), (0, 0)))


def _split_lora(w):
    o = 0
    parts = []
    for width in (DECAY_LORA, DECAY_LORA, ICLR_LORA, ICLR_LORA):
        parts.append(_pad_cols(w[:, o:o + width], LORA_PAD))
        o += width
    parts.append(w[:, o:o + GATE_LORA])
    return jnp.concatenate(parts, axis=1)


def kernel(x, mem, positions, g_mix, w_in, g_q, w_uq, g_kv, w_ukv, shift_conv, w0_f, w2_f, w0_b, w2_b, a0_f, a2_f, a0_b, a2_b, g2, k_k, k_a, r_k, lnx_g, lnx_b, w_up_mla, w_up_rwkv, w_out, g_cross, g_mem, wq_c, wkv_c, wo_c, g_ffn, w_rg, b_rg, w_re, b_re, w_eg, w_eu, w_ed, g_final):
    bsz, seq, d = x.shape
    m = bsz * seq
    depth = w_in.shape[0]
    h = x.reshape(m, d)
    pos = positions.reshape(m, 1)
    lane = np.arange(LANES)
    invf = jnp.asarray(np.where(lane < ROPE_DIM, 1.0, 0.0), F32) * (
        ROPE_BASE ** (-jnp.asarray(lane % (ROPE_DIM // 2), F32) * (2.0 / ROPE_DIM)))
    invf = invf.reshape(1, LANES)
    assert depth == 1, "the MoE kernel applies the final norm, so it must be the last layer"
    for l in range(depth):
        wi = w_in[l]
        w_mla = _pad_cols(wi[:, :MLA_IN], MLA_IN + (LANES - ROPE_DIM)).astype(BF16)
        rw = wi[:, MLA_IN:MLA_IN + RWKV_IN]
        w_rkv = rw[:, :3 * RWKV_WIDTH].astype(BF16)
        w_lora = _split_lora(rw[:, 3 * RWKV_WIDTH:]).astype(BF16)
        w_gate = wi[:, MLA_IN + RWKV_IN:].astype(BF16)
        sc = shift_conv[l]
        sc_rkv = sc[:, :3 * RWKV_WIDTH]
        sc_lora = _split_lora(sc[:, 3 * RWKV_WIDTH:])
        wq = w_uq[l].reshape(Q_LORA, MLA_HEADS, QK_DIM)
        wq = jnp.pad(wq, ((0, 0), (0, 0), (0, QK_PAD - QK_DIM))).reshape(Q_LORA, MLA_HEADS * QK_PAD)
        wkv = w_ukv[l].reshape(KV_LORA, MLA_HEADS, NOPE_DIM + V_DIM)
        wkv = jnp.concatenate([wkv[:, :, :NOPE_DIM].reshape(KV_LORA, -1),
                               wkv[:, :, NOPE_DIM:].reshape(KV_LORA, -1)], axis=1)
        lora_rows = lambda w: _pad_rows(w, LORA_PAD).astype(BF16)
        w_router = _pad_cols(jnp.concatenate(
            [jnp.moveaxis(w_re[l], 0, 1).reshape(d, N_EXPERTS), w_rg[l]], axis=1), LANES)
        b_router = _pad_cols(jnp.concatenate([b_re[l].reshape(1, N_EXPERTS), b_rg[l].reshape(1, N_GROUPS)],
                                             axis=1), LANES)

        n1 = _rmsnorm(h, g_mix[l])
        z_mla = _matmul(n1, w_mla, F32, 1024, w_mla.shape[1], name="in_proj_mla")
        z_rkv = _matmul(n1, w_rkv, F32, 1024, 512, name="in_proj_rkv")
        z_lora = _matmul(n1, w_lora, F32, 1024, w_lora.shape[1], name="in_proj_lora")
        z_gate = _matmul(n1, w_gate, F32, 1024, 512, name="in_proj_gate")

        q_cat, k_cat, v_mla = _mla_proj(z_mla, pos, invf, g_q[l], g_kv[l], wq.astype(BF16), wkv.astype(BF16))
        o_mla = _mla_attn(q_cat.reshape(bsz, seq, -1), k_cat.reshape(bsz, seq, -1),
                          v_mla.reshape(bsz, seq, -1)).reshape(m, -1)

        (r, k, v, kk, lw_f, lw_b, al_f, al_b, gate) = _rwkv_prep(
            z_rkv, z_lora, seq, sc_rkv, sc_lora, lora_rows(w2_f[l]), lora_rows(w2_b[l]),
            lora_rows(a2_f[l]), lora_rows(a2_b[l]), g2[l].astype(BF16),
            w0_f[l], w0_b[l], a0_f[l], a0_b[l], k_k[l])
        sh = lambda t: t.reshape(bsz, seq, RWKV_WIDTH)
        y_f, y_b = _rwkv_scan(sh(r), sh(k), sh(v), sh(kk), sh(lw_f), sh(lw_b), sh(al_f), sh(al_b),
                              r_k[l], k_a[l])
        o_rwkv = _rwkv_post(y_f.reshape(m, -1), y_b.reshape(m, -1), gate, lnx_g[l], lnx_b[l])

        merged = _merge(o_mla, w_up_mla[l].astype(BF16), o_rwkv, w_up_rwkv[l].astype(BF16), z_gate)
        h1 = _matmul(merged, w_out[l].astype(BF16), F32, 1024, 512, res=h, name="out_proj")

        mem_n = _rmsnorm(mem.reshape(bsz * MEM_LEN, d), g_mem[l], tm=MEM_LEN)
        kvm = _matmul(mem_n, wkv_c[l].astype(BF16), BF16, bsz * MEM_LEN, 512, name="mem_kv")
        h2, n3, info, counts = _cross_router(h1, seq, g_cross[l], wq_c[l].astype(BF16),
                                             kvm.reshape(bsz, MEM_LEN, 2 * CROSS_WIDTH),
                                             wo_c[l].astype(BF16), g_ffn[l], w_router, b_router)

        pos1, pos2, tile_end, p_max = _moe_plan(info, counts)
        flat = lambda w: w.reshape((N_EXPERTS,) + w.shape[2:])
        xs = _moe_dispatch(n3, pos1, pos2, p_max)
        ys = _moe_experts(xs, tile_end, flat(w_eg[l]), flat(w_eu[l]), flat(w_ed[l]))
        out = _moe_combine(ys, pos1, pos2, h2, info, g_final)
    return out.reshape(bsz, seq, d)
```
